```python
import jax
import jax.numpy as jnp
from jax import lax
import numpy as np

D_MODEL = 2048
BATCH = 8
SEQ = 4096
DEPTH = 4
DEC_BATCH = 16
DEC_SEQ = 32
PAST_LEN = 1024

CHUNK = 64
HEAD_DIM = 64
N_HEADS = D_MODEL // HEAD_DIM
N_KV_HEADS = N_HEADS // 4
GROUP = N_HEADS // N_KV_HEADS
N_A_LAYERS = DEPTH // 2
N_B_LAYERS = DEPTH - N_A_LAYERS
N_DENSE = (DEPTH + 1) // 2
N_MOE = DEPTH // 2
WINDOW_A = 128
PREV_CHUNKS_A = WINDOW_A // CHUNK
PREV_CHUNKS_B = 8
BAND_B = PREV_CHUNKS_B * CHUNK
REL_MIN = -(CHUNK - 1)
REL_MAX = 256
N_REL = REL_MAX - REL_MIN + 1
ROPE_THETA = 10000.0
D_FF = 5632
N_EXPERTS = 8
TOP_K = 2
D_FF_EXPERT = 2048
MOE_BLOCK = 128
RMS_EPS = 1e-6
NEG_INF = -1e30
Q_WIDTH = N_HEADS * HEAD_DIM
KV_WIDTH = N_KV_HEADS * HEAD_DIM
QKV_A_WIDTH = Q_WIDTH + 2 * KV_WIDTH

kernel_name = "yoco_streaming_swa_sink_chunkrel_moe"


def rms_norm(x, g):
    xf = x.astype(jnp.float32)
    y = xf * lax.rsqrt(jnp.mean(xf * xf, axis=-1, keepdims=True) + RMS_EPS)
    return (y * g.astype(jnp.float32)).astype(x.dtype)


def rope(x, pos):
    half = HEAD_DIM // 2
    inv_freq = 1.0 / (ROPE_THETA ** (jnp.arange(0, HEAD_DIM, 2, dtype=jnp.float32) / HEAD_DIM))
    ang = pos.astype(jnp.float32)[:, None] * inv_freq[None, :]
    cos = jnp.cos(ang)[None, :, None, :]
    sin = jnp.sin(ang)[None, :, None, :]
    xf = x.astype(jnp.float32)
    x1, x2 = xf[..., :half], xf[..., half:]
    return jnp.concatenate([x1 * cos - x2 * sin, x2 * cos + x1 * sin], axis=-1).astype(x.dtype)


def swiglu(x, wg, wu, wd):
    return (jax.nn.silu(x @ wg) * (x @ wu)) @ wd


def rel_bias(table, rel):
    idx = jnp.clip(rel, REL_MIN, REL_MAX) - REL_MIN
    b = table.astype(jnp.float32)[:, idx]
    return b.reshape(N_KV_HEADS, GROUP, rel.shape[0], rel.shape[1])


def band_attend(q, k, v, key_valid, bias, sink):
    s = jnp.einsum('bqhgd,bkhd->bhgqk', q, k).astype(jnp.float32) * (HEAD_DIM ** -0.5)
    if bias is not None:
        s = s + bias
    if key_valid is not None:
        s = jnp.where(key_valid, s, NEG_INF)
    if sink is not None:
        sink_col = jnp.broadcast_to(sink.astype(jnp.float32)[None, :, :, None, None],
                                    s.shape[:-1] + (1,))
        p = jax.nn.softmax(jnp.concatenate([s, sink_col], axis=-1), axis=-1)[..., :-1]
    else:
        p = jax.nn.softmax(s, axis=-1)
    return jnp.einsum('bhgqk,bkhd->bqhgd', p.astype(v.dtype), v)


def prompt_band_attention(q, k, v, n_prev, bias_table, sink):
    b, s = q.shape[0], q.shape[1]
    n_chunks = s // CHUNK
    pad = n_prev * CHUNK
    band = pad + CHUNK
    kp = jnp.pad(k, ((0, 0), (pad, 0), (0, 0), (0, 0)))
    vp = jnp.pad(v, ((0, 0), (pad, 0), (0, 0), (0, 0)))
    qc = jnp.moveaxis(q.reshape(b, n_chunks, CHUNK, N_KV_HEADS, GROUP, HEAD_DIM), 1, 0)
    k_off = jnp.arange(band) - pad
    q_off = jnp.arange(CHUNK)
    bias = None if bias_table is None else rel_bias(bias_table, q_off[:, None] - k_off[None, :])

    def one_chunk(args):
        qb, c = args
        kb = lax.dynamic_slice_in_dim(kp, c * CHUNK, band, axis=1)
        vb = lax.dynamic_slice_in_dim(vp, c * CHUNK, band, axis=1)
        valid = (c * CHUNK + k_off >= 0)[None, :]
        return band_attend(qb, kb, vb, valid, bias, sink)

    out = lax.map(one_chunk, (qc, jnp.arange(n_chunks)))
    return jnp.moveaxis(out, 0, 1).reshape(b, s, Q_WIDTH)


def sample_band_attention(q, k_all, v_all, n_cached, bias_table, sink):
    b, l = q.shape[0], q.shape[1]
    k_off = jnp.arange(n_cached + l) - n_cached
    q_off = jnp.arange(l)
    bias = None if bias_table is None else rel_bias(bias_table, q_off[:, None] - k_off[None, :])
    return band_attend(q, k_all, v_all, None, bias, sink).reshape(b, l, Q_WIDTH)


def a_project(h, pos, w_qkv, g_q, g_k):
    b, l, _ = h.shape
    qkv = h @ w_qkv
    q = qkv[..., :Q_WIDTH].reshape(b, l, N_HEADS, HEAD_DIM)
    k = qkv[..., Q_WIDTH:Q_WIDTH + KV_WIDTH].reshape(b, l, N_KV_HEADS, HEAD_DIM)
    v = qkv[..., Q_WIDTH + KV_WIDTH:].reshape(b, l, N_KV_HEADS, HEAD_DIM)
    q = rope(rms_norm(q, g_q), pos).reshape(b, l, N_KV_HEADS, GROUP, HEAD_DIM)
    k = rope(rms_norm(k, g_k), pos)
    return q, k, v


def b_shared_kv(x, g_kv, w_kv, g_k):
    b, l, _ = x.shape
    kv = rms_norm(x, g_kv) @ w_kv
    k = rms_norm(kv[..., :KV_WIDTH].reshape(b, l, N_KV_HEADS, HEAD_DIM), g_k)
    v = kv[..., KV_WIDTH:].reshape(b, l, N_KV_HEADS, HEAD_DIM)
    return k, v


def b_query(h, w_q, g_q):
    b, l, _ = h.shape
    q = rms_norm((h @ w_q).reshape(b, l, N_HEADS, HEAD_DIM), g_q)
    return q.reshape(b, l, N_KV_HEADS, GROUP, HEAD_DIM)


def moe_swiglu(h, w_router, b_router, w_gate, w_up, w_down):
    shp = h.shape
    x = h.reshape(-1, shp[-1])
    t = x.shape[0]
    n_assign = t * TOP_K
    logits = (x @ w_router).astype(jnp.float32) + b_router.astype(jnp.float32)
    top_logit, top_e = lax.top_k(logits, TOP_K)
    gates = jax.nn.softmax(top_logit, axis=-1)
    flat_e = top_e.reshape(-1)
    flat_tok = jnp.arange(n_assign) // TOP_K
    order = jnp.argsort(flat_e)
    e_sorted = flat_e[order]
    tok_sorted = flat_tok[order]
    gate_sorted = gates.reshape(-1)[order]
    counts = jnp.bincount(flat_e, length=N_EXPERTS)
    padded = (counts + MOE_BLOCK - 1) // MOE_BLOCK * MOE_BLOCK
    start = jnp.cumsum(counts) - counts
    pstart = jnp.cumsum(padded) - padded
    dest = pstart[e_sorted] + jnp.arange(n_assign) - start[e_sorted]
    n_blocks = (n_assign + N_EXPERTS * (MOE_BLOCK - 1) + MOE_BLOCK - 1) // MOE_BLOCK
    rows = n_blocks * MOE_BLOCK
    buf_tok = jnp.full((rows,), t, dtype=jnp.int32).at[dest].set(tok_sorted)
    x_pad = jnp.concatenate([x, jnp.zeros((1, x.shape[1]), x.dtype)], axis=0)
    xb = x_pad[buf_tok].reshape(n_blocks, MOE_BLOCK, x.shape[1])
    blk_e = jnp.minimum(
        jnp.searchsorted(jnp.cumsum(padded), jnp.arange(n_blocks) * MOE_BLOCK, side='right'),
        N_EXPERTS - 1)

    def expert_block(args):
        xblk, e = args
        return swiglu(xblk, w_gate[e], w_up[e], w_down[e])

    yb = lax.map(expert_block, (xb, blk_e)).reshape(rows, x.shape[1])
    y = jnp.zeros((t, x.shape[1]), jnp.float32).at[tok_sorted].add(
        gate_sorted[:, None] * yb[dest].astype(jnp.float32))
    return y.astype(h.dtype).reshape(shp)


def setup_inputs(seed: int = 0) -> dict:
    key = jax.random.key(seed)
    ks = jax.random.split(key, 28)
    f32 = jnp.float32

    def nrm(i, shape, scale):
        return jax.random.normal(ks[i], shape, f32) * scale

    def gain(i, shape):
        return 1.0 + nrm(i, shape, 0.02)

    la = min(WINDOW_A, PAST_LEN)
    lb = min(BAND_B, PAST_LEN)
    res_scale = (2 * DEPTH) ** -0.5
    return {
        "x_prompt": nrm(0, (BATCH, SEQ, D_MODEL), 1.0),
        "x_sample": nrm(1, (DEC_BATCH, DEC_SEQ, D_MODEL), 1.0),
        "cache_k_a": nrm(2, (N_A_LAYERS, DEC_BATCH, la, N_KV_HEADS, HEAD_DIM), 1.0),
        "cache_v_a": nrm(3, (N_A_LAYERS, DEC_BATCH, la, N_KV_HEADS, HEAD_DIM), 1.0),
        "cache_k_b": nrm(4, (DEC_BATCH, lb, N_KV_HEADS, HEAD_DIM), 1.0),
        "cache_v_b": nrm(5, (DEC_BATCH, lb, N_KV_HEADS, HEAD_DIM), 1.0),
        "g_attn": gain(6, (DEPTH, D_MODEL)),
        "g_ffn": gain(7, (DEPTH, D_MODEL)),
        "w_qkv_a": nrm(8, (N_A_LAYERS, D_MODEL, QKV_A_WIDTH), D_MODEL ** -0.5),
        "g_q_a": gain(9, (N_A_LAYERS, HEAD_DIM)),
        "g_k_a": gain(10, (N_A_LAYERS, HEAD_DIM)),
        "sink_a": nrm(11, (N_A_LAYERS, N_HEADS), 0.5),
        "w_o_a": nrm(12, (N_A_LAYERS, Q_WIDTH, D_MODEL), Q_WIDTH ** -0.5 * res_scale),
        "g_kv_b": gain(13, (D_MODEL,)),
        "w_kv_b": nrm(14, (D_MODEL, 2 * KV_WIDTH), D_MODEL ** -0.5),
        "g_k_b": gain(15, (HEAD_DIM,)),
        "w_q_b": nrm(16, (N_B_LAYERS, D_MODEL, Q_WIDTH), D_MODEL ** -0.5),
        "g_q_b": gain(17, (N_B_LAYERS, HEAD_DIM)),
        "rel_bias_b": nrm(18, (N_B_LAYERS, N_HEADS, N_REL), 0.1),
        "w_o_b": nrm(19, (N_B_LAYERS, Q_WIDTH, D_MODEL), Q_WIDTH ** -0.5 * res_scale),
        "w_gate_ffn": nrm(20, (N_DENSE, D_MODEL, D_FF), D_MODEL ** -0.5),
        "w_up_ffn": nrm(21, (N_DENSE, D_MODEL, D_FF), D_MODEL ** -0.5),
        "w_down_ffn": nrm(22, (N_DENSE, D_FF, D_MODEL), D_FF ** -0.5 * res_scale),
        "w_router": nrm(23, (N_MOE, D_MODEL, N_EXPERTS), D_MODEL ** -0.5),
        "b_router": nrm(24, (N_MOE, N_EXPERTS), 0.01),
        "w_gate_moe": nrm(25, (N_MOE, N_EXPERTS, D_MODEL, D_FF_EXPERT), D_MODEL ** -0.5),
        "w_up_moe": nrm(26, (N_MOE, N_EXPERTS, D_MODEL, D_FF_EXPERT), D_MODEL ** -0.5),
        "w_down_moe": nrm(27, (N_MOE, N_EXPERTS, D_FF_EXPERT, D_MODEL), D_FF_EXPERT ** -0.5 * res_scale),
    }


def reference(x_prompt, x_sample, cache_k_a, cache_v_a, cache_k_b, cache_v_b,
              g_attn, g_ffn, w_qkv_a, g_q_a, g_k_a, sink_a, w_o_a,
              g_kv_b, w_kv_b, g_k_b, w_q_b, g_q_b, rel_bias_b, w_o_b,
              w_gate_ffn, w_up_ffn, w_down_ffn,
              w_router, b_router, w_gate_moe, w_up_moe, w_down_moe):
    seq_p = x_prompt.shape[1]
    seq_s = x_sample.shape[1]
    pos_p = jnp.arange(seq_p)
    pos_s = PAST_LEN + jnp.arange(seq_s)
    n_cache_a = cache_k_a.shape[2]
    n_cache_b = cache_k_b.shape[1]
    keep_a_p = min(WINDOW_A, seq_p)
    keep_b_p = min(BAND_B, seq_p)

    xp, xs = x_prompt, x_sample
    ka_p, va_p, ka_s, va_s = [], [], [], []
    for layer in range(DEPTH):
        hp = rms_norm(xp, g_attn[layer])
        hs = rms_norm(xs, g_attn[layer])
        if layer < N_A_LAYERS:
            i = layer
            sink = sink_a[i].reshape(N_KV_HEADS, GROUP)
            qp, kp, vp = a_project(hp, pos_p, w_qkv_a[i], g_q_a[i], g_k_a[i])
            qs, ks, vs = a_project(hs, pos_s, w_qkv_a[i], g_q_a[i], g_k_a[i])
            k_all = jnp.concatenate([cache_k_a[i], ks], axis=1)
            v_all = jnp.concatenate([cache_v_a[i], vs], axis=1)
            op = prompt_band_attention(qp, kp, vp, PREV_CHUNKS_A, None, sink)
            o_s = sample_band_attention(qs, k_all, v_all, n_cache_a, None, sink)
            xp = xp + op @ w_o_a[i]
            xs = xs + o_s @ w_o_a[i]
            ka_p.append(kp[:, seq_p - keep_a_p:])
            va_p.append(vp[:, seq_p - keep_a_p:])
            ka_s.append(k_all[:, seq_s:])
            va_s.append(v_all[:, seq_s:])
        else:
            j = layer - N_A_LAYERS
            qp = b_query(hp, w_q_b[j], g_q_b[j])
            qs = b_query(hs, w_q_b[j], g_q_b[j])
            op = prompt_band_attention(qp, kb_p, vb_p, PREV_CHUNKS_B, rel_bias_b[j], None)
            o_s = sample_band_attention(qs, kb_all, vb_all, n_cache_b, rel_bias_b[j], None)
            xp = xp + op @ w_o_b[j]
            xs = xs + o_s @ w_o_b[j]

        hp = rms_norm(xp, g_ffn[layer])
        hs = rms_norm(xs, g_ffn[layer])
        m = layer // 2
        if layer % 2 == 0:
            xp = xp + swiglu(hp, w_gate_ffn[m], w_up_ffn[m], w_down_ffn[m])
            xs = xs + swiglu(hs, w_gate_ffn[m], w_up_ffn[m], w_down_ffn[m])
        else:
            xp = xp + moe_swiglu(hp, w_router[m], b_router[m], w_gate_moe[m], w_up_moe[m], w_down_moe[m])
            xs = xs + moe_swiglu(hs, w_router[m], b_router[m], w_gate_moe[m], w_up_moe[m], w_down_moe[m])

        if layer == N_A_LAYERS - 1:
            kb_p, vb_p = b_shared_kv(xp, g_kv_b, w_kv_b, g_k_b)
            kb_s, vb_s = b_shared_kv(xs, g_kv_b, w_kv_b, g_k_b)
            kb_all = jnp.concatenate([cache_k_b, kb_s], axis=1)
            vb_all = jnp.concatenate([cache_v_b, vb_s], axis=1)

    new_k_a_p = jnp.stack(ka_p)
    new_v_a_p = jnp.stack(va_p)
    new_k_a_s = jnp.stack(ka_s)
    new_v_a_s = jnp.stack(va_s)
    new_k_b_p = kb_p[:, seq_p - keep_b_p:]
    new_v_b_p = vb_p[:, seq_p - keep_b_p:]
    new_k_b_s = kb_all[:, seq_s:]
    new_v_b_s = vb_all[:, seq_s:]
    return (xp, xs, new_k_a_p, new_v_a_p, new_k_b_p, new_v_b_p,
            new_k_a_s, new_v_a_s, new_k_b_s, new_v_b_s)
```

```python
import functools

import jax
import jax.numpy as jnp
from jax import lax
from jax.experimental import pallas as pl
from jax.experimental.pallas import tpu as pltpu

CHUNK = 64
HEAD_DIM = 64
GROUP = 4
WINDOW_A = 128
PREV_CHUNKS_A = WINDOW_A // CHUNK
PREV_CHUNKS_B = 8
BAND_B = PREV_CHUNKS_B * CHUNK
REL_MIN = -(CHUNK - 1)
REL_MAX = 256
ROPE_THETA = 10000.0
N_EXPERTS = 8
TOP_K = 2
RMS_EPS = 1e-6
NEG_INF = -1e30
PAST_LEN = 1024

LANES = 128
HEADS_PER_VREG = LANES // HEAD_DIM
KV_GROUP_WIDTH = GROUP * HEAD_DIM

ROW_TILE = 512
COL_TILE = 512
MOE_ROWS = 512
COMBINE_ROWS = 256
Q_CHUNKS_PER_STEP = 8
VMEM_LIMIT = 56 * 1024 * 1024


def _params(semantics):
    return pltpu.CompilerParams(dimension_semantics=semantics, vmem_limit_bytes=VMEM_LIMIT)


def _rms_rows(x, g):
    ms = jnp.sum(x * x, axis=-1, keepdims=True) * (1.0 / x.shape[-1])
    return (x * lax.rsqrt(ms + RMS_EPS)) * g


def _head_norm(y, gain, cos, sin):
    lane = lax.broadcasted_iota(jnp.int32, y.shape, 1)
    low = lane < HEAD_DIM
    ss = y * y
    s_lo = jnp.sum(jnp.where(low, ss, 0.0), axis=-1, keepdims=True)
    s_hi = jnp.sum(jnp.where(low, 0.0, ss), axis=-1, keepdims=True)
    ms = jnp.where(low, s_lo, s_hi) * (1.0 / HEAD_DIM)
    yn = (y * lax.rsqrt(ms + RMS_EPS)) * gain
    if cos is None:
        return yn
    half = HEAD_DIM // 2
    fwd = pltpu.roll(yn, half, 1)
    bwd = pltpu.roll(yn, LANES - half, 1)
    rot = jnp.where((lane % HEAD_DIM) < half, bwd, fwd)
    return yn * cos + rot * sin


def _proj_kernel(*refs, nq, has_kv, rope):
    it = iter(refs)
    x_ref, g_ref, w_ref = next(it), next(it), next(it)
    gq_ref = next(it) if nq else None
    gk_ref = next(it) if has_kv else None
    cos_ref, sin_ref = (next(it), next(it)) if rope else (None, None)
    q_ref = next(it) if nq else None
    if has_kv:
        kb_ref, vb_ref, kf_ref, vf_ref = next(it), next(it), next(it), next(it)
    h_scr = next(it)
    j = pl.program_id(1)

    @pl.when(j == 0)
    def _():
        h_scr[...] = _rms_rows(x_ref[...], g_ref[...]).astype(h_scr.dtype)

    y = jnp.dot(h_scr[...], w_ref[...], preferred_element_type=jnp.float32)
    cos = cos_ref[...] if rope else None
    sin = sin_ref[...] if rope else None

    def normed(gain_ref, scale):
        outs = []
        for c in range(COL_TILE // LANES):
            z = _head_norm(y[:, c * LANES:(c + 1) * LANES], gain_ref[...], cos, sin)
            outs.append(z * scale if scale != 1.0 else z)
        return outs

    if nq:
        @pl.when(j < nq)
        def _():
            for c, z in enumerate(normed(gq_ref, HEAD_DIM ** -0.5)):
                q_ref[:, c * LANES:(c + 1) * LANES] = z.astype(q_ref.dtype)

    if has_kv:
        @pl.when(j == nq)
        def _():
            for c, z in enumerate(normed(gk_ref, 1.0)):
                kf_ref[:, c * LANES:(c + 1) * LANES] = z
                kb_ref[:, c * LANES:(c + 1) * LANES] = z.astype(kb_ref.dtype)

        @pl.when(j == nq + 1)
        def _():
            vf_ref[...] = y
            vb_ref[...] = y.astype(vb_ref.dtype)


def _project(x, g, w, gq, gk, cos, sin, *, nq, has_kv, rope):
    t, d = x.shape
    tm = ROW_TILE
    n_col = nq + (2 if has_kv else 0)
    assert w.shape == (d, n_col * COL_TILE) and t % tm == 0
    row = lambda i, j: (i, 0)
    in_specs = [pl.BlockSpec((tm, d), row),
                pl.BlockSpec((1, d), lambda i, j: (0, 0)),
                pl.BlockSpec((d, COL_TILE), lambda i, j: (0, j))]
    args = [x, g, w]
    vec = pl.BlockSpec((1, LANES), lambda i, j: (0, 0))
    if nq:
        in_specs.append(vec)
        args.append(gq)
    if has_kv:
        in_specs.append(vec)
        args.append(gk)
    if rope:
        in_specs += [pl.BlockSpec((tm, LANES), row)] * 2
        args += [cos, sin]
    out_shape, out_specs = [], []
    if nq:
        out_shape.append(jax.ShapeDtypeStruct((t, nq * COL_TILE), jnp.bfloat16))
        out_specs.append(pl.BlockSpec((tm, COL_TILE), lambda i, j: (i, jnp.minimum(j, nq - 1))))
    if has_kv:
        for dt in (jnp.bfloat16, jnp.bfloat16, jnp.float32, jnp.float32):
            out_shape.append(jax.ShapeDtypeStruct((t, COL_TILE), dt))
            out_specs.append(pl.BlockSpec((tm, COL_TILE), row))
    return pl.pallas_call(
        functools.partial(_proj_kernel, nq=nq, has_kv=has_kv, rope=rope),
        grid=(t // tm, n_col),
        in_specs=in_specs,
        out_specs=out_specs,
        out_shape=out_shape,
        scratch_shapes=[pltpu.VMEM((tm, d), jnp.bfloat16)],
        compiler_params=_params(("parallel", "arbitrary")),
        name="proj",
    )(*args)


def _attn_kernel(*refs, lq, band, kstep, pad, n_chunks, has_bias, has_sink, n_kv):
    it = iter(refs)
    q_ref, k_ref, v_ref = next(it), next(it), next(it)
    bias_ref = next(it) if has_bias else None
    sink_ref = next(it) if has_sink else None
    o_ref = next(it)
    qb = pl.program_id(1)
    kidx = lax.broadcasted_iota(jnp.int32, (1, band), 1)

    def chunk(cc, carry):
        c = qb * n_chunks + cc
        r0 = pl.multiple_of(cc * lq, lq)
        k0 = pl.multiple_of(c * kstep, CHUNK) if kstep else 0
        valid = kidx >= (pad - c * kstep)
        for j in range(n_kv):
            qj = q_ref[0, pl.ds(r0, lq), j * KV_GROUP_WIDTH:(j + 1) * KV_GROUP_WIDTH]
            qcat = jnp.concatenate(
                [qj[:, g * HEAD_DIM:(g + 1) * HEAD_DIM] for g in range(GROUP)], axis=0)
            kj = k_ref[0, pl.ds(k0, band), j * HEAD_DIM:(j + 1) * HEAD_DIM]
            vj = v_ref[0, pl.ds(k0, band), j * HEAD_DIM:(j + 1) * HEAD_DIM]
            s = lax.dot_general(qcat, kj, (((1,), (1,)), ((), ())),
                                preferred_element_type=jnp.float32)
            if has_bias:
                s = s + bias_ref[j]
            if pad:
                s = jnp.where(valid, s, NEG_INF)
            m = jnp.max(s, axis=-1, keepdims=True)
            if has_sink:
                sink = sink_ref[j]
                m = jnp.maximum(m, sink)
            e = jnp.exp(s - m)
            l = jnp.sum(e, axis=-1, keepdims=True)
            if has_sink:
                l = l + jnp.exp(sink - m)
            o = jnp.dot(e.astype(vj.dtype), vj, preferred_element_type=jnp.float32)
            o = o / l
            ocat = jnp.concatenate([o[g * lq:(g + 1) * lq] for g in range(GROUP)], axis=1)
            o_ref[0, pl.ds(r0, lq), j * KV_GROUP_WIDTH:(j + 1) * KV_GROUP_WIDTH] = (
                ocat.astype(o_ref.dtype))
        return carry

    lax.fori_loop(0, n_chunks, chunk, 0)


def _band_attention(q, k, v, bias, sink, *, lq, band, kstep, pad, n_chunks):
    b, l, qw = q.shape
    lk, kw = k.shape[1], k.shape[2]
    n_kv = kw // HEAD_DIM
    rows = lq * n_chunks
    assert l % rows == 0
    in_specs = [pl.BlockSpec((1, rows, qw), lambda i, j: (i, j, 0)),
                pl.BlockSpec((1, lk, kw), lambda i, j: (i, 0, 0)),
                pl.BlockSpec((1, lk, kw), lambda i, j: (i, 0, 0))]
    args = [q, k, v]
    if bias is not None:
        in_specs.append(pl.BlockSpec(bias.shape, lambda i, j: (0, 0, 0)))
        args.append(bias)
    if sink is not None:
        in_specs.append(pl.BlockSpec(sink.shape, lambda i, j: (0, 0, 0)))
        args.append(sink)
    return pl.pallas_call(
        functools.partial(_attn_kernel, lq=lq, band=band, kstep=kstep, pad=pad,
                          n_chunks=n_chunks, has_bias=bias is not None,
                          has_sink=sink is not None, n_kv=n_kv),
        grid=(b, l // rows),
        in_specs=in_specs,
        out_specs=pl.BlockSpec((1, rows, qw), lambda i, j: (i, j, 0)),
        out_shape=jax.ShapeDtypeStruct(q.shape, jnp.bfloat16),
        compiler_params=_params(("parallel", "arbitrary")),
        name="band_attn",
    )(*args)


def _out_proj_kernel(o_ref, w_ref, x_ref, y_ref):
    y_ref[...] = x_ref[...] + jnp.dot(o_ref[...], w_ref[...],
                                      preferred_element_type=jnp.float32)


def _out_proj(o, w, x):
    t, d = x.shape
    tm = ROW_TILE
    return pl.pallas_call(
        _out_proj_kernel,
        grid=(t // tm, d // COL_TILE),
        in_specs=[pl.BlockSpec((tm, o.shape[1]), lambda i, j: (i, 0)),
                  pl.BlockSpec((o.shape[1], COL_TILE), lambda i, j: (0, j)),
                  pl.BlockSpec((tm, COL_TILE), lambda i, j: (i, j))],
        out_specs=pl.BlockSpec((tm, COL_TILE), lambda i, j: (i, j)),
        out_shape=jax.ShapeDtypeStruct((t, d), jnp.float32),
        compiler_params=_params(("parallel", "arbitrary")),
        name="out_proj",
    )(o, w, x)


def _ffn_kernel(e_ref, x_ref, g_ref, wg_ref, wu_ref, wd_ref, y_ref, h_scr, acc_scr, *,
                residual):
    f = pl.program_id(1)

    @pl.when(f == 0)
    def _():
        h_scr[...] = _rms_rows(x_ref[...], g_ref[...]).astype(h_scr.dtype)

    h = h_scr[...]
    gate = jnp.dot(h, wg_ref[...], preferred_element_type=jnp.float32)
    up = jnp.dot(h, wu_ref[...], preferred_element_type=jnp.float32)
    act = (gate * jax.nn.sigmoid(gate)) * up
    part = jnp.dot(act.astype(jnp.bfloat16), wd_ref[...], preferred_element_type=jnp.float32)

    @pl.when(f == 0)
    def _():
        acc_scr[...] = part

    @pl.when(f > 0)
    def _():
        acc_scr[...] += part

    @pl.when(f == pl.num_programs(1) - 1)
    def _():
        y_ref[...] = (x_ref[...] + acc_scr[...]) if residual else acc_scr[...]


def _swiglu_blocks(x, g, wg, wu, wd, block_expert, *, tm, residual):
    r, d = x.shape
    ff = wg.shape[2]
    assert r % tm == 0 and ff % COL_TILE == 0
    grid_spec = pltpu.PrefetchScalarGridSpec(
        num_scalar_prefetch=1,
        grid=(r // tm, ff // COL_TILE),
        in_specs=[pl.BlockSpec((tm, d), lambda i, f, e: (i, 0)),
                  pl.BlockSpec((1, d), lambda i, f, e: (0, 0)),
                  pl.BlockSpec((None, d, COL_TILE), lambda i, f, e: (e[i], 0, f)),
                  pl.BlockSpec((None, d, COL_TILE), lambda i, f, e: (e[i], 0, f)),
                  pl.BlockSpec((None, COL_TILE, d), lambda i, f, e: (e[i], f, 0))],
        out_specs=pl.BlockSpec((tm, d), lambda i, f, e: (i, 0)),
        scratch_shapes=[pltpu.VMEM((tm, d), jnp.bfloat16),
                        pltpu.VMEM((tm, d), jnp.float32)],
    )
    return pl.pallas_call(
        functools.partial(_ffn_kernel, residual=residual),
        grid_spec=grid_spec,
        out_shape=jax.ShapeDtypeStruct((r, d), jnp.float32),
        compiler_params=_params(("parallel", "arbitrary")),
        name="swiglu",
    )(block_expert, x, g, wg, wu, wd)


def _router_kernel(x_ref, g_ref, w_ref, b_ref, idx_ref, gate_ref):
    h = _rms_rows(x_ref[...], g_ref[...]).astype(jnp.bfloat16)
    logits = jnp.dot(h, w_ref[...], preferred_element_type=jnp.float32) + b_ref[...]
    lane = lax.broadcasted_iota(jnp.int32, logits.shape, 1)
    m1 = jnp.max(logits, axis=-1, keepdims=True)
    e1 = jnp.min(jnp.where(logits == m1, lane, LANES), axis=-1, keepdims=True)
    rest = jnp.where(lane == e1, -jnp.inf, logits)
    m2 = jnp.max(rest, axis=-1, keepdims=True)
    e2 = jnp.min(jnp.where(rest == m2, lane, LANES), axis=-1, keepdims=True)
    t = jnp.exp(m2 - m1)
    denom = 1.0 + t
    idx_ref[...] = jnp.where(lane == 0, e1, jnp.where(lane == 1, e2, 0))
    gate_ref[...] = jnp.where(lane == 0, 1.0 / denom, jnp.where(lane == 1, t / denom, 0.0))


def _route(x, g, w_router, b_router):
    t, d = x.shape
    tm = ROW_TILE
    n_e = w_router.shape[1]
    w = jnp.zeros((d, LANES), jnp.bfloat16).at[:, :n_e].set(w_router.astype(jnp.bfloat16))
    b = jnp.full((1, LANES), -jnp.inf, jnp.float32).at[0, :n_e].set(b_router.astype(jnp.float32))
    return pl.pallas_call(
        _router_kernel,
        grid=(t // tm,),
        in_specs=[pl.BlockSpec((tm, d), lambda i: (i, 0)),
                  pl.BlockSpec((1, d), lambda i: (0, 0)),
                  pl.BlockSpec((d, LANES), lambda i: (0, 0)),
                  pl.BlockSpec((1, LANES), lambda i: (0, 0))],
        out_specs=[pl.BlockSpec((tm, LANES), lambda i: (i, 0))] * 2,
        out_shape=[jax.ShapeDtypeStruct((t, LANES), jnp.int32),
                   jax.ShapeDtypeStruct((t, LANES), jnp.float32)],
        compiler_params=_params(("parallel",)),
        name="router",
    )(x, g, w, b)


def _gather_kernel(idx_ref, src_ref, dst_ref, sem, *, rows):
    def row_copy(r):
        return pltpu.make_async_copy(src_ref.at[pl.ds(idx_ref[r], 1)],
                                     dst_ref.at[pl.ds(r, 1)], sem)

    def issue(r, carry):
        row_copy(r).start()
        return carry

    lax.fori_loop(0, rows, issue, 0)

    def drain(r, carry):
        row_copy(r).wait()
        return carry

    lax.fori_loop(0, rows, drain, 0)


def _gather_rows(src, idx, *, rows):
    n, d = src.shape
    r = idx.shape[0]
    assert r % rows == 0
    return pl.pallas_call(
        functools.partial(_gather_kernel, rows=rows),
        grid=(r // rows,),
        in_specs=[pl.BlockSpec((rows,), lambda i: (i,), memory_space=pltpu.SMEM),
                  pl.BlockSpec(memory_space=pl.ANY)],
        out_specs=pl.BlockSpec((rows, d), lambda i: (i, 0)),
        out_shape=jax.ShapeDtypeStruct((r, d), src.dtype),
        scratch_shapes=[pltpu.SemaphoreType.DMA(())],
        compiler_params=_params(("arbitrary",)),
        name="gather_rows",
    )(idx, src)


def _combine_kernel(d0_ref, d1_ref, x_ref, gate_ref, y_hbm, out_ref, buf0, buf1, sem, *, rows):
    def copies(r):
        return (pltpu.make_async_copy(y_hbm.at[pl.ds(d0_ref[r], 1)], buf0.at[pl.ds(r, 1)],
                                      sem.at[0]),
                pltpu.make_async_copy(y_hbm.at[pl.ds(d1_ref[r], 1)], buf1.at[pl.ds(r, 1)],
                                      sem.at[1]))

    def issue(r, carry):
        c0, c1 = copies(r)
        c0.start()
        c1.start()
        return carry

    lax.fori_loop(0, rows, issue, 0)

    def drain(r, carry):
        c0, c1 = copies(r)
        c0.wait()
        c1.wait()
        return carry

    lax.fori_loop(0, rows, drain, 0)
    gates = gate_ref[...]
    y = gates[:, 0:1] * buf0[...] + gates[:, 1:2] * buf1[...]
    out_ref[...] = x_ref[...] + y


def _combine(x, gates, y_rows, d0, d1):
    t, d = x.shape
    rows = COMBINE_ROWS
    assert t % rows == 0
    idx_spec = pl.BlockSpec((rows,), lambda i: (i,), memory_space=pltpu.SMEM)
    return pl.pallas_call(
        functools.partial(_combine_kernel, rows=rows),
        grid=(t // rows,),
        in_specs=[idx_spec, idx_spec,
                  pl.BlockSpec((rows, d), lambda i: (i, 0)),
                  pl.BlockSpec((rows, LANES), lambda i: (i, 0)),
                  pl.BlockSpec(memory_space=pl.ANY)],
        out_specs=pl.BlockSpec((rows, d), lambda i: (i, 0)),
        out_shape=jax.ShapeDtypeStruct((t, d), jnp.float32),
        scratch_shapes=[pltpu.VMEM((rows, d), jnp.float32),
                        pltpu.VMEM((rows, d), jnp.float32),
                        pltpu.SemaphoreType.DMA((2,))],
        compiler_params=_params(("arbitrary",)),
        name="combine",
    )(d0, d1, x, gates, y_rows)


def _moe(x, g, w_router, b_router, wg, wu, wd):
    t, d = x.shape
    n_e = wg.shape[0]
    n_assign = t * TOP_K
    top_idx, gates = _route(x, g, w_router, b_router)
    flat_e = top_idx[:, :TOP_K].reshape(-1)
    onehot = (flat_e[:, None] == jnp.arange(n_e, dtype=jnp.int32)[None, :]).astype(jnp.int32)
    rank = jnp.take_along_axis(jnp.cumsum(onehot, axis=0) - onehot, flat_e[:, None], axis=1)[:, 0]
    counts = jnp.sum(onehot, axis=0)
    padded = (counts + MOE_ROWS - 1) // MOE_ROWS * MOE_ROWS
    ends = jnp.cumsum(padded)
    dest = (ends - padded)[flat_e] + rank
    n_blocks = (n_assign + n_e * (MOE_ROWS - 1) + MOE_ROWS - 1) // MOE_ROWS
    rows = n_blocks * MOE_ROWS
    flat_tok = jnp.arange(n_assign, dtype=jnp.int32) // TOP_K
    row_tok = jnp.zeros((rows,), jnp.int32).at[dest].set(flat_tok)
    block_expert = jnp.minimum(
        jnp.searchsorted(ends, jnp.arange(n_blocks, dtype=jnp.int32) * MOE_ROWS, side='right'),
        n_e - 1).astype(jnp.int32)
    x_rows = _gather_rows(x, row_tok, rows=MOE_ROWS)
    y_rows = _swiglu_blocks(x_rows, g, wg, wu, wd, block_expert, tm=MOE_ROWS, residual=False)
    dest2 = dest.reshape(t, TOP_K).astype(jnp.int32)
    return _combine(x, gates, y_rows, dest2[:, 0], dest2[:, 1])


def _rope_tables(pos):
    half = HEAD_DIM // 2
    inv_freq = 1.0 / (ROPE_THETA ** (jnp.arange(0, HEAD_DIM, 2, dtype=jnp.float32) / HEAD_DIM))
    ang = pos.astype(jnp.float32)[:, None] * inv_freq[None, :]
    cos, sin = jnp.cos(ang), jnp.sin(ang)
    reps = LANES // half
    signs = jnp.tile(jnp.concatenate([-jnp.ones((half,)), jnp.ones((half,))]), HEADS_PER_VREG)
    return jnp.tile(cos, (1, reps)), jnp.tile(sin, (1, reps)) * signs[None, :].astype(jnp.float32)


def _rel_bias_rows(table, lq, n_before):
    rel = jnp.arange(lq)[:, None] - (jnp.arange(n_before + lq) - n_before)[None, :]
    idx = jnp.clip(rel, REL_MIN, REL_MAX) - REL_MIN
    b = table.astype(jnp.float32)[:, idx]
    h = table.shape[0]
    return b.reshape(h // GROUP, GROUP * lq, n_before + lq)


def _sink_rows(sink, lq):
    h = sink.shape[0]
    s = jnp.repeat(sink.astype(jnp.float32).reshape(h // GROUP, GROUP, 1), lq, axis=2)
    return s.reshape(h // GROUP, GROUP * lq, 1)


def _head_gain(g):
    return jnp.tile(g.astype(jnp.float32), HEADS_PER_VREG)[None, :]


def kernel(x_prompt, x_sample, cache_k_a, cache_v_a, cache_k_b, cache_v_b, g_attn, g_ffn, w_qkv_a, g_q_a, g_k_a, sink_a, w_o_a, g_kv_b, w_kv_b, g_k_b, w_q_b, g_q_b, rel_bias_b, w_o_b, w_gate_ffn, w_up_ffn, w_down_ffn, w_router, b_router, w_gate_moe, w_up_moe, w_down_moe):
    bf16 = jnp.bfloat16
    nb, seq, d = x_prompt.shape
    ns, seq_s, _ = x_sample.shape
    tp, ts = nb * seq, ns * seq_s
    t = tp + ts
    depth = g_attn.shape[0]
    n_a = w_qkv_a.shape[0]
    n_kv = cache_k_b.shape[2]
    kvw = n_kv * HEAD_DIM
    n_cache_a, n_cache_b = cache_k_a.shape[2], cache_k_b.shape[1]
    keep_a, keep_b = min(WINDOW_A, seq), min(BAND_B, seq)
    assert t % ROW_TILE == 0 and seq % (CHUNK * Q_CHUNKS_PER_STEP) == 0

    x = jnp.concatenate([x_prompt.reshape(tp, d), x_sample.reshape(ts, d)], axis=0)
    pos = jnp.concatenate([jnp.tile(jnp.arange(seq), nb),
                           jnp.tile(PAST_LEN + jnp.arange(seq_s), ns)])
    cos, sin = _rope_tables(pos)
    row = lambda v: v.astype(jnp.float32)[None, :]

    def attend(q, k, v, cache_k, cache_v, n_prev, table, sink):
        pad = n_prev * CHUNK
        kp = jnp.pad(k[:tp].reshape(nb, seq, kvw), ((0, 0), (pad, 0), (0, 0)))
        vp = jnp.pad(v[:tp].reshape(nb, seq, kvw), ((0, 0), (pad, 0), (0, 0)))
        bias_p = None if table is None else _rel_bias_rows(table, CHUNK, pad)
        sink_p = None if sink is None else _sink_rows(sink, CHUNK)
        op = _band_attention(q[:tp].reshape(nb, seq, -1), kp, vp, bias_p, sink_p,
                             lq=CHUNK, band=pad + CHUNK, kstep=CHUNK, pad=pad,
                             n_chunks=Q_CHUNKS_PER_STEP)
        n_cached = cache_k.shape[1]
        ks = jnp.concatenate([cache_k.reshape(ns, n_cached, kvw).astype(bf16),
                              k[tp:].reshape(ns, seq_s, kvw)], axis=1)
        vs = jnp.concatenate([cache_v.reshape(ns, n_cached, kvw).astype(bf16),
                              v[tp:].reshape(ns, seq_s, kvw)], axis=1)
        bias_s = None if table is None else _rel_bias_rows(table, seq_s, n_cached)
        sink_s = None if sink is None else _sink_rows(sink, seq_s)
        o_s = _band_attention(q[tp:].reshape(ns, seq_s, -1), ks, vs, bias_s, sink_s,
                              lq=seq_s, band=n_cached + seq_s, kstep=0, pad=0, n_chunks=1)
        return jnp.concatenate([op.reshape(tp, -1), o_s.reshape(ts, -1)], axis=0)

    def tail(c, n_keep):
        return c[:tp].reshape(nb, seq, n_kv, HEAD_DIM)[:, seq - n_keep:]

    def rolled(cache, c):
        new = c[tp:].reshape(ns, seq_s, n_kv, HEAD_DIM)
        return jnp.concatenate([cache, new], axis=1)[:, seq_s:]

    ka_p, va_p, ka_s, va_s = [], [], [], []
    kb = vb = kb_f = vb_f = None
    for layer in range(depth):
        g_in = row(g_attn[layer])
        if layer < n_a:
            i = layer
            q, k_b16, v_b16, k_f32, v_f32 = _project(
                x, g_in, w_qkv_a[i].astype(bf16), _head_gain(g_q_a[i]), _head_gain(g_k_a[i]),
                cos, sin, nq=(w_qkv_a.shape[2] - 2 * kvw) // COL_TILE, has_kv=True, rope=True)
            o = attend(q, k_b16, v_b16, cache_k_a[i], cache_v_a[i], PREV_CHUNKS_A, None, sink_a[i])
            x = _out_proj(o, w_o_a[i].astype(bf16), x)
            ka_p.append(tail(k_f32, keep_a))
            va_p.append(tail(v_f32, keep_a))
            ka_s.append(rolled(cache_k_a[i], k_f32))
            va_s.append(rolled(cache_v_a[i], v_f32))
        else:
            jb = layer - n_a
            (q,) = _project(x, g_in, w_q_b[jb].astype(bf16), _head_gain(g_q_b[jb]), None,
                            None, None, nq=w_q_b.shape[2] // COL_TILE, has_kv=False, rope=False)
            o = attend(q, kb, vb, cache_k_b, cache_v_b, PREV_CHUNKS_B, rel_bias_b[jb], None)
            x = _out_proj(o, w_o_b[jb].astype(bf16), x)

        g_mid = row(g_ffn[layer])
        m = layer // 2
        if layer % 2 == 0:
            x = _swiglu_blocks(x, g_mid, w_gate_ffn[m:m + 1].astype(bf16),
                               w_up_ffn[m:m + 1].astype(bf16), w_down_ffn[m:m + 1].astype(bf16),
                               jnp.zeros((t // ROW_TILE,), jnp.int32), tm=ROW_TILE, residual=True)
        else:
            x = _moe(x, g_mid, w_router[m], b_router[m], w_gate_moe[m].astype(bf16),
                     w_up_moe[m].astype(bf16), w_down_moe[m].astype(bf16))

        if layer == n_a - 1:
            kb, vb, kb_f, vb_f = _project(x, row(g_kv_b), w_kv_b.astype(bf16), None,
                                          _head_gain(g_k_b), None, None,
                                          nq=0, has_kv=True, rope=False)

    y_prompt = x[:tp].reshape(nb, seq, d)
    y_sample = x[tp:].reshape(ns, seq_s, d)
    return (y_prompt, y_sample, jnp.stack(ka_p), jnp.stack(va_p), tail(kb_f, keep_b),
            tail(vb_f, keep_b), jnp.stack(ka_s), jnp.stack(va_s),
            rolled(cache_k_b, kb_f), rolled(cache_v_b, vb_f))
```

```python
import functools

import jax
import jax.numpy as jnp
from jax import lax
from jax.experimental import pallas as pl
from jax.experimental.pallas import tpu as pltpu

CHUNK = 64
HEAD_DIM = 64
GROUP = 4
WINDOW_A = 128
PREV_CHUNKS_A = WINDOW_A // CHUNK
PREV_CHUNKS_B = 8
BAND_B = PREV_CHUNKS_B * CHUNK
REL_MIN = -(CHUNK - 1)
REL_MAX = 256
ROPE_THETA = 10000.0
TOP_K = 2
RMS_EPS = 1e-6
NEG_INF = -1e30
PAST_LEN = 1024

LANES = 128
HEADS_PER_VREG = LANES // HEAD_DIM
KV_GROUP_WIDTH = GROUP * HEAD_DIM
PAIRS = GROUP // HEADS_PER_VREG

ROW_TILE = 512
COL_TILE = 512
MOE_ROWS = 512
COMBINE_ROWS = 256
ATTN_UNROLL = 4
VMEM_LIMIT = 56 * 1024 * 1024


def _params(semantics):
    return pltpu.CompilerParams(dimension_semantics=semantics, vmem_limit_bytes=VMEM_LIMIT)


def _rms_rows(x, g):
    ms = jnp.sum(x * x, axis=-1, keepdims=True) * (1.0 / x.shape[-1])
    return (x * lax.rsqrt(ms + RMS_EPS)) * g


def _head_norm(y, gain, cos, sin):
    lane = lax.broadcasted_iota(jnp.int32, y.shape, 1)
    low = lane < HEAD_DIM
    ss = y * y
    s_lo = jnp.sum(jnp.where(low, ss, 0.0), axis=-1, keepdims=True)
    s_hi = jnp.sum(jnp.where(low, 0.0, ss), axis=-1, keepdims=True)
    ms = jnp.where(low, s_lo, s_hi) * (1.0 / HEAD_DIM)
    yn = (y * lax.rsqrt(ms + RMS_EPS)) * gain
    if cos is None:
        return yn
    half = HEAD_DIM // 2
    fwd = pltpu.roll(yn, half, 1)
    bwd = pltpu.roll(yn, LANES - half, 1)
    rot = jnp.where((lane % HEAD_DIM) < half, bwd, fwd)
    return yn * cos + rot * sin


def _store_half_padded(lo_ref, hi_ref, c, z):
    lane = lax.broadcasted_iota(jnp.int32, z.shape, 1)
    low = lane < HEAD_DIM
    zr = pltpu.roll(z, HEAD_DIM, 1)
    dt = lo_ref.dtype
    a, b = 2 * c * LANES, (2 * c + 1) * LANES
    lo_ref[:, a:a + LANES] = jnp.where(low, z, 0.0).astype(dt)
    lo_ref[:, b:b + LANES] = jnp.where(low, zr, 0.0).astype(dt)
    hi_ref[:, a:a + LANES] = jnp.where(low, 0.0, zr).astype(dt)
    hi_ref[:, b:b + LANES] = jnp.where(low, 0.0, z).astype(dt)


def _proj_kernel(*refs, nq, has_kv, rope):
    it = iter(refs)
    x_ref, g_ref, w_ref = next(it), next(it), next(it)
    gq_ref = next(it) if nq else None
    gk_ref = next(it) if has_kv else None
    cos_ref, sin_ref = (next(it), next(it)) if rope else (None, None)
    q_ref = next(it) if nq else None
    if has_kv:
        kl_ref, kh_ref, vl_ref, vh_ref, kf_ref, vf_ref = (next(it) for _ in range(6))
    h_scr = next(it)
    j = pl.program_id(1)

    @pl.when(j == 0)
    def _():
        h_scr[...] = _rms_rows(x_ref[...], g_ref[...]).astype(h_scr.dtype)

    y = jnp.dot(h_scr[...], w_ref[...], preferred_element_type=jnp.float32)
    cos = cos_ref[...] if rope else None
    sin = sin_ref[...] if rope else None

    def normed(gain_ref, c):
        return _head_norm(y[:, c * LANES:(c + 1) * LANES], gain_ref[...], cos, sin)

    if nq:
        @pl.when(j < nq)
        def _():
            for c in range(COL_TILE // LANES):
                z = normed(gq_ref, c) * (HEAD_DIM ** -0.5)
                q_ref[:, c * LANES:(c + 1) * LANES] = z.astype(q_ref.dtype)

    if has_kv:
        @pl.when(j == nq)
        def _():
            for c in range(COL_TILE // LANES):
                z = normed(gk_ref, c)
                kf_ref[:, c * LANES:(c + 1) * LANES] = z
                _store_half_padded(kl_ref, kh_ref, c, z)

        @pl.when(j == nq + 1)
        def _():
            vf_ref[...] = y
            for c in range(COL_TILE // LANES):
                _store_half_padded(vl_ref, vh_ref, c, y[:, c * LANES:(c + 1) * LANES])


def _project(x, g, w, gq, gk, cos, sin, *, nq, has_kv, rope):
    t, d = x.shape
    tm = ROW_TILE
    n_col = nq + (2 if has_kv else 0)
    assert w.shape == (d, n_col * COL_TILE) and t % tm == 0
    row = lambda i, j: (i, 0)
    in_specs = [pl.BlockSpec((tm, d), row),
                pl.BlockSpec((1, d), lambda i, j: (0, 0)),
                pl.BlockSpec((d, COL_TILE), lambda i, j: (0, j))]
    args = [x, g, w]
    vec = pl.BlockSpec((1, LANES), lambda i, j: (0, 0))
    if nq:
        in_specs.append(vec)
        args.append(gq)
    if has_kv:
        in_specs.append(vec)
        args.append(gk)
    if rope:
        in_specs += [pl.BlockSpec((tm, LANES), row)] * 2
        args += [cos, sin]
    out_shape, out_specs = [], []
    if nq:
        out_shape.append(jax.ShapeDtypeStruct((t, nq * COL_TILE), jnp.bfloat16))
        out_specs.append(pl.BlockSpec((tm, COL_TILE), lambda i, j: (i, jnp.minimum(j, nq - 1))))
    if has_kv:
        for _ in range(4):
            out_shape.append(jax.ShapeDtypeStruct((t, HEADS_PER_VREG * COL_TILE), jnp.bfloat16))
            out_specs.append(pl.BlockSpec((tm, HEADS_PER_VREG * COL_TILE), row))
        for _ in range(2):
            out_shape.append(jax.ShapeDtypeStruct((t, COL_TILE), jnp.float32))
            out_specs.append(pl.BlockSpec((tm, COL_TILE), row))
    return pl.pallas_call(
        functools.partial(_proj_kernel, nq=nq, has_kv=has_kv, rope=rope),
        grid=(t // tm, n_col),
        in_specs=in_specs,
        out_specs=out_specs,
        out_shape=out_shape,
        scratch_shapes=[pltpu.VMEM((tm, d), jnp.bfloat16)],
        compiler_params=_params(("parallel", "arbitrary")),
        name="proj",
    )(*args)


def _attn_prompt_kernel(*refs, seq, band, n_prev, unroll, has_bias, has_sink):
    it = iter(refs)
    q_ref, kl_ref, kh_ref, vl_ref, vh_ref = (next(it) for _ in range(5))
    bias_ref = next(it) if has_bias else None
    sink_ref = next(it) if has_sink else None
    o_ref, s_scr, p_scr = next(it), next(it), next(it)
    row = lax.broadcasted_iota(jnp.int32, (band, LANES), 0)
    nt = (((1,), (1,)), ((), ()))
    tn = (((0,), (0,)), ((), ()))

    def key_start(c):
        return pl.multiple_of(jnp.maximum(c - n_prev, 0) * CHUNK, CHUNK)

    def group(gi, carry):
        for u in range(unroll):
            c = gi * unroll + u
            r0 = pl.multiple_of(c * CHUNK, CHUNK)
            k0 = key_start(c)
            qs = jnp.concatenate([q_ref[pl.ds(r0, CHUNK), p * LANES:(p + 1) * LANES]
                                  for p in range(PAIRS)], axis=0)
            for half, k_ref in enumerate((kl_ref, kh_ref)):
                s_scr[u, half] = lax.dot_general(k_ref[pl.ds(k0, band), :], qs, nt,
                                                 preferred_element_type=jnp.float32)
        for u in range(unroll):
            c = gi * unroll + u
            shift = pl.multiple_of(jnp.maximum(n_prev - c, 0) * CHUNK, CHUNK)
            for half in range(HEADS_PER_VREG):
                s = s_scr[u, half]
                if has_bias:
                    s = s + bias_ref[half, pl.ds(shift, band), :]
                s = jnp.where(row < band - shift, s, NEG_INF)
                m = jnp.max(s, axis=0, keepdims=True)
                if has_sink:
                    m = jnp.maximum(m, sink_ref[half])
                e = jnp.exp(s - m)
                l = jnp.sum(e, axis=0, keepdims=True)
                if has_sink:
                    l = l + jnp.exp(sink_ref[half] - m)
                p_scr[u, half] = (e * (1.0 / l)).astype(p_scr.dtype)
        for u in range(unroll):
            c = gi * unroll + u
            r0 = pl.multiple_of(c * CHUNK, CHUNK)
            k0 = key_start(c)
            o = (lax.dot_general(p_scr[u, 0], vl_ref[pl.ds(k0, band), :], tn,
                                 preferred_element_type=jnp.float32)
                 + lax.dot_general(p_scr[u, 1], vh_ref[pl.ds(k0, band), :], tn,
                                   preferred_element_type=jnp.float32))
            for p in range(PAIRS):
                o_ref[pl.ds(r0, CHUNK), p * LANES:(p + 1) * LANES] = (
                    o[p * CHUNK:(p + 1) * CHUNK].astype(o_ref.dtype))
        return carry

    lax.fori_loop(0, seq // (CHUNK * unroll), group, 0)


def _attend_prompt(q, kl, kh, vl, vh, bias, sink, *, nb, seq, n_prev):
    t, qw = q.shape
    n_kv = kl.shape[1] // LANES
    band = (n_prev + 1) * CHUNK
    unroll = ATTN_UNROLL
    assert seq % (CHUNK * unroll) == 0 and seq >= band
    kv_spec = pl.BlockSpec((seq, LANES), lambda b, j: (b, j))
    in_specs = [pl.BlockSpec((seq, KV_GROUP_WIDTH), lambda b, j: (b, j))] + [kv_spec] * 4
    args = [q, kl, kh, vl, vh]
    if bias is not None:
        in_specs.append(pl.BlockSpec((HEADS_PER_VREG,) + bias.shape[1:], lambda b, j: (j, 0, 0)))
        args.append(bias)
    if sink is not None:
        in_specs.append(pl.BlockSpec((HEADS_PER_VREG, 1, LANES), lambda b, j: (j, 0, 0)))
        args.append(sink)
    return pl.pallas_call(
        functools.partial(_attn_prompt_kernel, seq=seq, band=band, n_prev=n_prev, unroll=unroll,
                          has_bias=bias is not None, has_sink=sink is not None),
        grid=(nb, n_kv),
        in_specs=in_specs,
        out_specs=pl.BlockSpec((seq, KV_GROUP_WIDTH), lambda b, j: (b, j)),
        out_shape=jax.ShapeDtypeStruct((t, qw), jnp.bfloat16),
        scratch_shapes=[pltpu.VMEM((unroll, HEADS_PER_VREG, band, LANES), jnp.float32),
                        pltpu.VMEM((unroll, HEADS_PER_VREG, band, LANES), jnp.bfloat16)],
        compiler_params=_params(("parallel", "arbitrary")),
        name="attn_prompt",
    )(*args)


def _attn_sample_kernel(*refs, lq, has_bias, has_sink, n_kv):
    it = iter(refs)
    q_ref, k_ref, v_ref = next(it), next(it), next(it)
    bias_ref = next(it) if has_bias else None
    sink_ref = next(it) if has_sink else None
    _ = next(it)
    o_ref = next(it)
    for j in range(n_kv):
        qj = q_ref[:, j * KV_GROUP_WIDTH:(j + 1) * KV_GROUP_WIDTH]
        qcat = jnp.concatenate(
            [qj[:, g * HEAD_DIM:(g + 1) * HEAD_DIM] for g in range(GROUP)], axis=0)
        kj = k_ref[0, :, j * HEAD_DIM:(j + 1) * HEAD_DIM]
        vj = v_ref[0, :, j * HEAD_DIM:(j + 1) * HEAD_DIM]
        s = lax.dot_general(qcat, kj, (((1,), (1,)), ((), ())),
                            preferred_element_type=jnp.float32)
        if has_bias:
            s = s + bias_ref[j]
        m = jnp.max(s, axis=-1, keepdims=True)
        if has_sink:
            sink = sink_ref[j]
            m = jnp.maximum(m, sink)
        e = jnp.exp(s - m)
        l = jnp.sum(e, axis=-1, keepdims=True)
        if has_sink:
            l = l + jnp.exp(sink - m)
        p = (e * (1.0 / l)).astype(vj.dtype)
        o = jnp.dot(p, vj, preferred_element_type=jnp.float32)
        ocat = jnp.concatenate([o[g * lq:(g + 1) * lq] for g in range(GROUP)], axis=1)
        o_ref[:, j * KV_GROUP_WIDTH:(j + 1) * KV_GROUP_WIDTH] = ocat.astype(o_ref.dtype)


def _attend_sample(q, k, v, bias, sink, o, *, row0, lq):
    ns, lk, kw = k.shape
    qw = q.shape[1]
    assert row0 % lq == 0
    blk0 = row0 // lq
    qo_spec = pl.BlockSpec((lq, qw), lambda i: (blk0 + i, 0))
    in_specs = [qo_spec, pl.BlockSpec((1, lk, kw), lambda i: (i, 0, 0)),
                pl.BlockSpec((1, lk, kw), lambda i: (i, 0, 0))]
    args = [q, k, v]
    if bias is not None:
        in_specs.append(pl.BlockSpec(bias.shape, lambda i: (0, 0, 0)))
        args.append(bias)
    if sink is not None:
        in_specs.append(pl.BlockSpec(sink.shape, lambda i: (0, 0, 0)))
        args.append(sink)
    in_specs.append(pl.BlockSpec(memory_space=pl.ANY))
    args.append(o)
    return pl.pallas_call(
        functools.partial(_attn_sample_kernel, lq=lq, has_bias=bias is not None,
                          has_sink=sink is not None, n_kv=kw // HEAD_DIM),
        grid=(ns,),
        in_specs=in_specs,
        out_specs=qo_spec,
        out_shape=jax.ShapeDtypeStruct(o.shape, o.dtype),
        input_output_aliases={len(args) - 1: 0},
        compiler_params=_params(("arbitrary",)),
        name="attn_sample",
    )(*args)


def _out_proj_kernel(o_ref, w_ref, x_ref, y_ref):
    y_ref[...] = x_ref[...] + jnp.dot(o_ref[...], w_ref[...],
                                      preferred_element_type=jnp.float32)


def _out_proj(o, w, x):
    t, d = x.shape
    tm = ROW_TILE
    return pl.pallas_call(
        _out_proj_kernel,
        grid=(t // tm, d // COL_TILE),
        in_specs=[pl.BlockSpec((tm, o.shape[1]), lambda i, j: (i, 0)),
                  pl.BlockSpec((o.shape[1], COL_TILE), lambda i, j: (0, j)),
                  pl.BlockSpec((tm, COL_TILE), lambda i, j: (i, j))],
        out_specs=pl.BlockSpec((tm, COL_TILE), lambda i, j: (i, j)),
        out_shape=jax.ShapeDtypeStruct((t, d), jnp.float32),
        compiler_params=_params(("parallel", "arbitrary")),
        name="out_proj",
    )(o, w, x)


def _ffn_kernel(e_ref, n_ref, x_ref, g_ref, wg_ref, wu_ref, wd_ref, y_ref, h_scr, acc_scr, *,
                residual):
    f = pl.program_id(1)
    last = pl.num_programs(1) - 1

    @pl.when(pl.program_id(0) < n_ref[0])
    def _():
        @pl.when(f == 0)
        def _():
            h_scr[...] = _rms_rows(x_ref[...], g_ref[...]).astype(h_scr.dtype)

        h = h_scr[...]
        gate = jnp.dot(h, wg_ref[...], preferred_element_type=jnp.float32)
        up = jnp.dot(h, wu_ref[...], preferred_element_type=jnp.float32)
        act = (gate * jax.nn.sigmoid(gate)) * up
        part = jnp.dot(act.astype(jnp.bfloat16), wd_ref[...],
                       preferred_element_type=jnp.float32)

        @pl.when(f == 0)
        def _():
            acc_scr[...] = part

        @pl.when(f > 0)
        def _():
            acc_scr[...] += part

        @pl.when(f == last)
        def _():
            y_ref[...] = (x_ref[...] + acc_scr[...]) if residual else acc_scr[...]


def _swiglu_blocks(x, g, wg, wu, wd, block_expert, n_used, *, tm, residual):
    r, d = x.shape
    ff = wg.shape[2]
    assert r % tm == 0 and ff % COL_TILE == 0
    n_f = ff // COL_TILE

    def blk(i, n):
        return jnp.minimum(i, n[0] - 1)

    def col(i, f, n):
        return jnp.where(i < n[0], f, n_f - 1)

    grid_spec = pltpu.PrefetchScalarGridSpec(
        num_scalar_prefetch=2,
        grid=(r // tm, n_f),
        in_specs=[pl.BlockSpec((tm, d), lambda i, f, e, n: (blk(i, n), 0)),
                  pl.BlockSpec((1, d), lambda i, f, e, n: (0, 0)),
                  pl.BlockSpec((None, d, COL_TILE),
                               lambda i, f, e, n: (e[blk(i, n)], 0, col(i, f, n))),
                  pl.BlockSpec((None, d, COL_TILE),
                               lambda i, f, e, n: (e[blk(i, n)], 0, col(i, f, n))),
                  pl.BlockSpec((None, COL_TILE, d),
                               lambda i, f, e, n: (e[blk(i, n)], col(i, f, n), 0))],
        out_specs=pl.BlockSpec((tm, d), lambda i, f, e, n: (blk(i, n), 0)),
        scratch_shapes=[pltpu.VMEM((tm, d), jnp.bfloat16),
                        pltpu.VMEM((tm, d), jnp.float32)],
    )
    return pl.pallas_call(
        functools.partial(_ffn_kernel, residual=residual),
        grid_spec=grid_spec,
        out_shape=jax.ShapeDtypeStruct((r, d), jnp.float32),
        compiler_params=_params(("arbitrary", "arbitrary")),
        name="swiglu",
    )(block_expert, n_used, x, g, wg, wu, wd)


def _router_kernel(x_ref, g_ref, w_ref, b_ref, idx_ref, gate_ref):
    h = _rms_rows(x_ref[...], g_ref[...]).astype(jnp.bfloat16)
    logits = jnp.dot(h, w_ref[...], preferred_element_type=jnp.float32) + b_ref[...]
    lane = lax.broadcasted_iota(jnp.int32, logits.shape, 1)
    m1 = jnp.max(logits, axis=-1, keepdims=True)
    e1 = jnp.min(jnp.where(logits == m1, lane, LANES), axis=-1, keepdims=True)
    rest = jnp.where(lane == e1, -jnp.inf, logits)
    m2 = jnp.max(rest, axis=-1, keepdims=True)
    e2 = jnp.min(jnp.where(rest == m2, lane, LANES), axis=-1, keepdims=True)
    t = jnp.exp(m2 - m1)
    denom = 1.0 + t
    idx_ref[...] = jnp.where(lane == 0, e1, jnp.where(lane == 1, e2, 0))
    gate_ref[...] = jnp.where(lane == 0, 1.0 / denom, jnp.where(lane == 1, t / denom, 0.0))


def _route(x, g, w_router, b_router):
    t, d = x.shape
    tm = ROW_TILE
    n_e = w_router.shape[1]
    w = jnp.zeros((d, LANES), jnp.bfloat16).at[:, :n_e].set(w_router.astype(jnp.bfloat16))
    b = jnp.full((1, LANES), -jnp.inf, jnp.float32).at[0, :n_e].set(b_router.astype(jnp.float32))
    return pl.pallas_call(
        _router_kernel,
        grid=(t // tm,),
        in_specs=[pl.BlockSpec((tm, d), lambda i: (i, 0)),
                  pl.BlockSpec((1, d), lambda i: (0, 0)),
                  pl.BlockSpec((d, LANES), lambda i: (0, 0)),
                  pl.BlockSpec((1, LANES), lambda i: (0, 0))],
        out_specs=[pl.BlockSpec((tm, LANES), lambda i: (i, 0))] * 2,
        out_shape=[jax.ShapeDtypeStruct((t, LANES), jnp.int32),
                   jax.ShapeDtypeStruct((t, LANES), jnp.float32)],
        compiler_params=_params(("parallel",)),
        name="router",
    )(x, g, w, b)


def _gather_kernel(idx_ref, src_ref, dst_ref, sem, *, rows):
    def row_copy(r):
        return pltpu.make_async_copy(src_ref.at[pl.ds(idx_ref[r], 1)],
                                     dst_ref.at[pl.ds(r, 1)], sem)

    def issue(r, carry):
        row_copy(r).start()
        return carry

    lax.fori_loop(0, rows, issue, 0)

    def drain(r, carry):
        row_copy(r).wait()
        return carry

    lax.fori_loop(0, rows, drain, 0)


def _gather_rows(src, idx, *, rows):
    n, d = src.shape
    r = idx.shape[0]
    assert r % rows == 0
    return pl.pallas_call(
        functools.partial(_gather_kernel, rows=rows),
        grid=(r // rows,),
        in_specs=[pl.BlockSpec((rows,), lambda i: (i,), memory_space=pltpu.SMEM),
                  pl.BlockSpec(memory_space=pl.ANY)],
        out_specs=pl.BlockSpec((rows, d), lambda i: (i, 0)),
        out_shape=jax.ShapeDtypeStruct((r, d), src.dtype),
        scratch_shapes=[pltpu.SemaphoreType.DMA(())],
        compiler_params=_params(("arbitrary",)),
        name="gather_rows",
    )(idx, src)


def _combine_kernel(d0_ref, d1_ref, x_ref, gate_ref, y_hbm, out_ref, buf0, buf1, sem, *, rows):
    def copies(r):
        return (pltpu.make_async_copy(y_hbm.at[pl.ds(d0_ref[r], 1)], buf0.at[pl.ds(r, 1)],
                                      sem.at[0]),
                pltpu.make_async_copy(y_hbm.at[pl.ds(d1_ref[r], 1)], buf1.at[pl.ds(r, 1)],
                                      sem.at[1]))

    def issue(r, carry):
        c0, c1 = copies(r)
        c0.start()
        c1.start()
        return carry

    lax.fori_loop(0, rows, issue, 0)

    def drain(r, carry):
        c0, c1 = copies(r)
        c0.wait()
        c1.wait()
        return carry

    lax.fori_loop(0, rows, drain, 0)
    gates = gate_ref[...]
    y = gates[:, 0:1] * buf0[...] + gates[:, 1:2] * buf1[...]
    out_ref[...] = x_ref[...] + y


def _combine(x, gates, y_rows, d0, d1):
    t, d = x.shape
    rows = COMBINE_ROWS
    assert t % rows == 0
    idx_spec = pl.BlockSpec((rows,), lambda i: (i,), memory_space=pltpu.SMEM)
    return pl.pallas_call(
        functools.partial(_combine_kernel, rows=rows),
        grid=(t // rows,),
        in_specs=[idx_spec, idx_spec,
                  pl.BlockSpec((rows, d), lambda i: (i, 0)),
                  pl.BlockSpec((rows, LANES), lambda i: (i, 0)),
                  pl.BlockSpec(memory_space=pl.ANY)],
        out_specs=pl.BlockSpec((rows, d), lambda i: (i, 0)),
        out_shape=jax.ShapeDtypeStruct((t, d), jnp.float32),
        scratch_shapes=[pltpu.VMEM((rows, d), jnp.float32),
                        pltpu.VMEM((rows, d), jnp.float32),
                        pltpu.SemaphoreType.DMA((2,))],
        compiler_params=_params(("arbitrary",)),
        name="combine",
    )(d0, d1, x, gates, y_rows)


def _moe(x, g, w_router, b_router, wg, wu, wd):
    t, d = x.shape
    n_e = wg.shape[0]
    n_assign = t * TOP_K
    top_idx, gates = _route(x, g, w_router, b_router)
    flat_e = top_idx[:, :TOP_K].reshape(-1)
    onehot = (flat_e[:, None] == jnp.arange(n_e, dtype=jnp.int32)[None, :]).astype(jnp.int32)
    rank = jnp.sum((jnp.cumsum(onehot, axis=0) - onehot) * onehot, axis=1)
    counts = jnp.sum(onehot, axis=0)
    padded = (counts + MOE_ROWS - 1) // MOE_ROWS * MOE_ROWS
    ends = jnp.cumsum(padded)
    dest = jnp.sum((ends - padded)[None, :] * onehot, axis=1) + rank
    n_blocks = (n_assign + n_e * (MOE_ROWS - 1) + MOE_ROWS - 1) // MOE_ROWS
    rows = n_blocks * MOE_ROWS
    flat_tok = jnp.arange(n_assign, dtype=jnp.int32) // TOP_K
    row_tok = jnp.zeros((rows,), jnp.int32).at[dest].set(flat_tok)
    block_start = jnp.arange(n_blocks, dtype=jnp.int32) * MOE_ROWS
    block_expert = jnp.minimum(
        jnp.sum((ends[None, :] <= block_start[:, None]).astype(jnp.int32), axis=1), n_e - 1)
    n_used = (ends[-1:] // MOE_ROWS).astype(jnp.int32)
    x_rows = _gather_rows(x, row_tok, rows=MOE_ROWS)
    y_rows = _swiglu_blocks(x_rows, g, wg, wu, wd, block_expert, n_used, tm=MOE_ROWS,
                            residual=False)
    dest2 = dest.reshape(t, TOP_K).astype(jnp.int32)
    return _combine(x, gates, y_rows, dest2[:, 0], dest2[:, 1])


def _rope_tables(pos):
    half = HEAD_DIM // 2
    inv_freq = 1.0 / (ROPE_THETA ** (jnp.arange(0, HEAD_DIM, 2, dtype=jnp.float32) / HEAD_DIM))
    ang = pos.astype(jnp.float32)[:, None] * inv_freq[None, :]
    cos, sin = jnp.cos(ang), jnp.sin(ang)
    reps = LANES // half
    signs = jnp.tile(jnp.concatenate([-jnp.ones((half,)), jnp.ones((half,))]), HEADS_PER_VREG)
    return jnp.tile(cos, (1, reps)), jnp.tile(sin, (1, reps)) * signs[None, :].astype(jnp.float32)


def _rel_bias(table, lq, n_before):
    rel = jnp.arange(lq)[:, None] - (jnp.arange(n_before + lq) - n_before)[None, :]
    idx = jnp.clip(rel, REL_MIN, REL_MAX) - REL_MIN
    return table.astype(jnp.float32)[:, idx]


def _pair_major(a):
    h = a.shape[0]
    a = a.reshape((h // GROUP, PAIRS, HEADS_PER_VREG) + a.shape[1:])
    return jnp.swapaxes(a, 1, 2)


def _prompt_bias(table, n_prev):
    pad = n_prev * CHUNK
    b = _pair_major(_rel_bias(table, CHUNK, pad))
    kvh = b.shape[0]
    b = jnp.transpose(b, (0, 1, 4, 2, 3)).reshape(kvh * HEADS_PER_VREG, pad + CHUNK, LANES)
    return jnp.pad(b, ((0, 0), (0, pad), (0, 0)))


def _prompt_sink(sink):
    s = _pair_major(sink.astype(jnp.float32))
    s = jnp.repeat(s[..., None], CHUNK, axis=-1)
    return s.reshape(-1, 1, LANES)


def _sample_bias(table, lq, n_cached):
    b = _rel_bias(table, lq, n_cached)
    return b.reshape(b.shape[0] // GROUP, GROUP * lq, n_cached + lq)


def _sample_sink(sink, lq):
    h = sink.shape[0]
    s = jnp.repeat(sink.astype(jnp.float32).reshape(h // GROUP, GROUP, 1), lq, axis=2)
    return s.reshape(h // GROUP, GROUP * lq, 1)


def _head_gain(g):
    return jnp.tile(g.astype(jnp.float32), HEADS_PER_VREG)[None, :]


def kernel(x_prompt, x_sample, cache_k_a, cache_v_a, cache_k_b, cache_v_b, g_attn, g_ffn, w_qkv_a, g_q_a, g_k_a, sink_a, w_o_a, g_kv_b, w_kv_b, g_k_b, w_q_b, g_q_b, rel_bias_b, w_o_b, w_gate_ffn, w_up_ffn, w_down_ffn, w_router, b_router, w_gate_moe, w_up_moe, w_down_moe):
    bf16 = jnp.bfloat16
    nb, seq, d = x_prompt.shape
    ns, seq_s, _ = x_sample.shape
    tp, ts = nb * seq, ns * seq_s
    t = tp + ts
    depth = g_attn.shape[0]
    n_a = w_qkv_a.shape[0]
    n_kv = cache_k_b.shape[2]
    kvw = n_kv * HEAD_DIM
    keep_a, keep_b = min(WINDOW_A, seq), min(BAND_B, seq)
    assert t % ROW_TILE == 0

    x = jnp.concatenate([x_prompt.reshape(tp, d), x_sample.reshape(ts, d)], axis=0)
    pos = jnp.concatenate([jnp.tile(jnp.arange(seq), nb),
                           jnp.tile(PAST_LEN + jnp.arange(seq_s), ns)])
    cos, sin = _rope_tables(pos)
    row = lambda v: v.astype(jnp.float32)[None, :]

    def attend(q, kv_pad, k_f32, v_f32, cache_k, cache_v, n_prev, table, sink):
        bias_p = None if table is None else _prompt_bias(table, n_prev)
        sink_p = None if sink is None else _prompt_sink(sink)
        o = _attend_prompt(q, *kv_pad, bias_p, sink_p, nb=nb, seq=seq, n_prev=n_prev)
        n_cached = cache_k.shape[1]
        ks = jnp.concatenate([cache_k.reshape(ns, n_cached, kvw),
                              k_f32[tp:].reshape(ns, seq_s, kvw)], axis=1).astype(bf16)
        vs = jnp.concatenate([cache_v.reshape(ns, n_cached, kvw),
                              v_f32[tp:].reshape(ns, seq_s, kvw)], axis=1).astype(bf16)
        bias_s = None if table is None else _sample_bias(table, seq_s, n_cached)
        sink_s = None if sink is None else _sample_sink(sink, seq_s)
        return _attend_sample(q, ks, vs, bias_s, sink_s, o, row0=tp, lq=seq_s)

    def tail(c, n_keep):
        return c[:tp].reshape(nb, seq, n_kv, HEAD_DIM)[:, seq - n_keep:]

    def rolled(cache, c):
        new = c[tp:].reshape(ns, seq_s, n_kv, HEAD_DIM)
        return jnp.concatenate([cache, new], axis=1)[:, seq_s:]

    ka_p, va_p, ka_s, va_s = [], [], [], []
    kvb_pad = kb_f = vb_f = None
    for layer in range(depth):
        g_in = row(g_attn[layer])
        if layer < n_a:
            i = layer
            q, kl, kh, vl, vh, k_f32, v_f32 = _project(
                x, g_in, w_qkv_a[i].astype(bf16), _head_gain(g_q_a[i]), _head_gain(g_k_a[i]),
                cos, sin, nq=(w_qkv_a.shape[2] - 2 * kvw) // COL_TILE, has_kv=True, rope=True)
            o = attend(q, (kl, kh, vl, vh), k_f32, v_f32, cache_k_a[i], cache_v_a[i],
                       PREV_CHUNKS_A, None, sink_a[i])
            x = _out_proj(o, w_o_a[i].astype(bf16), x)
            ka_p.append(tail(k_f32, keep_a))
            va_p.append(tail(v_f32, keep_a))
            ka_s.append(rolled(cache_k_a[i], k_f32))
            va_s.append(rolled(cache_v_a[i], v_f32))
        else:
            jb = layer - n_a
            (q,) = _project(x, g_in, w_q_b[jb].astype(bf16), _head_gain(g_q_b[jb]), None,
                            None, None, nq=w_q_b.shape[2] // COL_TILE, has_kv=False, rope=False)
            o = attend(q, kvb_pad, kb_f, vb_f, cache_k_b, cache_v_b, PREV_CHUNKS_B,
                       rel_bias_b[jb], None)
            x = _out_proj(o, w_o_b[jb].astype(bf16), x)

        g_mid = row(g_ffn[layer])
        m = layer // 2
        if layer % 2 == 0:
            n_tiles = t // ROW_TILE
            x = _swiglu_blocks(x, g_mid, w_gate_ffn[m:m + 1].astype(bf16),
                               w_up_ffn[m:m + 1].astype(bf16), w_down_ffn[m:m + 1].astype(bf16),
                               jnp.zeros((n_tiles,), jnp.int32),
                               jnp.full((1,), n_tiles, jnp.int32), tm=ROW_TILE, residual=True)
        else:
            x = _moe(x, g_mid, w_router[m], b_router[m], w_gate_moe[m].astype(bf16),
                     w_up_moe[m].astype(bf16), w_down_moe[m].astype(bf16))

        if layer == n_a - 1:
            *kvb_pad, kb_f, vb_f = _project(x, row(g_kv_b), w_kv_b.astype(bf16), None,
                                            _head_gain(g_k_b), None, None,
                                            nq=0, has_kv=True, rope=False)

    y_prompt = x[:tp].reshape(nb, seq, d)
    y_sample = x[tp:].reshape(ns, seq_s, d)
    return (y_prompt, y_sample, jnp.stack(ka_p), jnp.stack(va_p), tail(kb_f, keep_b),
            tail(vb_f, keep_b), jnp.stack(ka_s), jnp.stack(va_s),
            rolled(cache_k_b, kb_f), rolled(cache_v_b, vb_f))
```

```python
import functools

import jax
import jax.numpy as jnp
from jax import lax
from jax.experimental import pallas as pl
from jax.experimental.pallas import tpu as pltpu

CHUNK = 64
HEAD_DIM = 64
GROUP = 4
WINDOW_A = 128
PREV_CHUNKS_A = WINDOW_A // CHUNK
PREV_CHUNKS_B = 8
BAND_B = PREV_CHUNKS_B * CHUNK
REL_MIN = -(CHUNK - 1)
REL_MAX = 256
ROPE_THETA = 10000.0
TOP_K = 2
RMS_EPS = 1e-6
NEG_INF = -1e30
PAST_LEN = 1024

LANES = 128
HEADS_PER_VREG = LANES // HEAD_DIM
KV_GROUP_WIDTH = GROUP * HEAD_DIM
PAIRS = GROUP // HEADS_PER_VREG

ROW_TILE = 512
COL_TILE = 512
MOE_ROWS = 512
MOE_COL_TILE = 1024
COMBINE_ROWS = 256
ATTN_UNROLL = 4
VMEM_LIMIT = 56 * 1024 * 1024


def _params(semantics):
    return pltpu.CompilerParams(dimension_semantics=semantics, vmem_limit_bytes=VMEM_LIMIT)


def _rms_rows(x, g):
    ms = jnp.sum(x * x, axis=-1, keepdims=True) * (1.0 / x.shape[-1])
    return (x * lax.rsqrt(ms + RMS_EPS)) * g


def _head_norm(y, gain, cos, sin):
    lane = lax.broadcasted_iota(jnp.int32, y.shape, 1)
    low = lane < HEAD_DIM
    ss = y * y
    s_lo = jnp.sum(jnp.where(low, ss, 0.0), axis=-1, keepdims=True)
    s_hi = jnp.sum(jnp.where(low, 0.0, ss), axis=-1, keepdims=True)
    ms = jnp.where(low, s_lo, s_hi) * (1.0 / HEAD_DIM)
    yn = (y * lax.rsqrt(ms + RMS_EPS)) * gain
    if cos is None:
        return yn
    half = HEAD_DIM // 2
    fwd = pltpu.roll(yn, half, 1)
    bwd = pltpu.roll(yn, LANES - half, 1)
    rot = jnp.where((lane % HEAD_DIM) < half, bwd, fwd)
    return yn * cos + rot * sin


def _store_half_padded(lo_ref, hi_ref, c, z):
    lane = lax.broadcasted_iota(jnp.int32, z.shape, 1)
    low = lane < HEAD_DIM
    zr = pltpu.roll(z, HEAD_DIM, 1)
    dt = lo_ref.dtype
    a, b = 2 * c * LANES, (2 * c + 1) * LANES
    lo_ref[:, a:a + LANES] = jnp.where(low, z, 0.0).astype(dt)
    lo_ref[:, b:b + LANES] = jnp.where(low, zr, 0.0).astype(dt)
    hi_ref[:, a:a + LANES] = jnp.where(low, 0.0, zr).astype(dt)
    hi_ref[:, b:b + LANES] = jnp.where(low, 0.0, z).astype(dt)


def _row_specs(x, tm, n_extra_grid_axes):
    def spec(cols, fn):
        if n_extra_grid_axes:
            return pl.BlockSpec((tm, cols), lambda i, j: (fn(i), 0))
        return pl.BlockSpec((tm, cols), lambda i: (fn(i), 0))

    if not isinstance(x, tuple):
        return [spec(x.shape[1], lambda i: i)], 0
    head, tail = x
    assert head.shape[0] % tm == 0 and tail.shape[0] % tm == 0 and head.shape[1] == tail.shape[1]
    n_head = head.shape[0] // tm
    return [spec(head.shape[1], lambda i: jnp.minimum(i, n_head - 1)),
            spec(head.shape[1], lambda i: jnp.maximum(i - n_head, 0))], n_head


def _read_rows(x_refs, n_head):
    if len(x_refs) == 1:
        return x_refs[0][...]
    return jnp.where(pl.program_id(0) < n_head, x_refs[0][...], x_refs[1][...])


def _n_rows(x):
    return x[0].shape[0] + x[1].shape[0] if isinstance(x, tuple) else x.shape[0]


def _as_list(x):
    return list(x) if isinstance(x, tuple) else [x]


def _proj_kernel(*refs, nq, has_kv, rope, n_x, n_head):
    it = iter(refs)
    x_refs = [next(it) for _ in range(n_x)]
    g_ref, w_ref = next(it), next(it)
    gq_ref = next(it) if nq else None
    gk_ref = next(it) if has_kv else None
    cos_ref, sin_ref = (next(it), next(it)) if rope else (None, None)
    q_ref = next(it) if nq else None
    if has_kv:
        kl_ref, kh_ref, vl_ref, vh_ref, kf_ref, vf_ref = (next(it) for _ in range(6))
    h_scr = next(it)
    j = pl.program_id(1)

    @pl.when(j == 0)
    def _():
        h_scr[...] = _rms_rows(_read_rows(x_refs, n_head), g_ref[...]).astype(h_scr.dtype)

    y = jnp.dot(h_scr[...], w_ref[j], preferred_element_type=jnp.float32)
    cos = cos_ref[...] if rope else None
    sin = sin_ref[...] if rope else None

    def normed(gain_ref, c):
        return _head_norm(y[:, c * LANES:(c + 1) * LANES], gain_ref[...], cos, sin)

    if nq:
        @pl.when(j < nq)
        def _():
            for c in range(COL_TILE // LANES):
                z = normed(gq_ref, c) * (HEAD_DIM ** -0.5)
                q_ref[:, c * LANES:(c + 1) * LANES] = z.astype(q_ref.dtype)

    if has_kv:
        @pl.when(j == nq)
        def _():
            for c in range(COL_TILE // LANES):
                z = normed(gk_ref, c)
                kf_ref[:, c * LANES:(c + 1) * LANES] = z
                _store_half_padded(kl_ref, kh_ref, c, z)

        @pl.when(j == nq + 1)
        def _():
            vf_ref[...] = y
            for c in range(COL_TILE // LANES):
                _store_half_padded(vl_ref, vh_ref, c, y[:, c * LANES:(c + 1) * LANES])


def _project(x, g, w, gq, gk, cos, sin, *, nq, has_kv, rope):
    t = _n_rows(x)
    d = w.shape[0]
    tm = ROW_TILE
    n_col = nq + (2 if has_kv else 0)
    assert w.shape == (d, n_col * COL_TILE) and t % tm == 0
    w = jnp.swapaxes(w.reshape(d, n_col, COL_TILE), 0, 1)
    row = lambda i, j: (i, 0)
    x_specs, n_head = _row_specs(x, tm, 1)
    in_specs = x_specs + [pl.BlockSpec((1, d), lambda i, j: (0, 0)),
                          pl.BlockSpec((n_col, d, COL_TILE), lambda i, j: (0, 0, 0))]
    args = _as_list(x) + [g, w]
    vec = pl.BlockSpec((1, LANES), lambda i, j: (0, 0))
    if nq:
        in_specs.append(vec)
        args.append(gq)
    if has_kv:
        in_specs.append(vec)
        args.append(gk)
    if rope:
        in_specs += [pl.BlockSpec((tm, LANES), row)] * 2
        args += [cos, sin]
    out_shape, out_specs = [], []
    if nq:
        out_shape.append(jax.ShapeDtypeStruct((t, nq * COL_TILE), jnp.bfloat16))
        out_specs.append(pl.BlockSpec((tm, COL_TILE), lambda i, j: (i, jnp.minimum(j, nq - 1))))
    if has_kv:
        for _ in range(4):
            out_shape.append(jax.ShapeDtypeStruct((t, HEADS_PER_VREG * COL_TILE), jnp.bfloat16))
            out_specs.append(pl.BlockSpec((tm, HEADS_PER_VREG * COL_TILE), row))
        for _ in range(2):
            out_shape.append(jax.ShapeDtypeStruct((t, COL_TILE), jnp.float32))
            out_specs.append(pl.BlockSpec((tm, COL_TILE), row))
    return pl.pallas_call(
        functools.partial(_proj_kernel, nq=nq, has_kv=has_kv, rope=rope,
                          n_x=len(x_specs), n_head=n_head),
        grid=(t // tm, n_col),
        in_specs=in_specs,
        out_specs=out_specs,
        out_shape=out_shape,
        scratch_shapes=[pltpu.VMEM((tm, d), jnp.bfloat16)],
        compiler_params=_params(("parallel", "arbitrary")),
        name="proj",
    )(*args)


def _attn_prompt_kernel(*refs, seq, band, n_prev, unroll, has_bias, has_sink):
    it = iter(refs)
    q_ref, kl_ref, kh_ref, vl_ref, vh_ref = (next(it) for _ in range(5))
    bias_ref = next(it) if has_bias else None
    sink_ref = next(it) if has_sink else None
    o_ref, s_scr, p_scr = next(it), next(it), next(it)
    row = lax.broadcasted_iota(jnp.int32, (band, LANES), 0)
    nt = (((1,), (1,)), ((), ()))
    tn = (((0,), (0,)), ((), ()))

    def key_start(c):
        return pl.multiple_of(jnp.maximum(c - n_prev, 0) * CHUNK, CHUNK)

    def group(gi, carry):
        for u in range(unroll):
            c = gi * unroll + u
            r0 = pl.multiple_of(c * CHUNK, CHUNK)
            k0 = key_start(c)
            qs = jnp.concatenate([q_ref[pl.ds(r0, CHUNK), p * LANES:(p + 1) * LANES]
                                  for p in range(PAIRS)], axis=0)
            for half, k_ref in enumerate((kl_ref, kh_ref)):
                s_scr[u, half] = lax.dot_general(k_ref[pl.ds(k0, band), :], qs, nt,
                                                 preferred_element_type=jnp.float32)
        for u in range(unroll):
            c = gi * unroll + u
            shift = pl.multiple_of(jnp.maximum(n_prev - c, 0) * CHUNK, CHUNK)
            for half in range(HEADS_PER_VREG):
                s = s_scr[u, half]
                if has_bias:
                    s = s + bias_ref[half, pl.ds(shift, band), :]
                s = jnp.where(row < band - shift, s, NEG_INF)
                m = jnp.max(s, axis=0, keepdims=True)
                if has_sink:
                    m = jnp.maximum(m, sink_ref[half])
                e = jnp.exp(s - m)
                l = jnp.sum(e, axis=0, keepdims=True)
                if has_sink:
                    l = l + jnp.exp(sink_ref[half] - m)
                p_scr[u, half] = (e * (1.0 / l)).astype(p_scr.dtype)
        for u in range(unroll):
            c = gi * unroll + u
            r0 = pl.multiple_of(c * CHUNK, CHUNK)
            k0 = key_start(c)
            o = (lax.dot_general(p_scr[u, 0], vl_ref[pl.ds(k0, band), :], tn,
                                 preferred_element_type=jnp.float32)
                 + lax.dot_general(p_scr[u, 1], vh_ref[pl.ds(k0, band), :], tn,
                                   preferred_element_type=jnp.float32))
            for p in range(PAIRS):
                o_ref[pl.ds(r0, CHUNK), p * LANES:(p + 1) * LANES] = (
                    o[p * CHUNK:(p + 1) * CHUNK].astype(o_ref.dtype))
        return carry

    lax.fori_loop(0, seq // (CHUNK * unroll), group, 0)


def _attend_prompt(q, kl, kh, vl, vh, bias, sink, *, nb, seq, n_prev):
    qw = q.shape[1]
    n_kv = kl.shape[1] // LANES
    band = (n_prev + 1) * CHUNK
    unroll = ATTN_UNROLL
    assert seq % (CHUNK * unroll) == 0 and seq >= band
    kv_spec = pl.BlockSpec((seq, LANES), lambda b, j: (b, j))
    in_specs = [pl.BlockSpec((seq, KV_GROUP_WIDTH), lambda b, j: (b, j))] + [kv_spec] * 4
    args = [q, kl, kh, vl, vh]
    if bias is not None:
        in_specs.append(pl.BlockSpec((HEADS_PER_VREG,) + bias.shape[1:], lambda b, j: (j, 0, 0)))
        args.append(bias)
    if sink is not None:
        in_specs.append(pl.BlockSpec((HEADS_PER_VREG, 1, LANES), lambda b, j: (j, 0, 0)))
        args.append(sink)
    return pl.pallas_call(
        functools.partial(_attn_prompt_kernel, seq=seq, band=band, n_prev=n_prev, unroll=unroll,
                          has_bias=bias is not None, has_sink=sink is not None),
        grid=(nb, n_kv),
        in_specs=in_specs,
        out_specs=pl.BlockSpec((seq, KV_GROUP_WIDTH), lambda b, j: (b, j)),
        out_shape=jax.ShapeDtypeStruct((nb * seq, qw), jnp.bfloat16),
        scratch_shapes=[pltpu.VMEM((unroll, HEADS_PER_VREG, band, LANES), jnp.float32),
                        pltpu.VMEM((unroll, HEADS_PER_VREG, band, LANES), jnp.bfloat16)],
        compiler_params=_params(("parallel", "arbitrary")),
        name="attn_prompt",
    )(*args)


def _attn_sample_kernel(*refs, lq, has_bias, has_sink, n_kv):
    it = iter(refs)
    q_ref, k_ref, v_ref = next(it), next(it), next(it)
    bias_ref = next(it) if has_bias else None
    sink_ref = next(it) if has_sink else None
    o_ref = next(it)
    for j in range(n_kv):
        qj = q_ref[:, j * KV_GROUP_WIDTH:(j + 1) * KV_GROUP_WIDTH]
        qcat = jnp.concatenate(
            [qj[:, g * HEAD_DIM:(g + 1) * HEAD_DIM] for g in range(GROUP)], axis=0)
        kj = k_ref[0, :, j * HEAD_DIM:(j + 1) * HEAD_DIM]
        vj = v_ref[0, :, j * HEAD_DIM:(j + 1) * HEAD_DIM]
        s = lax.dot_general(qcat, kj, (((1,), (1,)), ((), ())),
                            preferred_element_type=jnp.float32)
        if has_bias:
            s = s + bias_ref[j]
        m = jnp.max(s, axis=-1, keepdims=True)
        if has_sink:
            sink = sink_ref[j]
            m = jnp.maximum(m, sink)
        e = jnp.exp(s - m)
        l = jnp.sum(e, axis=-1, keepdims=True)
        if has_sink:
            l = l + jnp.exp(sink - m)
        p = (e * (1.0 / l)).astype(vj.dtype)
        o = jnp.dot(p, vj, preferred_element_type=jnp.float32)
        ocat = jnp.concatenate([o[g * lq:(g + 1) * lq] for g in range(GROUP)], axis=1)
        o_ref[:, j * KV_GROUP_WIDTH:(j + 1) * KV_GROUP_WIDTH] = ocat.astype(o_ref.dtype)


def _attend_sample(q, k, v, bias, sink, *, row0, lq):
    ns, lk, kw = k.shape
    qw = q.shape[1]
    assert row0 % lq == 0
    blk0 = row0 // lq
    in_specs = [pl.BlockSpec((lq, qw), lambda i: (blk0 + i, 0)),
                pl.BlockSpec((1, lk, kw), lambda i: (i, 0, 0)),
                pl.BlockSpec((1, lk, kw), lambda i: (i, 0, 0))]
    args = [q, k, v]
    if bias is not None:
        in_specs.append(pl.BlockSpec(bias.shape, lambda i: (0, 0, 0)))
        args.append(bias)
    if sink is not None:
        in_specs.append(pl.BlockSpec(sink.shape, lambda i: (0, 0, 0)))
        args.append(sink)
    return pl.pallas_call(
        functools.partial(_attn_sample_kernel, lq=lq, has_bias=bias is not None,
                          has_sink=sink is not None, n_kv=kw // HEAD_DIM),
        grid=(ns,),
        in_specs=in_specs,
        out_specs=pl.BlockSpec((lq, qw), lambda i: (i, 0)),
        out_shape=jax.ShapeDtypeStruct((ns * lq, qw), jnp.bfloat16),
        compiler_params=_params(("arbitrary",)),
        name="attn_sample",
    )(*args)


def _out_proj_kernel(*refs, n_o, o_head, n_x, x_head):
    o_refs, w_ref = refs[:n_o], refs[n_o]
    x_refs, y_ref = refs[n_o + 1:n_o + 1 + n_x], refs[n_o + 1 + n_x]
    y_ref[...] = _read_rows(x_refs, x_head) + jnp.dot(
        _read_rows(o_refs, o_head), w_ref[...], preferred_element_type=jnp.float32)


def _out_proj(o, w, x):
    t = _n_rows(x)
    d = w.shape[1]
    tm = ROW_TILE
    o_specs, o_head = _row_specs(o, tm, 0)
    x_specs, x_head = _row_specs(x, tm, 0)
    return pl.pallas_call(
        functools.partial(_out_proj_kernel, n_o=len(o_specs), o_head=o_head,
                          n_x=len(x_specs), x_head=x_head),
        grid=(t // tm,),
        in_specs=o_specs + [pl.BlockSpec(w.shape, lambda i: (0, 0))] + x_specs,
        out_specs=pl.BlockSpec((tm, d), lambda i: (i, 0)),
        out_shape=jax.ShapeDtypeStruct((t, d), jnp.float32),
        compiler_params=_params(("parallel",)),
        name="out_proj",
    )(*_as_list(o), w, *_as_list(x))


def _ffn_kernel(e_ref, n_ref, x_ref, g_ref, wg_ref, wu_ref, wd_ref, y_ref, h_scr, acc_scr, *,
                residual):
    f = pl.program_id(1)
    last = pl.num_programs(1) - 1

    @pl.when(pl.program_id(0) < n_ref[0])
    def _():
        @pl.when(f == 0)
        def _():
            h_scr[...] = _rms_rows(x_ref[...], g_ref[...]).astype(h_scr.dtype)

        h = h_scr[...]
        gate = jnp.dot(h, wg_ref[...], preferred_element_type=jnp.float32)
        up = jnp.dot(h, wu_ref[...], preferred_element_type=jnp.float32)
        act = (gate * jax.nn.sigmoid(gate)) * up
        part = jnp.dot(act.astype(jnp.bfloat16), wd_ref[...],
                       preferred_element_type=jnp.float32)

        @pl.when(f == 0)
        def _():
            acc_scr[...] = part

        @pl.when(f > 0)
        def _():
            acc_scr[...] += part

        @pl.when(f == last)
        def _():
            y_ref[...] = (x_ref[...] + acc_scr[...]) if residual else acc_scr[...]

    @pl.when((pl.program_id(0) >= n_ref[0]) & (f == last))
    def _():
        y_ref[...] = jnp.zeros_like(y_ref)


def _swiglu_blocks(x, g, wg, wu, wd, block_expert, n_used, *, tm, tf, residual):
    r, d = x.shape
    ff = wg.shape[2]
    assert r % tm == 0 and ff % tf == 0
    n_f = ff // tf

    def blk(i, n):
        return jnp.minimum(i, n[0] - 1)

    def col(i, f, n):
        return jnp.where(i < n[0], f, n_f - 1)

    grid_spec = pltpu.PrefetchScalarGridSpec(
        num_scalar_prefetch=2,
        grid=(r // tm, n_f),
        in_specs=[pl.BlockSpec((tm, d), lambda i, f, e, n: (blk(i, n), 0)),
                  pl.BlockSpec((1, d), lambda i, f, e, n: (0, 0)),
                  pl.BlockSpec((None, d, tf),
                               lambda i, f, e, n: (e[blk(i, n)], 0, col(i, f, n))),
                  pl.BlockSpec((None, d, tf),
                               lambda i, f, e, n: (e[blk(i, n)], 0, col(i, f, n))),
                  pl.BlockSpec((None, tf, d),
                               lambda i, f, e, n: (e[blk(i, n)], col(i, f, n), 0))],
        out_specs=pl.BlockSpec((tm, d), lambda i, f, e, n: (i, 0)),
        scratch_shapes=[pltpu.VMEM((tm, d), jnp.bfloat16),
                        pltpu.VMEM((tm, d), jnp.float32)],
    )
    return pl.pallas_call(
        functools.partial(_ffn_kernel, residual=residual),
        grid_spec=grid_spec,
        out_shape=jax.ShapeDtypeStruct((r, d), jnp.float32),
        compiler_params=_params(("arbitrary", "arbitrary")),
        name="swiglu",
    )(block_expert, n_used, x, g, wg, wu, wd)


def _router_kernel(x_ref, g_ref, w_ref, b_ref, idx_ref, gate_ref):
    h = _rms_rows(x_ref[...], g_ref[...]).astype(jnp.bfloat16)
    logits = jnp.dot(h, w_ref[...], preferred_element_type=jnp.float32) + b_ref[...]
    lane = lax.broadcasted_iota(jnp.int32, logits.shape, 1)
    m1 = jnp.max(logits, axis=-1, keepdims=True)
    e1 = jnp.min(jnp.where(logits == m1, lane, LANES), axis=-1, keepdims=True)
    rest = jnp.where(lane == e1, -jnp.inf, logits)
    m2 = jnp.max(rest, axis=-1, keepdims=True)
    e2 = jnp.min(jnp.where(rest == m2, lane, LANES), axis=-1, keepdims=True)
    t = jnp.exp(m2 - m1)
    denom = 1.0 + t
    idx_ref[...] = jnp.where(lane == 0, e1, jnp.where(lane == 1, e2, 0))
    gate_ref[...] = jnp.where(lane == 0, 1.0 / denom, jnp.where(lane == 1, t / denom, 0.0))


def _route(x, g, w_router, b_router):
    t, d = x.shape
    tm = ROW_TILE
    n_e = w_router.shape[1]
    w = jnp.zeros((d, LANES), jnp.bfloat16).at[:, :n_e].set(w_router.astype(jnp.bfloat16))
    b = jnp.full((1, LANES), -jnp.inf, jnp.float32).at[0, :n_e].set(b_router.astype(jnp.float32))
    return pl.pallas_call(
        _router_kernel,
        grid=(t // tm,),
        in_specs=[pl.BlockSpec((tm, d), lambda i: (i, 0)),
                  pl.BlockSpec((1, d), lambda i: (0, 0)),
                  pl.BlockSpec((d, LANES), lambda i: (0, 0)),
                  pl.BlockSpec((1, LANES), lambda i: (0, 0))],
        out_specs=[pl.BlockSpec((tm, LANES), lambda i: (i, 0))] * 2,
        out_shape=[jax.ShapeDtypeStruct((t, LANES), jnp.int32),
                   jax.ShapeDtypeStruct((t, LANES), jnp.float32)],
        compiler_params=_params(("parallel",)),
        name="router",
    )(x, g, w, b)


def _gather_kernel(idx_ref, src_ref, dst_ref, sem, *, rows):
    def row_copy(r):
        return pltpu.make_async_copy(src_ref.at[pl.ds(idx_ref[r], 1)],
                                     dst_ref.at[pl.ds(r, 1)], sem)

    def issue(r, carry):
        row_copy(r).start()
        return carry

    lax.fori_loop(0, rows, issue, 0)

    def drain(r, carry):
        row_copy(r).wait()
        return carry

    lax.fori_loop(0, rows, drain, 0)


def _gather_rows(src, idx, *, rows):
    n, d = src.shape
    r = idx.shape[0]
    assert r % rows == 0
    return pl.pallas_call(
        functools.partial(_gather_kernel, rows=rows),
        grid=(r // rows,),
        in_specs=[pl.BlockSpec((rows,), lambda i: (i,), memory_space=pltpu.SMEM),
                  pl.BlockSpec(memory_space=pl.ANY)],
        out_specs=pl.BlockSpec((rows, d), lambda i: (i, 0)),
        out_shape=jax.ShapeDtypeStruct((r, d), src.dtype),
        scratch_shapes=[pltpu.SemaphoreType.DMA(())],
        compiler_params=_params(("arbitrary",)),
        name="gather_rows",
    )(idx, src)


def _combine_kernel(d0_ref, d1_ref, x_ref, gate_ref, y_hbm, *rest, rows, n_head):
    out_refs, (buf0, buf1, sem) = rest[:-3], rest[-3:]

    def copies(r):
        return (pltpu.make_async_copy(y_hbm.at[pl.ds(d0_ref[r], 1)], buf0.at[pl.ds(r, 1)],
                                      sem.at[0]),
                pltpu.make_async_copy(y_hbm.at[pl.ds(d1_ref[r], 1)], buf1.at[pl.ds(r, 1)],
                                      sem.at[1]))

    def issue(r, carry):
        c0, c1 = copies(r)
        c0.start()
        c1.start()
        return carry

    lax.fori_loop(0, rows, issue, 0)

    def drain(r, carry):
        c0, c1 = copies(r)
        c0.wait()
        c1.wait()
        return carry

    lax.fori_loop(0, rows, drain, 0)
    gates = gate_ref[...]
    y = gates[:, 0:1] * buf0[...] + gates[:, 1:2] * buf1[...]
    out = x_ref[...] + y
    if len(out_refs) == 1:
        out_refs[0][...] = out
    else:
        @pl.when(pl.program_id(0) < n_head)
        def _():
            out_refs[0][...] = out

        @pl.when(pl.program_id(0) >= n_head)
        def _():
            out_refs[1][...] = out


def _combine(x, gates, y_rows, d0, d1, split_at=None):
    t, d = x.shape
    rows = COMBINE_ROWS
    assert t % rows == 0
    idx_spec = pl.BlockSpec((rows,), lambda i: (i,), memory_space=pltpu.SMEM)
    if split_at is None:
        n_head = 0
        out_specs = pl.BlockSpec((rows, d), lambda i: (i, 0))
        out_shape = jax.ShapeDtypeStruct((t, d), jnp.float32)
    else:
        assert split_at % rows == 0 and 0 < split_at < t
        n_head = split_at // rows
        out_specs = [pl.BlockSpec((rows, d), lambda i: (jnp.minimum(i, n_head - 1), 0)),
                     pl.BlockSpec((rows, d), lambda i: (jnp.maximum(i - n_head, 0), 0))]
        out_shape = [jax.ShapeDtypeStruct((split_at, d), jnp.float32),
                     jax.ShapeDtypeStruct((t - split_at, d), jnp.float32)]
    return pl.pallas_call(
        functools.partial(_combine_kernel, rows=rows, n_head=n_head),
        grid=(t // rows,),
        in_specs=[idx_spec, idx_spec,
                  pl.BlockSpec((rows, d), lambda i: (i, 0)),
                  pl.BlockSpec((rows, LANES), lambda i: (i, 0)),
                  pl.BlockSpec(memory_space=pl.ANY)],
        out_specs=out_specs,
        out_shape=out_shape,
        scratch_shapes=[pltpu.VMEM((rows, d), jnp.float32),
                        pltpu.VMEM((rows, d), jnp.float32),
                        pltpu.SemaphoreType.DMA((2,))],
        compiler_params=_params(("arbitrary",)),
        name="combine",
    )(d0, d1, x, gates, y_rows)


def _moe(x, g, w_router, b_router, wg, wu, wd, split_at=None):
    t, d = x.shape
    n_e = wg.shape[0]
    n_assign = t * TOP_K
    top_idx, gates = _route(x, g, w_router, b_router)
    flat_e = top_idx[:, :TOP_K].reshape(-1)
    onehot = (flat_e[:, None] == jnp.arange(n_e, dtype=jnp.int32)[None, :]).astype(jnp.int32)
    rank = jnp.sum((jnp.cumsum(onehot, axis=0) - onehot) * onehot, axis=1)
    counts = jnp.sum(onehot, axis=0)
    padded = (counts + MOE_ROWS - 1) // MOE_ROWS * MOE_ROWS
    ends = jnp.cumsum(padded)
    dest = jnp.sum((ends - padded)[None, :] * onehot, axis=1) + rank
    n_blocks = (n_assign + n_e * (MOE_ROWS - 1) + MOE_ROWS - 1) // MOE_ROWS
    rows = n_blocks * MOE_ROWS
    flat_tok = jnp.arange(n_assign, dtype=jnp.int32) // TOP_K
    row_tok = jnp.zeros((rows,), jnp.int32).at[dest].set(flat_tok)
    block_start = jnp.arange(n_blocks, dtype=jnp.int32) * MOE_ROWS
    block_expert = jnp.minimum(
        jnp.sum((ends[None, :] <= block_start[:, None]).astype(jnp.int32), axis=1), n_e - 1)
    n_used = (ends[-1:] // MOE_ROWS).astype(jnp.int32)
    x_rows = _gather_rows(x, row_tok, rows=MOE_ROWS)
    y_rows = _swiglu_blocks(x_rows, g, wg, wu, wd, block_expert, n_used, tm=MOE_ROWS,
                            tf=MOE_COL_TILE, residual=False)
    dest2 = dest.reshape(t, TOP_K).astype(jnp.int32)
    return _combine(x, gates, y_rows, dest2[:, 0], dest2[:, 1], split_at)


def _rope_tables(pos):
    half = HEAD_DIM // 2
    inv_freq = 1.0 / (ROPE_THETA ** (jnp.arange(0, HEAD_DIM, 2, dtype=jnp.float32) / HEAD_DIM))
    ang = pos.astype(jnp.float32)[:, None] * inv_freq[None, :]
    cos, sin = jnp.cos(ang), jnp.sin(ang)
    reps = LANES // half
    signs = jnp.tile(jnp.concatenate([-jnp.ones((half,)), jnp.ones((half,))]), HEADS_PER_VREG)
    return jnp.tile(cos, (1, reps)), jnp.tile(sin, (1, reps)) * signs[None, :].astype(jnp.float32)


def _rel_bias(table, lq, n_before):
    rel = jnp.arange(lq)[:, None] - (jnp.arange(n_before + lq) - n_before)[None, :]
    idx = jnp.clip(rel, REL_MIN, REL_MAX) - REL_MIN
    return table.astype(jnp.float32)[:, idx]


def _pair_major(a):
    h = a.shape[0]
    a = a.reshape((h // GROUP, PAIRS, HEADS_PER_VREG) + a.shape[1:])
    return jnp.swapaxes(a, 1, 2)


def _prompt_bias(table, n_prev):
    pad = n_prev * CHUNK
    b = _pair_major(_rel_bias(table, CHUNK, pad))
    kvh = b.shape[0]
    b = jnp.transpose(b, (0, 1, 4, 2, 3)).reshape(kvh * HEADS_PER_VREG, pad + CHUNK, LANES)
    return jnp.pad(b, ((0, 0), (0, pad), (0, 0)))


def _prompt_sink(sink):
    s = _pair_major(sink.astype(jnp.float32))
    s = jnp.repeat(s[..., None], CHUNK, axis=-1)
    return s.reshape(-1, 1, LANES)


def _sample_bias(table, lq, n_cached):
    b = _rel_bias(table, lq, n_cached)
    return b.reshape(b.shape[0] // GROUP, GROUP * lq, n_cached + lq)


def _sample_sink(sink, lq):
    h = sink.shape[0]
    s = jnp.repeat(sink.astype(jnp.float32).reshape(h // GROUP, GROUP, 1), lq, axis=2)
    return s.reshape(h // GROUP, GROUP * lq, 1)


def _head_gain(g):
    return jnp.tile(g.astype(jnp.float32), HEADS_PER_VREG)[None, :]


def kernel(x_prompt, x_sample, cache_k_a, cache_v_a, cache_k_b, cache_v_b, g_attn, g_ffn, w_qkv_a, g_q_a, g_k_a, sink_a, w_o_a, g_kv_b, w_kv_b, g_k_b, w_q_b, g_q_b, rel_bias_b, w_o_b, w_gate_ffn, w_up_ffn, w_down_ffn, w_router, b_router, w_gate_moe, w_up_moe, w_down_moe):
    bf16 = jnp.bfloat16
    nb, seq, d = x_prompt.shape
    ns, seq_s, _ = x_sample.shape
    tp, ts = nb * seq, ns * seq_s
    t = tp + ts
    depth = g_attn.shape[0]
    n_a = w_qkv_a.shape[0]
    n_kv = cache_k_b.shape[2]
    kvw = n_kv * HEAD_DIM
    keep_a, keep_b = min(WINDOW_A, seq), min(BAND_B, seq)
    assert t % ROW_TILE == 0

    x = (x_prompt.reshape(tp, d), x_sample.reshape(ts, d))
    pos = jnp.concatenate([jnp.tile(jnp.arange(seq), nb),
                           jnp.tile(PAST_LEN + jnp.arange(seq_s), ns)])
    cos, sin = _rope_tables(pos)
    row = lambda v: v.astype(jnp.float32)[None, :]

    def attend(q, kv_pad, k_f32, v_f32, cache_k, cache_v, n_prev, table, sink):
        bias_p = None if table is None else _prompt_bias(table, n_prev)
        sink_p = None if sink is None else _prompt_sink(sink)
        o_p = _attend_prompt(q, *kv_pad, bias_p, sink_p, nb=nb, seq=seq, n_prev=n_prev)
        n_cached = cache_k.shape[1]
        ks = jnp.concatenate([cache_k.reshape(ns, n_cached, kvw),
                              k_f32[tp:].reshape(ns, seq_s, kvw)], axis=1).astype(bf16)
        vs = jnp.concatenate([cache_v.reshape(ns, n_cached, kvw),
                              v_f32[tp:].reshape(ns, seq_s, kvw)], axis=1).astype(bf16)
        bias_s = None if table is None else _sample_bias(table, seq_s, n_cached)
        sink_s = None if sink is None else _sample_sink(sink, seq_s)
        return o_p, _attend_sample(q, ks, vs, bias_s, sink_s, row0=tp, lq=seq_s)

    def tail(c, n_keep):
        rows = jnp.stack([c[(b + 1) * seq - n_keep:(b + 1) * seq] for b in range(nb)])
        return rows.reshape(nb, n_keep, n_kv, HEAD_DIM)

    def rolled(cache, c):
        new = c[tp:].reshape(ns, seq_s, n_kv, HEAD_DIM)
        return jnp.concatenate([cache, new], axis=1)[:, seq_s:]

    ka_p, va_p, ka_s, va_s = [], [], [], []
    kvb_pad = kb_f = vb_f = None
    for layer in range(depth):
        g_in = row(g_attn[layer])
        if layer < n_a:
            i = layer
            q, kl, kh, vl, vh, k_f32, v_f32 = _project(
                x, g_in, w_qkv_a[i].astype(bf16), _head_gain(g_q_a[i]), _head_gain(g_k_a[i]),
                cos, sin, nq=(w_qkv_a.shape[2] - 2 * kvw) // COL_TILE, has_kv=True, rope=True)
            o = attend(q, (kl, kh, vl, vh), k_f32, v_f32, cache_k_a[i], cache_v_a[i],
                       PREV_CHUNKS_A, None, sink_a[i])
            x = _out_proj(o, w_o_a[i].astype(bf16), x)
            ka_p.append(tail(k_f32, keep_a))
            va_p.append(tail(v_f32, keep_a))
            ka_s.append(rolled(cache_k_a[i], k_f32))
            va_s.append(rolled(cache_v_a[i], v_f32))
        else:
            jb = layer - n_a
            (q,) = _project(x, g_in, w_q_b[jb].astype(bf16), _head_gain(g_q_b[jb]), None,
                            None, None, nq=w_q_b.shape[2] // COL_TILE, has_kv=False, rope=False)
            o = attend(q, kvb_pad, kb_f, vb_f, cache_k_b, cache_v_b, PREV_CHUNKS_B,
                       rel_bias_b[jb], None)
            x = _out_proj(o, w_o_b[jb].astype(bf16), x)

        g_mid = row(g_ffn[layer])
        m = layer // 2
        if layer % 2 == 0:
            n_tiles = t // ROW_TILE
            x = _swiglu_blocks(x, g_mid, w_gate_ffn[m:m + 1].astype(bf16),
                               w_up_ffn[m:m + 1].astype(bf16), w_down_ffn[m:m + 1].astype(bf16),
                               jnp.zeros((n_tiles,), jnp.int32),
                               jnp.full((1,), n_tiles, jnp.int32), tm=ROW_TILE, tf=COL_TILE,
                               residual=True)
        else:
            x = _moe(x, g_mid, w_router[m], b_router[m], w_gate_moe[m].astype(bf16),
                     w_up_moe[m].astype(bf16), w_down_moe[m].astype(bf16),
                     split_at=tp if layer == depth - 1 else None)

        if layer == n_a - 1:
            *kvb_pad, kb_f, vb_f = _project(x, row(g_kv_b), w_kv_b.astype(bf16), None,
                                            _head_gain(g_k_b), None, None,
                                            nq=0, has_kv=True, rope=False)

    if not isinstance(x, tuple):
        x = (x[:tp], x[tp:])
    y_prompt = x[0].reshape(nb, seq, d)
    y_sample = x[1].reshape(ns, seq_s, d)
    return (y_prompt, y_sample, jnp.stack(ka_p), jnp.stack(va_p), tail(kb_f, keep_b),
            tail(vb_f, keep_b), jnp.stack(ka_s), jnp.stack(va_s),
            rolled(cache_k_b, kb_f), rolled(cache_v_b, vb_f))
```

```python
import functools

import jax
import jax.numpy as jnp
from jax import lax
from jax.experimental import pallas as pl
from jax.experimental.pallas import tpu as pltpu

CHUNK = 64
HEAD_DIM = 64
GROUP = 4
WINDOW_A = 128
PREV_CHUNKS_A = WINDOW_A // CHUNK
PREV_CHUNKS_B = 8
BAND_B = PREV_CHUNKS_B * CHUNK
REL_MIN = -(CHUNK - 1)
REL_MAX = 256
ROPE_THETA = 10000.0
TOP_K = 2
RMS_EPS = 1e-6
NEG_INF = -1e30
PAST_LEN = 1024

LANES = 128
HEADS_PER_VREG = LANES // HEAD_DIM
KV_GROUP_WIDTH = GROUP * HEAD_DIM
PAIRS = GROUP // HEADS_PER_VREG

ROW_TILE = 512
COL_TILE = 512
MOE_ROWS = 512
MOE_COL_TILE = 1024
COMBINE_ROWS = 256
ATTN_UNROLL = 4
VMEM_LIMIT = 56 * 1024 * 1024


def _params(semantics):
    return pltpu.CompilerParams(dimension_semantics=semantics, vmem_limit_bytes=VMEM_LIMIT)


def _rms_rows(x, g):
    ms = jnp.sum(x * x, axis=-1, keepdims=True) * (1.0 / x.shape[-1])
    return (x * lax.rsqrt(ms + RMS_EPS)) * g


def _head_norm(y, gain, cos, sin):
    lane = lax.broadcasted_iota(jnp.int32, y.shape, 1)
    low = lane < HEAD_DIM
    ss = y * y
    s_lo = jnp.sum(jnp.where(low, ss, 0.0), axis=-1, keepdims=True)
    s_hi = jnp.sum(jnp.where(low, 0.0, ss), axis=-1, keepdims=True)
    ms = jnp.where(low, s_lo, s_hi) * (1.0 / HEAD_DIM)
    yn = (y * lax.rsqrt(ms + RMS_EPS)) * gain
    if cos is None:
        return yn
    half = HEAD_DIM // 2
    fwd = pltpu.roll(yn, half, 1)
    bwd = pltpu.roll(yn, LANES - half, 1)
    rot = jnp.where((lane % HEAD_DIM) < half, bwd, fwd)
    return yn * cos + rot * sin


def _store_half_padded(lo_ref, hi_ref, c, z):
    lane = lax.broadcasted_iota(jnp.int32, z.shape, 1)
    low = lane < HEAD_DIM
    zr = pltpu.roll(z, HEAD_DIM, 1)
    dt = lo_ref.dtype
    a, b = 2 * c * LANES, (2 * c + 1) * LANES
    lo_ref[:, a:a + LANES] = jnp.where(low, z, 0.0).astype(dt)
    lo_ref[:, b:b + LANES] = jnp.where(low, zr, 0.0).astype(dt)
    hi_ref[:, a:a + LANES] = jnp.where(low, 0.0, zr).astype(dt)
    hi_ref[:, b:b + LANES] = jnp.where(low, 0.0, z).astype(dt)


def _row_specs(x, tm):
    def spec(cols, fn):
        return pl.BlockSpec((tm, cols), lambda i: (fn(i), 0))

    if not isinstance(x, tuple):
        return [spec(x.shape[1], lambda i: i)], 0
    head, tail = x
    assert head.shape[0] % tm == 0 and tail.shape[0] % tm == 0 and head.shape[1] == tail.shape[1]
    n_head = head.shape[0] // tm
    return [spec(head.shape[1], lambda i: jnp.minimum(i, n_head - 1)),
            spec(head.shape[1], lambda i: jnp.maximum(i - n_head, 0))], n_head


def _read_rows(x_refs, n_head):
    if len(x_refs) == 1:
        return x_refs[0][...]
    return jnp.where(pl.program_id(0) < n_head, x_refs[0][...], x_refs[1][...])


def _n_rows(x):
    return x[0].shape[0] + x[1].shape[0] if isinstance(x, tuple) else x.shape[0]


def _as_list(x):
    return list(x) if isinstance(x, tuple) else [x]


def _proj_kernel(*refs, nq, has_kv, rope, n_x, n_head):
    it = iter(refs)
    x_refs = [next(it) for _ in range(n_x)]
    g_ref, w_ref = next(it), next(it)
    gq_ref = next(it) if nq else None
    gk_ref = next(it) if has_kv else None
    cos_ref, sin_ref = (next(it), next(it)) if rope else (None, None)
    q_ref = next(it) if nq else None
    if has_kv:
        kl_ref, kh_ref, vl_ref, vh_ref, kf_ref, vf_ref = (next(it) for _ in range(6))
    h_scr = next(it)
    h_scr[...] = _rms_rows(_read_rows(x_refs, n_head), g_ref[...]).astype(h_scr.dtype)
    cos = cos_ref[...] if rope else None
    sin = sin_ref[...] if rope else None
    groups = COL_TILE // LANES
    for j in range(nq + (2 if has_kv else 0)):
        y = jnp.dot(h_scr[...], w_ref[j], preferred_element_type=jnp.float32)
        for c in range(groups):
            yc = y[:, c * LANES:(c + 1) * LANES]
            if j < nq:
                z = _head_norm(yc, gq_ref[...], cos, sin) * (HEAD_DIM ** -0.5)
                q_ref[:, (j * groups + c) * LANES:(j * groups + c + 1) * LANES] = z.astype(q_ref.dtype)
            elif j == nq:
                z = _head_norm(yc, gk_ref[...], cos, sin)
                kf_ref[:, c * LANES:(c + 1) * LANES] = z
                _store_half_padded(kl_ref, kh_ref, c, z)
            else:
                vf_ref[:, c * LANES:(c + 1) * LANES] = yc
                _store_half_padded(vl_ref, vh_ref, c, yc)


def _project(x, g, w, gq, gk, cos, sin, *, nq, has_kv, rope):
    t = _n_rows(x)
    d = w.shape[0]
    tm = ROW_TILE
    n_col = nq + (2 if has_kv else 0)
    assert w.shape == (d, n_col * COL_TILE) and t % tm == 0
    w = jnp.swapaxes(w.reshape(d, n_col, COL_TILE), 0, 1)
    row = lambda i: (i, 0)
    x_specs, n_head = _row_specs(x, tm)
    in_specs = x_specs + [pl.BlockSpec((1, d), lambda i: (0, 0)),
                          pl.BlockSpec((n_col, d, COL_TILE), lambda i: (0, 0, 0))]
    args = _as_list(x) + [g, w]
    vec = pl.BlockSpec((1, LANES), lambda i: (0, 0))
    if nq:
        in_specs.append(vec)
        args.append(gq)
    if has_kv:
        in_specs.append(vec)
        args.append(gk)
    if rope:
        in_specs += [pl.BlockSpec((tm, LANES), row)] * 2
        args += [cos, sin]
    out_shape, out_specs = [], []
    if nq:
        out_shape.append(jax.ShapeDtypeStruct((t, nq * COL_TILE), jnp.bfloat16))
        out_specs.append(pl.BlockSpec((tm, nq * COL_TILE), row))
    if has_kv:
        for _ in range(4):
            out_shape.append(jax.ShapeDtypeStruct((t, HEADS_PER_VREG * COL_TILE), jnp.bfloat16))
            out_specs.append(pl.BlockSpec((tm, HEADS_PER_VREG * COL_TILE), row))
        for _ in range(2):
            out_shape.append(jax.ShapeDtypeStruct((t, COL_TILE), jnp.float32))
            out_specs.append(pl.BlockSpec((tm, COL_TILE), row))
    return pl.pallas_call(
        functools.partial(_proj_kernel, nq=nq, has_kv=has_kv, rope=rope,
                          n_x=len(x_specs), n_head=n_head),
        grid=(t // tm,),
        in_specs=in_specs,
        out_specs=out_specs,
        out_shape=out_shape,
        scratch_shapes=[pltpu.VMEM((tm, d), jnp.bfloat16)],
        compiler_params=_params(("parallel",)),
        name="proj",
    )(*args)


def _attn_prompt_kernel(*refs, seq, band, n_prev, unroll, has_bias, has_sink):
    it = iter(refs)
    q_ref, kl_ref, kh_ref, vl_ref, vh_ref = (next(it) for _ in range(5))
    bias_ref = next(it) if has_bias else None
    sink_ref = next(it) if has_sink else None
    o_ref, s_scr, p_scr = next(it), next(it), next(it)
    row = lax.broadcasted_iota(jnp.int32, (band, LANES), 0)
    nt = (((1,), (1,)), ((), ()))
    tn = (((0,), (0,)), ((), ()))

    def key_start(c):
        return pl.multiple_of(jnp.maximum(c - n_prev, 0) * CHUNK, CHUNK)

    def group(gi, carry):
        for u in range(unroll):
            c = gi * unroll + u
            r0 = pl.multiple_of(c * CHUNK, CHUNK)
            k0 = key_start(c)
            qs = jnp.concatenate([q_ref[pl.ds(r0, CHUNK), p * LANES:(p + 1) * LANES]
                                  for p in range(PAIRS)], axis=0)
            for half, k_ref in enumerate((kl_ref, kh_ref)):
                s_scr[u, half] = lax.dot_general(k_ref[pl.ds(k0, band), :], qs, nt,
                                                 preferred_element_type=jnp.float32)
        for u in range(unroll):
            c = gi * unroll + u
            shift = pl.multiple_of(jnp.maximum(n_prev - c, 0) * CHUNK, CHUNK)
            for half in range(HEADS_PER_VREG):
                s = s_scr[u, half]
                if has_bias:
                    s = s + bias_ref[half, pl.ds(shift, band), :]
                s = jnp.where(row < band - shift, s, NEG_INF)
                m = jnp.max(s, axis=0, keepdims=True)
                if has_sink:
                    m = jnp.maximum(m, sink_ref[half])
                e = jnp.exp(s - m)
                l = jnp.sum(e, axis=0, keepdims=True)
                if has_sink:
                    l = l + jnp.exp(sink_ref[half] - m)
                p_scr[u, half] = (e * (1.0 / l)).astype(p_scr.dtype)
        for u in range(unroll):
            c = gi * unroll + u
            r0 = pl.multiple_of(c * CHUNK, CHUNK)
            k0 = key_start(c)
            o = (lax.dot_general(p_scr[u, 0], vl_ref[pl.ds(k0, band), :], tn,
                                 preferred_element_type=jnp.float32)
                 + lax.dot_general(p_scr[u, 1], vh_ref[pl.ds(k0, band), :], tn,
                                   preferred_element_type=jnp.float32))
            for p in range(PAIRS):
                o_ref[pl.ds(r0, CHUNK), p * LANES:(p + 1) * LANES] = (
                    o[p * CHUNK:(p + 1) * CHUNK].astype(o_ref.dtype))
        return carry

    lax.fori_loop(0, seq // (CHUNK * unroll), group, 0)


def _attend_prompt(q, kl, kh, vl, vh, bias, sink, *, nb, seq, n_prev):
    qw = q.shape[1]
    n_kv = kl.shape[1] // LANES
    band = (n_prev + 1) * CHUNK
    unroll = ATTN_UNROLL
    assert seq % (CHUNK * unroll) == 0 and seq >= band
    kv_spec = pl.BlockSpec((seq, LANES), lambda b, j: (b, j))
    in_specs = [pl.BlockSpec((seq, KV_GROUP_WIDTH), lambda b, j: (b, j))] + [kv_spec] * 4
    args = [q, kl, kh, vl, vh]
    if bias is not None:
        in_specs.append(pl.BlockSpec((HEADS_PER_VREG,) + bias.shape[1:], lambda b, j: (j, 0, 0)))
        args.append(bias)
    if sink is not None:
        in_specs.append(pl.BlockSpec((HEADS_PER_VREG, 1, LANES), lambda b, j: (j, 0, 0)))
        args.append(sink)
    return pl.pallas_call(
        functools.partial(_attn_prompt_kernel, seq=seq, band=band, n_prev=n_prev, unroll=unroll,
                          has_bias=bias is not None, has_sink=sink is not None),
        grid=(nb, n_kv),
        in_specs=in_specs,
        out_specs=pl.BlockSpec((seq, KV_GROUP_WIDTH), lambda b, j: (b, j)),
        out_shape=jax.ShapeDtypeStruct((nb * seq, qw), jnp.bfloat16),
        scratch_shapes=[pltpu.VMEM((unroll, HEADS_PER_VREG, band, LANES), jnp.float32),
                        pltpu.VMEM((unroll, HEADS_PER_VREG, band, LANES), jnp.bfloat16)],
        compiler_params=_params(("parallel", "arbitrary")),
        name="attn_prompt",
    )(*args)


def _attn_sample_kernel(*refs, lq, has_bias, has_sink, n_kv):
    it = iter(refs)
    q_ref, k_ref, v_ref = next(it), next(it), next(it)
    bias_ref = next(it) if has_bias else None
    sink_ref = next(it) if has_sink else None
    o_ref = next(it)
    for j in range(n_kv):
        qj = q_ref[:, j * KV_GROUP_WIDTH:(j + 1) * KV_GROUP_WIDTH]
        qcat = jnp.concatenate(
            [qj[:, g * HEAD_DIM:(g + 1) * HEAD_DIM] for g in range(GROUP)], axis=0)
        kj = k_ref[0, :, j * HEAD_DIM:(j + 1) * HEAD_DIM]
        vj = v_ref[0, :, j * HEAD_DIM:(j + 1) * HEAD_DIM]
        s = lax.dot_general(qcat, kj, (((1,), (1,)), ((), ())),
                            preferred_element_type=jnp.float32)
        if has_bias:
            s = s + bias_ref[j]
        m = jnp.max(s, axis=-1, keepdims=True)
        if has_sink:
            sink = sink_ref[j]
            m = jnp.maximum(m, sink)
        e = jnp.exp(s - m)
        l = jnp.sum(e, axis=-1, keepdims=True)
        if has_sink:
            l = l + jnp.exp(sink - m)
        p = (e * (1.0 / l)).astype(vj.dtype)
        o = jnp.dot(p, vj, preferred_element_type=jnp.float32)
        ocat = jnp.concatenate([o[g * lq:(g + 1) * lq] for g in range(GROUP)], axis=1)
        o_ref[:, j * KV_GROUP_WIDTH:(j + 1) * KV_GROUP_WIDTH] = ocat.astype(o_ref.dtype)


def _attend_sample(q, k, v, bias, sink, *, row0, lq):
    ns, lk, kw = k.shape
    qw = q.shape[1]
    assert row0 % lq == 0
    blk0 = row0 // lq
    in_specs = [pl.BlockSpec((lq, qw), lambda i: (blk0 + i, 0)),
                pl.BlockSpec((1, lk, kw), lambda i: (i, 0, 0)),
                pl.BlockSpec((1, lk, kw), lambda i: (i, 0, 0))]
    args = [q, k, v]
    if bias is not None:
        in_specs.append(pl.BlockSpec(bias.shape, lambda i: (0, 0, 0)))
        args.append(bias)
    if sink is not None:
        in_specs.append(pl.BlockSpec(sink.shape, lambda i: (0, 0, 0)))
        args.append(sink)
    return pl.pallas_call(
        functools.partial(_attn_sample_kernel, lq=lq, has_bias=bias is not None,
                          has_sink=sink is not None, n_kv=kw // HEAD_DIM),
        grid=(ns,),
        in_specs=in_specs,
        out_specs=pl.BlockSpec((lq, qw), lambda i: (i, 0)),
        out_shape=jax.ShapeDtypeStruct((ns * lq, qw), jnp.bfloat16),
        compiler_params=_params(("arbitrary",)),
        name="attn_sample",
    )(*args)


def _out_proj_kernel(*refs, n_o, o_head, n_x, x_head):
    o_refs, w_ref = refs[:n_o], refs[n_o]
    x_refs, y_ref = refs[n_o + 1:n_o + 1 + n_x], refs[n_o + 1 + n_x]
    y_ref[...] = _read_rows(x_refs, x_head) + jnp.dot(
        _read_rows(o_refs, o_head), w_ref[...], preferred_element_type=jnp.float32)


def _out_proj(o, w, x):
    t = _n_rows(x)
    d = w.shape[1]
    tm = ROW_TILE
    o_specs, o_head = _row_specs(o, tm)
    x_specs, x_head = _row_specs(x, tm)
    return pl.pallas_call(
        functools.partial(_out_proj_kernel, n_o=len(o_specs), o_head=o_head,
                          n_x=len(x_specs), x_head=x_head),
        grid=(t // tm,),
        in_specs=o_specs + [pl.BlockSpec(w.shape, lambda i: (0, 0))] + x_specs,
        out_specs=pl.BlockSpec((tm, d), lambda i: (i, 0)),
        out_shape=jax.ShapeDtypeStruct((t, d), jnp.float32),
        compiler_params=_params(("parallel",)),
        name="out_proj",
    )(*_as_list(o), w, *_as_list(x))


def _ffn_kernel(e_ref, n_ref, x_ref, g_ref, wg_ref, wu_ref, wd_ref, y_ref, h_scr, acc_scr, *,
                residual, token_major):
    f = pl.program_id(1)
    last = pl.num_programs(1) - 1

    @pl.when(pl.program_id(0) < n_ref[0])
    def _():
        @pl.when(f == 0)
        def _():
            if token_major:
                h_scr[...] = x_ref[...].reshape(h_scr.shape)
            else:
                h_scr[...] = _rms_rows(x_ref[...], g_ref[...]).astype(h_scr.dtype)

        h = h_scr[...]
        gate = jnp.dot(h, wg_ref[...], preferred_element_type=jnp.float32)
        up = jnp.dot(h, wu_ref[...], preferred_element_type=jnp.float32)
        act = (gate * jax.nn.sigmoid(gate)) * up
        part = jnp.dot(act.astype(jnp.bfloat16), wd_ref[...],
                       preferred_element_type=jnp.float32)

        @pl.when(f == 0)
        def _():
            acc_scr[...] = part

        @pl.when(f > 0)
        def _():
            acc_scr[...] += part

        @pl.when(f == last)
        def _():
            if token_major:
                y_ref[...] = acc_scr[...].reshape(y_ref.shape)
            else:
                y_ref[...] = (x_ref[...] + acc_scr[...]) if residual else acc_scr[...]

    @pl.when((pl.program_id(0) >= n_ref[0]) & (f == last))
    def _():
        y_ref[...] = jnp.zeros_like(y_ref)


def _swiglu_blocks(x, g, wg, wu, wd, block_expert, n_used, *, tm, tf, residual):
    token_major = x.ndim == 3
    assert not (token_major and residual)
    r, d = x.shape[0], wg.shape[1]
    row_block = (tm,) + x.shape[1:]
    zeros = (0,) * (x.ndim - 1)
    ff = wg.shape[2]
    assert r % tm == 0 and ff % tf == 0
    n_f = ff // tf

    def blk(i, n):
        return jnp.minimum(i, n[0] - 1)

    def col(i, f, n):
        return jnp.where(i < n[0], f, n_f - 1)

    grid_spec = pltpu.PrefetchScalarGridSpec(
        num_scalar_prefetch=2,
        grid=(r // tm, n_f),
        in_specs=[pl.BlockSpec(row_block, lambda i, f, e, n: (blk(i, n),) + zeros),
                  pl.BlockSpec((1, d), lambda i, f, e, n: (0, 0)),
                  pl.BlockSpec((None, d, tf),
                               lambda i, f, e, n: (e[blk(i, n)], 0, col(i, f, n))),
                  pl.BlockSpec((None, d, tf),
                               lambda i, f, e, n: (e[blk(i, n)], 0, col(i, f, n))),
                  pl.BlockSpec((None, tf, d),
                               lambda i, f, e, n: (e[blk(i, n)], col(i, f, n), 0))],
        out_specs=pl.BlockSpec(row_block, lambda i, f, e, n: (i,) + zeros),
        scratch_shapes=[pltpu.VMEM((tm, d), jnp.bfloat16),
                        pltpu.VMEM((tm, d), jnp.float32)],
    )
    return pl.pallas_call(
        functools.partial(_ffn_kernel, residual=residual, token_major=token_major),
        grid_spec=grid_spec,
        out_shape=jax.ShapeDtypeStruct(x.shape, jnp.float32),
        compiler_params=_params(("arbitrary", "arbitrary")),
        name="swiglu",
    )(block_expert, n_used, x, g, wg, wu, wd)


def _router_kernel(x_ref, g_ref, w_ref, b_ref, idx_ref, gate_ref, h_ref):
    h = _rms_rows(x_ref[...], g_ref[...]).astype(jnp.bfloat16)
    h_ref[...] = h.reshape(h_ref.shape)
    logits = jnp.dot(h, w_ref[...], preferred_element_type=jnp.float32) + b_ref[...]
    lane = lax.broadcasted_iota(jnp.int32, logits.shape, 1)
    m1 = jnp.max(logits, axis=-1, keepdims=True)
    e1 = jnp.min(jnp.where(logits == m1, lane, LANES), axis=-1, keepdims=True)
    rest = jnp.where(lane == e1, -jnp.inf, logits)
    m2 = jnp.max(rest, axis=-1, keepdims=True)
    e2 = jnp.min(jnp.where(rest == m2, lane, LANES), axis=-1, keepdims=True)
    t = jnp.exp(m2 - m1)
    denom = 1.0 + t
    idx_ref[...] = jnp.where(lane == 0, e1, jnp.where(lane == 1, e2, 0))
    gate_ref[...] = jnp.where(lane == 0, 1.0 / denom, jnp.where(lane == 1, t / denom, 0.0))


def _route(x, g, w_router, b_router):
    t, d = x.shape
    tm = ROW_TILE
    n_e = w_router.shape[1]
    w = jnp.zeros((d, LANES), jnp.bfloat16).at[:, :n_e].set(w_router.astype(jnp.bfloat16))
    b = jnp.full((1, LANES), -jnp.inf, jnp.float32).at[0, :n_e].set(b_router.astype(jnp.float32))
    return pl.pallas_call(
        _router_kernel,
        grid=(t // tm,),
        in_specs=[pl.BlockSpec((tm, d), lambda i: (i, 0)),
                  pl.BlockSpec((1, d), lambda i: (0, 0)),
                  pl.BlockSpec((d, LANES), lambda i: (0, 0)),
                  pl.BlockSpec((1, LANES), lambda i: (0, 0))],
        out_specs=[pl.BlockSpec((tm, LANES), lambda i: (i, 0))] * 2
        + [pl.BlockSpec((tm, d // LANES, LANES), lambda i: (i, 0, 0))],
        out_shape=[jax.ShapeDtypeStruct((t, LANES), jnp.int32),
                   jax.ShapeDtypeStruct((t, LANES), jnp.float32),
                   jax.ShapeDtypeStruct((t, d // LANES, LANES), jnp.bfloat16)],
        compiler_params=_params(("parallel",)),
        name="router",
    )(x, g, w, b)


def _gather_kernel(idx_ref, src_ref, dst_ref, sem, *, rows):
    def row_copy(r):
        return pltpu.make_async_copy(src_ref.at[idx_ref[r]], dst_ref.at[r], sem)

    def issue(r, carry):
        row_copy(r).start()
        return carry

    lax.fori_loop(0, rows, issue, 0)

    def drain(r, carry):
        row_copy(r).wait()
        return carry

    lax.fori_loop(0, rows, drain, 0)


def _gather_rows(src, idx, *, rows):
    tile = src.shape[1:]
    r = idx.shape[0]
    assert r % rows == 0
    return pl.pallas_call(
        functools.partial(_gather_kernel, rows=rows),
        grid=(r // rows,),
        in_specs=[pl.BlockSpec((rows,), lambda i: (i,), memory_space=pltpu.SMEM),
                  pl.BlockSpec(memory_space=pl.ANY)],
        out_specs=pl.BlockSpec((rows,) + tile, lambda i: (i, 0, 0)),
        out_shape=jax.ShapeDtypeStruct((r,) + tile, src.dtype),
        scratch_shapes=[pltpu.SemaphoreType.DMA(())],
        compiler_params=_params(("arbitrary",)),
        name="gather_rows",
    )(idx, src)


def _combine_kernel(d0_ref, d1_ref, x_ref, gate_ref, y_hbm, *rest, rows, n_head):
    out_refs, (buf0, buf1, sem) = rest[:-3], rest[-3:]

    def copies(r):
        return (pltpu.make_async_copy(y_hbm.at[d0_ref[r]], buf0.at[r], sem.at[0]),
                pltpu.make_async_copy(y_hbm.at[d1_ref[r]], buf1.at[r], sem.at[1]))

    def issue(r, carry):
        c0, c1 = copies(r)
        c0.start()
        c1.start()
        return carry

    lax.fori_loop(0, rows, issue, 0)

    def drain(r, carry):
        c0, c1 = copies(r)
        c0.wait()
        c1.wait()
        return carry

    lax.fori_loop(0, rows, drain, 0)
    gates = gate_ref[...]
    shape = x_ref.shape
    y = gates[:, 0:1] * buf0[...].reshape(shape) + gates[:, 1:2] * buf1[...].reshape(shape)
    out = x_ref[...] + y
    if len(out_refs) == 1:
        out_refs[0][...] = out
    else:
        @pl.when(pl.program_id(0) < n_head)
        def _():
            out_refs[0][...] = out

        @pl.when(pl.program_id(0) >= n_head)
        def _():
            out_refs[1][...] = out


def _combine(x, gates, y_rows, d0, d1, split_at=None):
    t, d = x.shape
    rows = COMBINE_ROWS
    assert t % rows == 0
    idx_spec = pl.BlockSpec((rows,), lambda i: (i,), memory_space=pltpu.SMEM)
    if split_at is None:
        n_head = 0
        out_specs = pl.BlockSpec((rows, d), lambda i: (i, 0))
        out_shape = jax.ShapeDtypeStruct((t, d), jnp.float32)
    else:
        assert split_at % rows == 0 and 0 < split_at < t
        n_head = split_at // rows
        out_specs = [pl.BlockSpec((rows, d), lambda i: (jnp.minimum(i, n_head - 1), 0)),
                     pl.BlockSpec((rows, d), lambda i: (jnp.maximum(i - n_head, 0), 0))]
        out_shape = [jax.ShapeDtypeStruct((split_at, d), jnp.float32),
                     jax.ShapeDtypeStruct((t - split_at, d), jnp.float32)]
    return pl.pallas_call(
        functools.partial(_combine_kernel, rows=rows, n_head=n_head),
        grid=(t // rows,),
        in_specs=[idx_spec, idx_spec,
                  pl.BlockSpec((rows, d), lambda i: (i, 0)),
                  pl.BlockSpec((rows, LANES), lambda i: (i, 0)),
                  pl.BlockSpec(memory_space=pl.ANY)],
        out_specs=out_specs,
        out_shape=out_shape,
        scratch_shapes=[pltpu.VMEM((rows,) + y_rows.shape[1:], jnp.float32),
                        pltpu.VMEM((rows,) + y_rows.shape[1:], jnp.float32),
                        pltpu.SemaphoreType.DMA((2,))],
        compiler_params=_params(("arbitrary",)),
        name="combine",
    )(d0, d1, x, gates, y_rows)


def _moe(x, g, w_router, b_router, wg, wu, wd, split_at=None):
    t, d = x.shape
    n_e = wg.shape[0]
    n_assign = t * TOP_K
    top_idx, gates, h_rows = _route(x, g, w_router, b_router)
    flat_e = top_idx[:, :TOP_K].reshape(-1)
    onehot = (flat_e[:, None] == jnp.arange(n_e, dtype=jnp.int32)[None, :]).astype(jnp.int32)
    rank = jnp.sum((jnp.cumsum(onehot, axis=0) - onehot) * onehot, axis=1)
    counts = jnp.sum(onehot, axis=0)
    padded = (counts + MOE_ROWS - 1) // MOE_ROWS * MOE_ROWS
    ends = jnp.cumsum(padded)
    dest = jnp.sum((ends - padded)[None, :] * onehot, axis=1) + rank
    n_blocks = (n_assign + n_e * (MOE_ROWS - 1) + MOE_ROWS - 1) // MOE_ROWS
    rows = n_blocks * MOE_ROWS
    flat_tok = jnp.arange(n_assign, dtype=jnp.int32) // TOP_K
    row_tok = jnp.zeros((rows,), jnp.int32).at[dest].set(flat_tok)
    block_start = jnp.arange(n_blocks, dtype=jnp.int32) * MOE_ROWS
    block_expert = jnp.minimum(
        jnp.sum((ends[None, :] <= block_start[:, None]).astype(jnp.int32), axis=1), n_e - 1)
    n_used = (ends[-1:] // MOE_ROWS).astype(jnp.int32)
    x_rows = _gather_rows(h_rows, row_tok, rows=MOE_ROWS)
    y_rows = _swiglu_blocks(x_rows, g, wg, wu, wd, block_expert, n_used, tm=MOE_ROWS,
                            tf=MOE_COL_TILE, residual=False)
    dest2 = dest.reshape(t, TOP_K).astype(jnp.int32)
    return _combine(x, gates, y_rows, dest2[:, 0], dest2[:, 1], split_at)


def _rope_tables(pos):
    half = HEAD_DIM // 2
    inv_freq = 1.0 / (ROPE_THETA ** (jnp.arange(0, HEAD_DIM, 2, dtype=jnp.float32) / HEAD_DIM))
    ang = pos.astype(jnp.float32)[:, None] * inv_freq[None, :]
    cos, sin = jnp.cos(ang), jnp.sin(ang)
    reps = LANES // half
    signs = jnp.tile(jnp.concatenate([-jnp.ones((half,)), jnp.ones((half,))]), HEADS_PER_VREG)
    return jnp.tile(cos, (1, reps)), jnp.tile(sin, (1, reps)) * signs[None, :].astype(jnp.float32)


def _rel_bias(table, lq, n_before):
    rel = jnp.arange(lq)[:, None] - (jnp.arange(n_before + lq) - n_before)[None, :]
    idx = jnp.clip(rel, REL_MIN, REL_MAX) - REL_MIN
    return table.astype(jnp.float32)[:, idx]


def _pair_major(a):
    h = a.shape[0]
    a = a.reshape((h // GROUP, PAIRS, HEADS_PER_VREG) + a.shape[1:])
    return jnp.swapaxes(a, 1, 2)


def _prompt_bias(table, n_prev):
    pad = n_prev * CHUNK
    b = _pair_major(_rel_bias(table, CHUNK, pad))
    kvh = b.shape[0]
    b = jnp.transpose(b, (0, 1, 4, 2, 3)).reshape(kvh * HEADS_PER_VREG, pad + CHUNK, LANES)
    return jnp.pad(b, ((0, 0), (0, pad), (0, 0)))


def _prompt_sink(sink):
    s = _pair_major(sink.astype(jnp.float32))
    s = jnp.repeat(s[..., None], CHUNK, axis=-1)
    return s.reshape(-1, 1, LANES)


def _sample_bias(table, lq, n_cached):
    b = _rel_bias(table, lq, n_cached)
    return b.reshape(b.shape[0] // GROUP, GROUP * lq, n_cached + lq)


def _sample_sink(sink, lq):
    h = sink.shape[0]
    s = jnp.repeat(sink.astype(jnp.float32).reshape(h // GROUP, GROUP, 1), lq, axis=2)
    return s.reshape(h // GROUP, GROUP * lq, 1)


def _head_gain(g):
    return jnp.tile(g.astype(jnp.float32), HEADS_PER_VREG)[None, :]


def kernel(x_prompt, x_sample, cache_k_a, cache_v_a, cache_k_b, cache_v_b, g_attn, g_ffn, w_qkv_a, g_q_a, g_k_a, sink_a, w_o_a, g_kv_b, w_kv_b, g_k_b, w_q_b, g_q_b, rel_bias_b, w_o_b, w_gate_ffn, w_up_ffn, w_down_ffn, w_router, b_router, w_gate_moe, w_up_moe, w_down_moe):
    bf16 = jnp.bfloat16
    nb, seq, d = x_prompt.shape
    ns, seq_s, _ = x_sample.shape
    tp, ts = nb * seq, ns * seq_s
    t = tp + ts
    depth = g_attn.shape[0]
    n_a = w_qkv_a.shape[0]
    n_kv = cache_k_b.shape[2]
    kvw = n_kv * HEAD_DIM
    keep_a, keep_b = min(WINDOW_A, seq), min(BAND_B, seq)
    assert t % ROW_TILE == 0

    x = (x_prompt.reshape(tp, d), x_sample.reshape(ts, d))
    pos = jnp.concatenate([jnp.tile(jnp.arange(seq), nb),
                           jnp.tile(PAST_LEN + jnp.arange(seq_s), ns)])
    cos, sin = _rope_tables(pos)
    row = lambda v: v.astype(jnp.float32)[None, :]

    def attend(q, kv_pad, k_f32, v_f32, cache_k, cache_v, n_prev, table, sink):
        bias_p = None if table is None else _prompt_bias(table, n_prev)
        sink_p = None if sink is None else _prompt_sink(sink)
        o_p = _attend_prompt(q, *kv_pad, bias_p, sink_p, nb=nb, seq=seq, n_prev=n_prev)
        n_cached = cache_k.shape[1]
        ks = jnp.concatenate([cache_k.reshape(ns, n_cached, kvw),
                              k_f32[tp:].reshape(ns, seq_s, kvw)], axis=1).astype(bf16)
        vs = jnp.concatenate([cache_v.reshape(ns, n_cached, kvw),
                              v_f32[tp:].reshape(ns, seq_s, kvw)], axis=1).astype(bf16)
        bias_s = None if table is None else _sample_bias(table, seq_s, n_cached)
        sink_s = None if sink is None else _sample_sink(sink, seq_s)
        return o_p, _attend_sample(q, ks, vs, bias_s, sink_s, row0=tp, lq=seq_s)

    def tail(c, n_keep):
        rows = jnp.stack([c[(b + 1) * seq - n_keep:(b + 1) * seq] for b in range(nb)])
        return rows.reshape(nb, n_keep, n_kv, HEAD_DIM)

    def rolled(cache, c):
        new = c[tp:].reshape(ns, seq_s, n_kv, HEAD_DIM)
        return jnp.concatenate([cache, new], axis=1)[:, seq_s:]

    ka_p, va_p, ka_s, va_s = [], [], [], []
    kvb_pad = kb_f = vb_f = None
    for layer in range(depth):
        g_in = row(g_attn[layer])
        if layer < n_a:
            i = layer
            q, kl, kh, vl, vh, k_f32, v_f32 = _project(
                x, g_in, w_qkv_a[i].astype(bf16), _head_gain(g_q_a[i]), _head_gain(g_k_a[i]),
                cos, sin, nq=(w_qkv_a.shape[2] - 2 * kvw) // COL_TILE, has_kv=True, rope=True)
            o = attend(q, (kl, kh, vl, vh), k_f32, v_f32, cache_k_a[i], cache_v_a[i],
                       PREV_CHUNKS_A, None, sink_a[i])
            x = _out_proj(o, w_o_a[i].astype(bf16), x)
            ka_p.append(tail(k_f32, keep_a))
            va_p.append(tail(v_f32, keep_a))
            ka_s.append(rolled(cache_k_a[i], k_f32))
            va_s.append(rolled(cache_v_a[i], v_f32))
        else:
            jb = layer - n_a
            (q,) = _project(x, g_in, w_q_b[jb].astype(bf16), _head_gain(g_q_b[jb]), None,
                            None, None, nq=w_q_b.shape[2] // COL_TILE, has_kv=False, rope=False)
            o = attend(q, kvb_pad, kb_f, vb_f, cache_k_b, cache_v_b, PREV_CHUNKS_B,
                       rel_bias_b[jb], None)
            x = _out_proj(o, w_o_b[jb].astype(bf16), x)

        g_mid = row(g_ffn[layer])
        m = layer // 2
        if layer % 2 == 0:
            n_tiles = t // ROW_TILE
            x = _swiglu_blocks(x, g_mid, w_gate_ffn[m:m + 1].astype(bf16),
                               w_up_ffn[m:m + 1].astype(bf16), w_down_ffn[m:m + 1].astype(bf16),
                               jnp.zeros((n_tiles,), jnp.int32),
                               jnp.full((1,), n_tiles, jnp.int32), tm=ROW_TILE, tf=COL_TILE,
                               residual=True)
        else:
            x = _moe(x, g_mid, w_router[m], b_router[m], w_gate_moe[m].astype(bf16),
                     w_up_moe[m].astype(bf16), w_down_moe[m].astype(bf16),
                     split_at=tp if layer == depth - 1 else None)

        if layer == n_a - 1:
            *kvb_pad, kb_f, vb_f = _project(x, row(g_kv_b), w_kv_b.astype(bf16), None,
                                            _head_gain(g_k_b), None, None,
                                            nq=0, has_kv=True, rope=False)

    if not isinstance(x, tuple):
        x = (x[:tp], x[tp:])
    y_prompt = x[0].reshape(nb, seq, d)
    y_sample = x[1].reshape(ns, seq_s, d)
    return (y_prompt, y_sample, jnp.stack(ka_p), jnp.stack(va_p), tail(kb_f, keep_b),
            tail(vb_f, keep_b), jnp.stack(ka_s), jnp.stack(va_s),
            rolled(cache_k_b, kb_f), rolled(cache_v_b, vb_f))
```

```python
import functools

import jax
import jax.numpy as jnp
from jax import lax
from jax.experimental import pallas as pl
from jax.experimental.pallas import tpu as pltpu

CHUNK = 64
HEAD_DIM = 64
GROUP = 4
WINDOW_A = 128
PREV_CHUNKS_A = WINDOW_A // CHUNK
PREV_CHUNKS_B = 8
BAND_B = PREV_CHUNKS_B * CHUNK
REL_MIN = -(CHUNK - 1)
REL_MAX = 256
ROPE_THETA = 10000.0
TOP_K = 2
RMS_EPS = 1e-6
NEG_INF = -1e30
PAST_LEN = 1024

LANES = 128
HEADS_PER_VREG = LANES // HEAD_DIM
KV_GROUP_WIDTH = GROUP * HEAD_DIM
PAIRS = GROUP // HEADS_PER_VREG

ROW_TILE = 512
COL_TILE = 512
MOE_ROWS = 512
MOE_COL_TILE = 1024
COMBINE_ROWS = 256
ATTN_UNROLL = 4
DMA_THREADS = 2
VMEM_LIMIT = 56 * 1024 * 1024


def _params(semantics):
    return pltpu.CompilerParams(dimension_semantics=semantics, vmem_limit_bytes=VMEM_LIMIT)


def _rms_rows(x, g):
    ms = jnp.sum(x * x, axis=-1, keepdims=True) * (1.0 / x.shape[-1])
    return (x * lax.rsqrt(ms + RMS_EPS)) * g


def _head_norm(y, gain, cos, sin):
    lane = lax.broadcasted_iota(jnp.int32, y.shape, 1)
    low = lane < HEAD_DIM
    ss = y * y
    s_lo = jnp.sum(jnp.where(low, ss, 0.0), axis=-1, keepdims=True)
    s_hi = jnp.sum(jnp.where(low, 0.0, ss), axis=-1, keepdims=True)
    ms = jnp.where(low, s_lo, s_hi) * (1.0 / HEAD_DIM)
    yn = (y * lax.rsqrt(ms + RMS_EPS)) * gain
    if cos is None:
        return yn
    half = HEAD_DIM // 2
    fwd = pltpu.roll(yn, half, 1)
    bwd = pltpu.roll(yn, LANES - half, 1)
    rot = jnp.where((lane % HEAD_DIM) < half, bwd, fwd)
    return yn * cos + rot * sin


def _store_half_padded(lo_ref, hi_ref, c, z):
    lane = lax.broadcasted_iota(jnp.int32, z.shape, 1)
    low = lane < HEAD_DIM
    zr = pltpu.roll(z, HEAD_DIM, 1)
    dt = lo_ref.dtype
    a, b = 2 * c * LANES, (2 * c + 1) * LANES
    lo_ref[:, a:a + LANES] = jnp.where(low, z, 0.0).astype(dt)
    lo_ref[:, b:b + LANES] = jnp.where(low, zr, 0.0).astype(dt)
    hi_ref[:, a:a + LANES] = jnp.where(low, 0.0, zr).astype(dt)
    hi_ref[:, b:b + LANES] = jnp.where(low, 0.0, z).astype(dt)


def _row_specs(x, tm):
    def spec(cols, fn):
        return pl.BlockSpec((tm, cols), lambda i: (fn(i), 0))

    if not isinstance(x, tuple):
        return [spec(x.shape[1], lambda i: i)], 0
    head, tail = x
    assert head.shape[0] % tm == 0 and tail.shape[0] % tm == 0 and head.shape[1] == tail.shape[1]
    n_head = head.shape[0] // tm
    return [spec(head.shape[1], lambda i: jnp.minimum(i, n_head - 1)),
            spec(head.shape[1], lambda i: jnp.maximum(i - n_head, 0))], n_head


def _read_rows(x_refs, n_head):
    if len(x_refs) == 1:
        return x_refs[0][...]
    return jnp.where(pl.program_id(0) < n_head, x_refs[0][...], x_refs[1][...])


def _n_rows(x):
    return x[0].shape[0] + x[1].shape[0] if isinstance(x, tuple) else x.shape[0]


def _as_list(x):
    return list(x) if isinstance(x, tuple) else [x]


def _proj_kernel(*refs, nq, has_kv, rope, n_x, n_head):
    it = iter(refs)
    x_refs = [next(it) for _ in range(n_x)]
    g_ref, w_ref = next(it), next(it)
    gq_ref = next(it) if nq else None
    gk_ref = next(it) if has_kv else None
    cos_ref, sin_ref = (next(it), next(it)) if rope else (None, None)
    q_ref = next(it) if nq else None
    if has_kv:
        kl_ref, kh_ref, vl_ref, vh_ref, kf_ref, vf_ref = (next(it) for _ in range(6))
    h_scr = next(it)
    h_scr[...] = _rms_rows(_read_rows(x_refs, n_head), g_ref[...]).astype(h_scr.dtype)
    cos = cos_ref[...] if rope else None
    sin = sin_ref[...] if rope else None
    groups = COL_TILE // LANES
    for j in range(nq + (2 if has_kv else 0)):
        y = jnp.dot(h_scr[...], w_ref[j], preferred_element_type=jnp.float32)
        for c in range(groups):
            yc = y[:, c * LANES:(c + 1) * LANES]
            if j < nq:
                z = _head_norm(yc, gq_ref[...], cos, sin) * (HEAD_DIM ** -0.5)
                q_ref[:, (j * groups + c) * LANES:(j * groups + c + 1) * LANES] = z.astype(q_ref.dtype)
            elif j == nq:
                z = _head_norm(yc, gk_ref[...], cos, sin)
                kf_ref[:, c * LANES:(c + 1) * LANES] = z
                _store_half_padded(kl_ref, kh_ref, c, z)
            else:
                vf_ref[:, c * LANES:(c + 1) * LANES] = yc
                _store_half_padded(vl_ref, vh_ref, c, yc)


def _project(x, g, w, gq, gk, cos, sin, *, nq, has_kv, rope):
    t = _n_rows(x)
    d = w.shape[0]
    tm = ROW_TILE
    n_col = nq + (2 if has_kv else 0)
    assert w.shape == (d, n_col * COL_TILE) and t % tm == 0
    w = jnp.swapaxes(w.reshape(d, n_col, COL_TILE), 0, 1)
    row = lambda i: (i, 0)
    x_specs, n_head = _row_specs(x, tm)
    in_specs = x_specs + [pl.BlockSpec((1, d), lambda i: (0, 0)),
                          pl.BlockSpec((n_col, d, COL_TILE), lambda i: (0, 0, 0))]
    args = _as_list(x) + [g, w]
    vec = pl.BlockSpec((1, LANES), lambda i: (0, 0))
    if nq:
        in_specs.append(vec)
        args.append(gq)
    if has_kv:
        in_specs.append(vec)
        args.append(gk)
    if rope:
        in_specs += [pl.BlockSpec((tm, LANES), row)] * 2
        args += [cos, sin]
    out_shape, out_specs = [], []
    if nq:
        out_shape.append(jax.ShapeDtypeStruct((t, nq * COL_TILE), jnp.bfloat16))
        out_specs.append(pl.BlockSpec((tm, nq * COL_TILE), row))
    if has_kv:
        for _ in range(4):
            out_shape.append(jax.ShapeDtypeStruct((t, HEADS_PER_VREG * COL_TILE), jnp.bfloat16))
            out_specs.append(pl.BlockSpec((tm, HEADS_PER_VREG * COL_TILE), row))
        for _ in range(2):
            out_shape.append(jax.ShapeDtypeStruct((t, COL_TILE), jnp.float32))
            out_specs.append(pl.BlockSpec((tm, COL_TILE), row))
    return pl.pallas_call(
        functools.partial(_proj_kernel, nq=nq, has_kv=has_kv, rope=rope,
                          n_x=len(x_specs), n_head=n_head),
        grid=(t // tm,),
        in_specs=in_specs,
        out_specs=out_specs,
        out_shape=out_shape,
        scratch_shapes=[pltpu.VMEM((tm, d), jnp.bfloat16)],
        compiler_params=_params(("parallel",)),
        name="proj",
    )(*args)


def _attn_prompt_kernel(*refs, seq, band, n_prev, unroll, has_bias, has_sink):
    it = iter(refs)
    q_ref, kl_ref, kh_ref, vl_ref, vh_ref = (next(it) for _ in range(5))
    bias_ref = next(it) if has_bias else None
    sink_ref = next(it) if has_sink else None
    o_ref = next(it)
    s_scr = (next(it), next(it))
    p_scr = (next(it), next(it))
    n_scr = (next(it), next(it))
    row = lax.broadcasted_iota(jnp.int32, (band, LANES), 0)
    nt = (((1,), (1,)), ((), ()))
    tn = (((0,), (0,)), ((), ()))
    n_groups = seq // (CHUNK * unroll)
    n_masked = min(-(-n_prev // unroll), n_groups)

    def key_start(c):
        return pl.multiple_of(jnp.maximum(c - n_prev, 0) * CHUNK, CHUNK)

    def scores(gi, slot):
        for u in range(unroll):
            c = gi * unroll + u
            r0 = pl.multiple_of(c * CHUNK, CHUNK)
            k0 = key_start(c)
            qs = jnp.concatenate([q_ref[pl.ds(r0, CHUNK), p * LANES:(p + 1) * LANES]
                                  for p in range(PAIRS)], axis=0)
            for half, k_ref in enumerate((kl_ref, kh_ref)):
                s_scr[slot][u, half] = lax.dot_general(k_ref[pl.ds(k0, band), :], qs, nt,
                                                        preferred_element_type=jnp.float32)

    def softmax(gi, slot, masked):
        for u in range(unroll):
            c = gi * unroll + u
            shift = pl.multiple_of(jnp.maximum(n_prev - c, 0) * CHUNK, CHUNK) if masked else 0
            inv = []
            for half in range(HEADS_PER_VREG):
                s = s_scr[slot][u, half]
                if has_bias:
                    s = s + bias_ref[half, pl.ds(shift, band), :]
                if masked:
                    s = jnp.where(row < band - shift, s, NEG_INF)
                m = jnp.max(s, axis=0, keepdims=True)
                if has_sink:
                    m = jnp.maximum(m, sink_ref[half])
                e = jnp.exp(s - m)
                l = jnp.sum(e, axis=0, keepdims=True)
                if has_sink:
                    l = l + jnp.exp(sink_ref[half] - m)
                p_scr[slot][u, half] = e.astype(p_scr[slot].dtype)
                inv.append(jnp.broadcast_to(1.0 / l, (LANES, LANES)))
            n_scr[slot][u] = jnp.where(row[:LANES] < HEAD_DIM, inv[0], inv[1])

    def outputs(gi, slot):
        for u in range(unroll):
            c = gi * unroll + u
            r0 = pl.multiple_of(c * CHUNK, CHUNK)
            k0 = key_start(c)
            o = (lax.dot_general(p_scr[slot][u, 0], vl_ref[pl.ds(k0, band), :], tn,
                                 preferred_element_type=jnp.float32)
                 + lax.dot_general(p_scr[slot][u, 1], vh_ref[pl.ds(k0, band), :], tn,
                                   preferred_element_type=jnp.float32))
            o = o * n_scr[slot][u].T
            for p in range(PAIRS):
                o_ref[pl.ds(r0, CHUNK), p * LANES:(p + 1) * LANES] = (
                    o[p * CHUNK:(p + 1) * CHUNK].astype(o_ref.dtype))

    def stage(g, parity, masked):
        static = isinstance(g, int)
        if not static or g < n_groups:
            scores(g, parity)
        if not static or 0 <= g - 1 < n_groups:
            softmax(g - 1, 1 - parity, masked)
        if not static or 0 <= g - 2 < n_groups:
            outputs(g - 2, parity)

    first_loop = max(n_masked + 1, 2)
    first_loop += first_loop % 2
    n_pairs = max(n_groups - first_loop, 0) // 2
    for g in range(min(first_loop, n_groups)):
        stage(g, g % 2, masked=True)
    if n_pairs:
        def body(k, carry):
            g = first_loop + 2 * k
            stage(g, 0, masked=False)
            stage(g + 1, 1, masked=False)
            return carry
        lax.fori_loop(0, n_pairs, body, 0)
    for g in range(min(first_loop, n_groups) + 2 * n_pairs, n_groups + 2):
        stage(g, g % 2, masked=(g - 1) < n_masked)


def _attend_prompt(q, kl, kh, vl, vh, bias, sink, *, nb, seq, n_prev):
    qw = q.shape[1]
    n_kv = kl.shape[1] // LANES
    band = (n_prev + 1) * CHUNK
    unroll = ATTN_UNROLL
    assert seq % (CHUNK * unroll) == 0 and seq >= band and seq >= 2 * CHUNK * unroll
    kv_spec = pl.BlockSpec((seq, LANES), lambda b, j: (b, j))
    in_specs = [pl.BlockSpec((seq, KV_GROUP_WIDTH), lambda b, j: (b, j))] + [kv_spec] * 4
    args = [q, kl, kh, vl, vh]
    if bias is not None:
        in_specs.append(pl.BlockSpec((HEADS_PER_VREG,) + bias.shape[1:], lambda b, j: (j, 0, 0)))
        args.append(bias)
    if sink is not None:
        in_specs.append(pl.BlockSpec((HEADS_PER_VREG, 1, LANES), lambda b, j: (j, 0, 0)))
        args.append(sink)
    return pl.pallas_call(
        functools.partial(_attn_prompt_kernel, seq=seq, band=band, n_prev=n_prev, unroll=unroll,
                          has_bias=bias is not None, has_sink=sink is not None),
        grid=(nb, n_kv),
        in_specs=in_specs,
        out_specs=pl.BlockSpec((seq, KV_GROUP_WIDTH), lambda b, j: (b, j)),
        out_shape=jax.ShapeDtypeStruct((nb * seq, qw), jnp.bfloat16),
        scratch_shapes=[pltpu.VMEM((unroll, HEADS_PER_VREG, band, LANES), jnp.float32)] * 2
        + [pltpu.VMEM((unroll, HEADS_PER_VREG, band, LANES), jnp.bfloat16)] * 2
        + [pltpu.VMEM((unroll, LANES, LANES), jnp.float32)] * 2,
        compiler_params=_params(("parallel", "arbitrary")),
        name="attn_prompt",
    )(*args)


def _attn_sample_kernel(*refs, lq, has_bias, has_sink, n_kv):
    it = iter(refs)
    q_ref, k_ref, v_ref = next(it), next(it), next(it)
    bias_ref = next(it) if has_bias else None
    sink_ref = next(it) if has_sink else None
    o_ref = next(it)
    for j in range(n_kv):
        qj = q_ref[:, j * KV_GROUP_WIDTH:(j + 1) * KV_GROUP_WIDTH]
        qcat = jnp.concatenate(
            [qj[:, g * HEAD_DIM:(g + 1) * HEAD_DIM] for g in range(GROUP)], axis=0)
        kj = k_ref[0, :, j * HEAD_DIM:(j + 1) * HEAD_DIM]
        vj = v_ref[0, :, j * HEAD_DIM:(j + 1) * HEAD_DIM]
        s = lax.dot_general(qcat, kj, (((1,), (1,)), ((), ())),
                            preferred_element_type=jnp.float32)
        if has_bias:
            s = s + bias_ref[j]
        m = jnp.max(s, axis=-1, keepdims=True)
        if has_sink:
            sink = sink_ref[j]
            m = jnp.maximum(m, sink)
        e = jnp.exp(s - m)
        l = jnp.sum(e, axis=-1, keepdims=True)
        if has_sink:
            l = l + jnp.exp(sink - m)
        p = (e * (1.0 / l)).astype(vj.dtype)
        o = jnp.dot(p, vj, preferred_element_type=jnp.float32)
        ocat = jnp.concatenate([o[g * lq:(g + 1) * lq] for g in range(GROUP)], axis=1)
        o_ref[:, j * KV_GROUP_WIDTH:(j + 1) * KV_GROUP_WIDTH] = ocat.astype(o_ref.dtype)


def _attend_sample(q, k, v, bias, sink, *, row0, lq):
    ns, lk, kw = k.shape
    qw = q.shape[1]
    assert row0 % lq == 0
    blk0 = row0 // lq
    in_specs = [pl.BlockSpec((lq, qw), lambda i: (blk0 + i, 0)),
                pl.BlockSpec((1, lk, kw), lambda i: (i, 0, 0)),
                pl.BlockSpec((1, lk, kw), lambda i: (i, 0, 0))]
    args = [q, k, v]
    if bias is not None:
        in_specs.append(pl.BlockSpec(bias.shape, lambda i: (0, 0, 0)))
        args.append(bias)
    if sink is not None:
        in_specs.append(pl.BlockSpec(sink.shape, lambda i: (0, 0, 0)))
        args.append(sink)
    return pl.pallas_call(
        functools.partial(_attn_sample_kernel, lq=lq, has_bias=bias is not None,
                          has_sink=sink is not None, n_kv=kw // HEAD_DIM),
        grid=(ns,),
        in_specs=in_specs,
        out_specs=pl.BlockSpec((lq, qw), lambda i: (i, 0)),
        out_shape=jax.ShapeDtypeStruct((ns * lq, qw), jnp.bfloat16),
        compiler_params=_params(("arbitrary",)),
        name="attn_sample",
    )(*args)


def _out_proj_kernel(*refs, n_o, o_head, n_x, x_head):
    o_refs, w_ref = refs[:n_o], refs[n_o]
    x_refs, y_ref = refs[n_o + 1:n_o + 1 + n_x], refs[n_o + 1 + n_x]
    y_ref[...] = _read_rows(x_refs, x_head) + jnp.dot(
        _read_rows(o_refs, o_head), w_ref[...], preferred_element_type=jnp.float32)


def _out_proj(o, w, x):
    t = _n_rows(x)
    d = w.shape[1]
    tm = ROW_TILE
    o_specs, o_head = _row_specs(o, tm)
    x_specs, x_head = _row_specs(x, tm)
    return pl.pallas_call(
        functools.partial(_out_proj_kernel, n_o=len(o_specs), o_head=o_head,
                          n_x=len(x_specs), x_head=x_head),
        grid=(t // tm,),
        in_specs=o_specs + [pl.BlockSpec(w.shape, lambda i: (0, 0))] + x_specs,
        out_specs=pl.BlockSpec((tm, d), lambda i: (i, 0)),
        out_shape=jax.ShapeDtypeStruct((t, d), jnp.float32),
        compiler_params=_params(("parallel",)),
        name="out_proj",
    )(*_as_list(o), w, *_as_list(x))


def _ffn_kernel(e_ref, n_ref, x_ref, g_ref, wg_ref, wu_ref, wd_ref, y_ref, h_scr, acc_scr, *,
                residual, token_major):
    f = pl.program_id(1)
    last = pl.num_programs(1) - 1

    @pl.when(pl.program_id(0) < n_ref[0])
    def _():
        @pl.when(f == 0)
        def _():
            if token_major:
                h_scr[...] = x_ref[...].reshape(h_scr.shape)
            else:
                h_scr[...] = _rms_rows(x_ref[...], g_ref[...]).astype(h_scr.dtype)

        h = h_scr[...]
        gate = jnp.dot(h, wg_ref[...], preferred_element_type=jnp.float32)
        up = jnp.dot(h, wu_ref[...], preferred_element_type=jnp.float32)
        act = (gate * jax.nn.sigmoid(gate)) * up
        part = jnp.dot(act.astype(jnp.bfloat16), wd_ref[...],
                       preferred_element_type=jnp.float32)

        @pl.when(f == 0)
        def _():
            acc_scr[...] = part

        @pl.when(f > 0)
        def _():
            acc_scr[...] += part

        @pl.when(f == last)
        def _():
            if token_major:
                y_ref[...] = acc_scr[...].reshape(y_ref.shape)
            else:
                y_ref[...] = (x_ref[...] + acc_scr[...]) if residual else acc_scr[...]

    @pl.when((pl.program_id(0) >= n_ref[0]) & (f == last))
    def _():
        y_ref[...] = jnp.zeros_like(y_ref)


def _swiglu_blocks(x, g, wg, wu, wd, layer, block_expert, n_used, *, tm, tf, residual):
    token_major = x.ndim == 3
    assert not (token_major and residual)
    r, d = x.shape[0], wg.shape[2]
    row_block = (tm,) + x.shape[1:]
    zeros = (0,) * (x.ndim - 1)
    ff = wg.shape[3]
    assert r % tm == 0 and ff % tf == 0
    n_f = ff // tf

    def blk(i, n):
        return jnp.minimum(i, n[0] - 1)

    def col(i, f, n):
        return jnp.where(i < n[0], f, n_f - 1)

    grid_spec = pltpu.PrefetchScalarGridSpec(
        num_scalar_prefetch=2,
        grid=(r // tm, n_f),
        in_specs=[pl.BlockSpec(row_block, lambda i, f, e, n: (blk(i, n),) + zeros),
                  pl.BlockSpec((1, d), lambda i, f, e, n: (0, 0)),
                  pl.BlockSpec((None, None, d, tf),
                               lambda i, f, e, n: (layer, e[blk(i, n)], 0, col(i, f, n))),
                  pl.BlockSpec((None, None, d, tf),
                               lambda i, f, e, n: (layer, e[blk(i, n)], 0, col(i, f, n))),
                  pl.BlockSpec((None, None, tf, d),
                               lambda i, f, e, n: (layer, e[blk(i, n)], col(i, f, n), 0))],
        out_specs=pl.BlockSpec(row_block, lambda i, f, e, n: (i,) + zeros),
        scratch_shapes=[pltpu.VMEM((tm, d), jnp.bfloat16),
                        pltpu.VMEM((tm, d), jnp.float32)],
    )
    return pl.pallas_call(
        functools.partial(_ffn_kernel, residual=residual, token_major=token_major),
        grid_spec=grid_spec,
        out_shape=jax.ShapeDtypeStruct(x.shape, jnp.float32),
        compiler_params=_params(("arbitrary", "arbitrary")),
        name="swiglu",
    )(block_expert, n_used, x, g, wg, wu, wd)


def _router_kernel(x_ref, g_ref, w_ref, b_ref, idx_ref, gate_ref, h_ref):
    h = _rms_rows(x_ref[...], g_ref[...]).astype(jnp.bfloat16)
    h_ref[...] = h.reshape(h_ref.shape)
    logits = jnp.dot(h, w_ref[...], preferred_element_type=jnp.float32) + b_ref[...]
    lane = lax.broadcasted_iota(jnp.int32, logits.shape, 1)
    m1 = jnp.max(logits, axis=-1, keepdims=True)
    e1 = jnp.min(jnp.where(logits == m1, lane, LANES), axis=-1, keepdims=True)
    rest = jnp.where(lane == e1, -jnp.inf, logits)
    m2 = jnp.max(rest, axis=-1, keepdims=True)
    e2 = jnp.min(jnp.where(rest == m2, lane, LANES), axis=-1, keepdims=True)
    t = jnp.exp(m2 - m1)
    denom = 1.0 + t
    idx_ref[...] = jnp.where(lane == 0, e1, jnp.where(lane == 1, e2, 0))
    gate_ref[...] = jnp.where(lane == 0, 1.0 / denom, jnp.where(lane == 1, t / denom, 0.0))


def _route(x, g, w_router, b_router):
    t, d = x.shape
    tm = ROW_TILE
    n_e = w_router.shape[1]
    w = jnp.zeros((d, LANES), jnp.bfloat16).at[:, :n_e].set(w_router.astype(jnp.bfloat16))
    b = jnp.full((1, LANES), -jnp.inf, jnp.float32).at[0, :n_e].set(b_router.astype(jnp.float32))
    return pl.pallas_call(
        _router_kernel,
        grid=(t // tm,),
        in_specs=[pl.BlockSpec((tm, d), lambda i: (i, 0)),
                  pl.BlockSpec((1, d), lambda i: (0, 0)),
                  pl.BlockSpec((d, LANES), lambda i: (0, 0)),
                  pl.BlockSpec((1, LANES), lambda i: (0, 0))],
        out_specs=[pl.BlockSpec((tm, LANES), lambda i: (i, 0))] * 2
        + [pl.BlockSpec((tm, d // LANES, LANES), lambda i: (i, 0, 0))],
        out_shape=[jax.ShapeDtypeStruct((t, LANES), jnp.int32),
                   jax.ShapeDtypeStruct((t, LANES), jnp.float32),
                   jax.ShapeDtypeStruct((t, d // LANES, LANES), jnp.bfloat16)],
        compiler_params=_params(("parallel",)),
        name="router",
    )(x, g, w, b)


def _gather_kernel(idx_ref, src_ref, dst_ref, sem, *, rows):
    def row_copy(r):
        return pltpu.make_async_copy(src_ref.at[idx_ref[r]], dst_ref.at[r], sem)

    def issue(r2, carry):
        for k in range(DMA_THREADS):
            row_copy(r2 * DMA_THREADS + k).start(priority=k)
        return carry

    lax.fori_loop(0, rows // DMA_THREADS, issue, 0)

    def drain(r, carry):
        row_copy(r).wait()
        return carry

    lax.fori_loop(0, rows, drain, 0)


def _gather_rows(src, idx, *, rows):
    tile = src.shape[1:]
    r = idx.shape[0]
    assert r % rows == 0 and rows % DMA_THREADS == 0
    return pl.pallas_call(
        functools.partial(_gather_kernel, rows=rows),
        grid=(r // rows,),
        in_specs=[pl.BlockSpec((rows,), lambda i: (i,), memory_space=pltpu.SMEM),
                  pl.BlockSpec(memory_space=pl.ANY)],
        out_specs=pl.BlockSpec((rows,) + tile, lambda i: (i, 0, 0)),
        out_shape=jax.ShapeDtypeStruct((r,) + tile, src.dtype),
        scratch_shapes=[pltpu.SemaphoreType.DMA(())],
        compiler_params=_params(("arbitrary",)),
        name="gather_rows",
    )(idx, src)


def _combine_kernel(d0_ref, d1_ref, x_ref, gate_ref, y_hbm, *rest, rows, n_head):
    out_refs, (buf0, buf1, sem) = rest[:-3], rest[-3:]

    def copies(r):
        return (pltpu.make_async_copy(y_hbm.at[d0_ref[r]], buf0.at[r], sem.at[0]),
                pltpu.make_async_copy(y_hbm.at[d1_ref[r]], buf1.at[r], sem.at[1]))

    def issue(r, carry):
        c0, c1 = copies(r)
        c0.start(priority=0)
        c1.start(priority=1)
        return carry

    lax.fori_loop(0, rows, issue, 0)

    def drain(r, carry):
        c0, c1 = copies(r)
        c0.wait()
        c1.wait()
        return carry

    lax.fori_loop(0, rows, drain, 0)
    gates = gate_ref[...]
    shape = x_ref.shape
    y = gates[:, 0:1] * buf0[...].reshape(shape) + gates[:, 1:2] * buf1[...].reshape(shape)
    out = x_ref[...] + y
    if len(out_refs) == 1:
        out_refs[0][...] = out
    else:
        @pl.when(pl.program_id(0) < n_head)
        def _():
            out_refs[0][...] = out

        @pl.when(pl.program_id(0) >= n_head)
        def _():
            out_refs[1][...] = out


def _combine(x, gates, y_rows, d0, d1, split_at=None):
    t, d = x.shape
    rows = COMBINE_ROWS
    assert t % rows == 0
    idx_spec = pl.BlockSpec((rows,), lambda i: (i,), memory_space=pltpu.SMEM)
    if split_at is None:
        n_head = 0
        out_specs = pl.BlockSpec((rows, d), lambda i: (i, 0))
        out_shape = jax.ShapeDtypeStruct((t, d), jnp.float32)
    else:
        assert split_at % rows == 0 and 0 < split_at < t
        n_head = split_at // rows
        out_specs = [pl.BlockSpec((rows, d), lambda i: (jnp.minimum(i, n_head - 1), 0)),
                     pl.BlockSpec((rows, d), lambda i: (jnp.maximum(i - n_head, 0), 0))]
        out_shape = [jax.ShapeDtypeStruct((split_at, d), jnp.float32),
                     jax.ShapeDtypeStruct((t - split_at, d), jnp.float32)]
    return pl.pallas_call(
        functools.partial(_combine_kernel, rows=rows, n_head=n_head),
        grid=(t // rows,),
        in_specs=[idx_spec, idx_spec,
                  pl.BlockSpec((rows, d), lambda i: (i, 0)),
                  pl.BlockSpec((rows, LANES), lambda i: (i, 0)),
                  pl.BlockSpec(memory_space=pl.ANY)],
        out_specs=out_specs,
        out_shape=out_shape,
        scratch_shapes=[pltpu.VMEM((rows,) + y_rows.shape[1:], jnp.float32),
                        pltpu.VMEM((rows,) + y_rows.shape[1:], jnp.float32),
                        pltpu.SemaphoreType.DMA((2,))],
        compiler_params=_params(("arbitrary",)),
        name="combine",
    )(d0, d1, x, gates, y_rows)


def _moe(x, g, w_router, b_router, wg, wu, wd, layer, split_at=None):
    t, d = x.shape
    n_e = wg.shape[1]
    n_assign = t * TOP_K
    top_idx, gates, h_rows = _route(x, g, w_router, b_router)
    flat_e = top_idx[:, :TOP_K].reshape(-1)
    onehot = (flat_e[:, None] == jnp.arange(n_e, dtype=jnp.int32)[None, :]).astype(jnp.int32)
    rank = jnp.sum((jnp.cumsum(onehot, axis=0) - onehot) * onehot, axis=1)
    counts = jnp.sum(onehot, axis=0)
    padded = (counts + MOE_ROWS - 1) // MOE_ROWS * MOE_ROWS
    ends = jnp.cumsum(padded)
    dest = jnp.sum((ends - padded)[None, :] * onehot, axis=1) + rank
    n_blocks = (n_assign + n_e * (MOE_ROWS - 1) + MOE_ROWS - 1) // MOE_ROWS
    rows = n_blocks * MOE_ROWS
    flat_tok = jnp.arange(n_assign, dtype=jnp.int32) // TOP_K
    row_tok = jnp.zeros((rows,), jnp.int32).at[dest].set(flat_tok)
    block_start = jnp.arange(n_blocks, dtype=jnp.int32) * MOE_ROWS
    block_expert = jnp.minimum(
        jnp.sum((ends[None, :] <= block_start[:, None]).astype(jnp.int32), axis=1), n_e - 1)
    n_used = (ends[-1:] // MOE_ROWS).astype(jnp.int32)
    x_rows = _gather_rows(h_rows, row_tok, rows=MOE_ROWS)
    y_rows = _swiglu_blocks(x_rows, g, wg, wu, wd, layer, block_expert, n_used, tm=MOE_ROWS,
                            tf=MOE_COL_TILE, residual=False)
    dest2 = dest.reshape(t, TOP_K).astype(jnp.int32)
    return _combine(x, gates, y_rows, dest2[:, 0], dest2[:, 1], split_at)


def _rope_tables(pos):
    half = HEAD_DIM // 2
    inv_freq = 1.0 / (ROPE_THETA ** (jnp.arange(0, HEAD_DIM, 2, dtype=jnp.float32) / HEAD_DIM))
    ang = pos.astype(jnp.float32)[:, None] * inv_freq[None, :]
    cos, sin = jnp.cos(ang), jnp.sin(ang)
    reps = LANES // half
    signs = jnp.tile(jnp.concatenate([-jnp.ones((half,)), jnp.ones((half,))]), HEADS_PER_VREG)
    return jnp.tile(cos, (1, reps)), jnp.tile(sin, (1, reps)) * signs[None, :].astype(jnp.float32)


def _rel_bias(table, lq, n_before):
    n_keys = n_before + lq
    rel = n_before + (lq - 1) - jnp.arange(n_keys + lq - 1)
    diag = table.astype(jnp.float32)[:, jnp.clip(rel, REL_MIN, REL_MAX) - REL_MIN]
    return jnp.stack([diag[:, lq - 1 - q:lq - 1 - q + n_keys] for q in range(lq)], axis=1)


def _pair_major(a):
    h = a.shape[0]
    a = a.reshape((h // GROUP, PAIRS, HEADS_PER_VREG) + a.shape[1:])
    return jnp.swapaxes(a, 1, 2)


def _prompt_bias(table, n_prev):
    pad = n_prev * CHUNK
    b = _pair_major(_rel_bias(table, CHUNK, pad))
    kvh = b.shape[0]
    b = jnp.transpose(b, (0, 1, 4, 2, 3)).reshape(kvh * HEADS_PER_VREG, pad + CHUNK, LANES)
    return jnp.pad(b, ((0, 0), (0, pad), (0, 0)))


def _prompt_sink(sink):
    s = _pair_major(sink.astype(jnp.float32))
    s = jnp.repeat(s[..., None], CHUNK, axis=-1)
    return s.reshape(-1, 1, LANES)


def _sample_bias(table, lq, n_cached):
    b = _rel_bias(table, lq, n_cached)
    return b.reshape(b.shape[0] // GROUP, GROUP * lq, n_cached + lq)


def _sample_sink(sink, lq):
    h = sink.shape[0]
    s = jnp.repeat(sink.astype(jnp.float32).reshape(h // GROUP, GROUP, 1), lq, axis=2)
    return s.reshape(h // GROUP, GROUP * lq, 1)


def _head_gain(g):
    return jnp.tile(g.astype(jnp.float32), HEADS_PER_VREG)[None, :]


def kernel(x_prompt, x_sample, cache_k_a, cache_v_a, cache_k_b, cache_v_b, g_attn, g_ffn, w_qkv_a, g_q_a, g_k_a, sink_a, w_o_a, g_kv_b, w_kv_b, g_k_b, w_q_b, g_q_b, rel_bias_b, w_o_b, w_gate_ffn, w_up_ffn, w_down_ffn, w_router, b_router, w_gate_moe, w_up_moe, w_down_moe):
    bf16 = jnp.bfloat16
    nb, seq, d = x_prompt.shape
    ns, seq_s, _ = x_sample.shape
    tp, ts = nb * seq, ns * seq_s
    t = tp + ts
    depth = g_attn.shape[0]
    n_a = w_qkv_a.shape[0]
    n_kv = cache_k_b.shape[2]
    kvw = n_kv * HEAD_DIM
    keep_a, keep_b = min(WINDOW_A, seq), min(BAND_B, seq)
    assert t % ROW_TILE == 0

    x = (x_prompt.reshape(tp, d), x_sample.reshape(ts, d))
    pos = jnp.concatenate([jnp.tile(jnp.arange(seq), nb),
                           jnp.tile(PAST_LEN + jnp.arange(seq_s), ns)])
    cos, sin = _rope_tables(pos)
    row = lambda v: v.astype(jnp.float32)[None, :]

    def attend(q, kv_pad, k_f32, v_f32, cache_k, cache_v, n_prev, table, sink):
        bias_p = None if table is None else _prompt_bias(table, n_prev)
        sink_p = None if sink is None else _prompt_sink(sink)
        o_p = _attend_prompt(q, *kv_pad, bias_p, sink_p, nb=nb, seq=seq, n_prev=n_prev)
        n_cached = cache_k.shape[1]
        ks = jnp.concatenate([cache_k.reshape(ns, n_cached, kvw),
                              k_f32[tp:].reshape(ns, seq_s, kvw)], axis=1).astype(bf16)
        vs = jnp.concatenate([cache_v.reshape(ns, n_cached, kvw),
                              v_f32[tp:].reshape(ns, seq_s, kvw)], axis=1).astype(bf16)
        bias_s = None if table is None else _sample_bias(table, seq_s, n_cached)
        sink_s = None if sink is None else _sample_sink(sink, seq_s)
        return o_p, _attend_sample(q, ks, vs, bias_s, sink_s, row0=tp, lq=seq_s)

    def tail(c, n_keep):
        rows = jnp.stack([c[(b + 1) * seq - n_keep:(b + 1) * seq] for b in range(nb)])
        return rows.reshape(nb, n_keep, n_kv, HEAD_DIM)

    def rolled(cache, c):
        new = c[tp:].reshape(ns, seq_s, n_kv, HEAD_DIM)
        return jnp.concatenate([cache, new], axis=1)[:, seq_s:]

    ffn_w = [w.astype(bf16)[:, None] for w in (w_gate_ffn, w_up_ffn, w_down_ffn)]
    moe_w = [w.astype(bf16) for w in (w_gate_moe, w_up_moe, w_down_moe)]

    ka_p, va_p, ka_s, va_s = [], [], [], []
    kvb_pad = kb_f = vb_f = None
    for layer in range(depth):
        g_in = row(g_attn[layer])
        if layer < n_a:
            i = layer
            q, kl, kh, vl, vh, k_f32, v_f32 = _project(
                x, g_in, w_qkv_a[i].astype(bf16), _head_gain(g_q_a[i]), _head_gain(g_k_a[i]),
                cos, sin, nq=(w_qkv_a.shape[2] - 2 * kvw) // COL_TILE, has_kv=True, rope=True)
            o = attend(q, (kl, kh, vl, vh), k_f32, v_f32, cache_k_a[i], cache_v_a[i],
                       PREV_CHUNKS_A, None, sink_a[i])
            x = _out_proj(o, w_o_a[i].astype(bf16), x)
            ka_p.append(tail(k_f32, keep_a))
            va_p.append(tail(v_f32, keep_a))
            ka_s.append(rolled(cache_k_a[i], k_f32))
            va_s.append(rolled(cache_v_a[i], v_f32))
        else:
            jb = layer - n_a
            (q,) = _project(x, g_in, w_q_b[jb].astype(bf16), _head_gain(g_q_b[jb]), None,
                            None, None, nq=w_q_b.shape[2] // COL_TILE, has_kv=False, rope=False)
            o = attend(q, kvb_pad, kb_f, vb_f, cache_k_b, cache_v_b, PREV_CHUNKS_B,
                       rel_bias_b[jb], None)
            x = _out_proj(o, w_o_b[jb].astype(bf16), x)

        g_mid = row(g_ffn[layer])
        m = layer // 2
        if layer % 2 == 0:
            n_tiles = t // ROW_TILE
            x = _swiglu_blocks(x, g_mid, *ffn_w, m, jnp.zeros((n_tiles,), jnp.int32),
                               jnp.full((1,), n_tiles, jnp.int32), tm=ROW_TILE, tf=COL_TILE,
                               residual=True)
        else:
            x = _moe(x, g_mid, w_router[m], b_router[m], *moe_w, m,
                     split_at=tp if layer == depth - 1 else None)

        if layer == n_a - 1:
            *kvb_pad, kb_f, vb_f = _project(x, row(g_kv_b), w_kv_b.astype(bf16), None,
                                            _head_gain(g_k_b), None, None,
                                            nq=0, has_kv=True, rope=False)

    if not isinstance(x, tuple):
        x = (x[:tp], x[tp:])
    y_prompt = x[0].reshape(nb, seq, d)
    y_sample = x[1].reshape(ns, seq_s, d)
    return (y_prompt, y_sample, jnp.stack(ka_p), jnp.stack(va_p), tail(kb_f, keep_b),
            tail(vb_f, keep_b), jnp.stack(ka_s), jnp.stack(va_s),
            rolled(cache_k_b, kb_f), rolled(cache_v_b, vb_f))
```

```python
import functools

import jax
import jax.numpy as jnp
from jax import lax
from jax.experimental import pallas as pl
from jax.experimental.pallas import tpu as pltpu

CHUNK = 64
HEAD_DIM = 64
GROUP = 4
WINDOW_A = 128
PREV_CHUNKS_A = WINDOW_A // CHUNK
PREV_CHUNKS_B = 8
BAND_B = PREV_CHUNKS_B * CHUNK
REL_MIN = -(CHUNK - 1)
REL_MAX = 256
ROPE_THETA = 10000.0
TOP_K = 2
RMS_EPS = 1e-6
NEG_INF = -1e30
PAST_LEN = 1024

LANES = 128
HEADS_PER_VREG = LANES // HEAD_DIM
KV_GROUP_WIDTH = GROUP * HEAD_DIM
PAIRS = GROUP // HEADS_PER_VREG

ROW_TILE = 512
COL_TILE = 512
MOE_ROWS = 512
MOE_COL_TILE = 1024
COMBINE_ROWS = 256
ATTN_UNROLL = 4
DMA_THREADS = 2
VMEM_LIMIT = 56 * 1024 * 1024


def _params(semantics):
    return pltpu.CompilerParams(dimension_semantics=semantics, vmem_limit_bytes=VMEM_LIMIT)


def _rms_rows(x, g):
    ms = jnp.sum(x * x, axis=-1, keepdims=True) * (1.0 / x.shape[-1])
    return (x * lax.rsqrt(ms + RMS_EPS)) * g


def _head_norm(y, gain, cos, sin):
    lane = lax.broadcasted_iota(jnp.int32, y.shape, 1)
    low = lane < HEAD_DIM
    ss = y * y
    s_lo = jnp.sum(jnp.where(low, ss, 0.0), axis=-1, keepdims=True)
    s_hi = jnp.sum(jnp.where(low, 0.0, ss), axis=-1, keepdims=True)
    ms = jnp.where(low, s_lo, s_hi) * (1.0 / HEAD_DIM)
    yn = (y * lax.rsqrt(ms + RMS_EPS)) * gain
    if cos is None:
        return yn
    half = HEAD_DIM // 2
    fwd = pltpu.roll(yn, half, 1)
    bwd = pltpu.roll(yn, LANES - half, 1)
    rot = jnp.where((lane % HEAD_DIM) < half, bwd, fwd)
    return yn * cos + rot * sin


def _store_half_padded(lo_ref, hi_ref, c, z):
    lane = lax.broadcasted_iota(jnp.int32, z.shape, 1)
    low = lane < HEAD_DIM
    zr = pltpu.roll(z, HEAD_DIM, 1)
    dt = lo_ref.dtype
    a, b = 2 * c * LANES, (2 * c + 1) * LANES
    lo_ref[:, a:a + LANES] = jnp.where(low, z, 0.0).astype(dt)
    lo_ref[:, b:b + LANES] = jnp.where(low, zr, 0.0).astype(dt)
    hi_ref[:, a:a + LANES] = jnp.where(low, 0.0, zr).astype(dt)
    hi_ref[:, b:b + LANES] = jnp.where(low, 0.0, z).astype(dt)


def _row_specs(x, tm):
    def spec(cols, fn):
        return pl.BlockSpec((tm, cols), lambda i: (fn(i), 0))

    if not isinstance(x, tuple):
        return [spec(x.shape[1], lambda i: i)], 0
    head, tail = x
    assert head.shape[0] % tm == 0 and tail.shape[0] % tm == 0 and head.shape[1] == tail.shape[1]
    n_head = head.shape[0] // tm
    return [spec(head.shape[1], lambda i: jnp.minimum(i, n_head - 1)),
            spec(head.shape[1], lambda i: jnp.maximum(i - n_head, 0))], n_head


def _read_rows(x_refs, n_head):
    if len(x_refs) == 1:
        return x_refs[0][...]
    return jnp.where(pl.program_id(0) < n_head, x_refs[0][...], x_refs[1][...])


def _n_rows(x):
    return x[0].shape[0] + x[1].shape[0] if isinstance(x, tuple) else x.shape[0]


def _as_list(x):
    return list(x) if isinstance(x, tuple) else [x]


def _proj_kernel(*refs, nq, has_kv, rope, n_x, n_head):
    it = iter(refs)
    x_refs = [next(it) for _ in range(n_x)]
    g_ref, w_ref = next(it), next(it)
    gq_ref = next(it) if nq else None
    gk_ref = next(it) if has_kv else None
    cos_ref, sin_ref = (next(it), next(it)) if rope else (None, None)
    q_ref = next(it) if nq else None
    if has_kv:
        kl_ref, kh_ref, vl_ref, vh_ref, kf_ref, vf_ref = (next(it) for _ in range(6))
    h_scr = next(it)
    h_scr[...] = _rms_rows(_read_rows(x_refs, n_head), g_ref[...]).astype(h_scr.dtype)
    cos = cos_ref[...] if rope else None
    sin = sin_ref[...] if rope else None
    groups = COL_TILE // LANES
    for j in range(nq + (2 if has_kv else 0)):
        y = jnp.dot(h_scr[...], w_ref[j], preferred_element_type=jnp.float32)
        for c in range(groups):
            yc = y[:, c * LANES:(c + 1) * LANES]
            if j < nq:
                z = _head_norm(yc, gq_ref[...], cos, sin) * (HEAD_DIM ** -0.5)
                q_ref[:, (j * groups + c) * LANES:(j * groups + c + 1) * LANES] = z.astype(q_ref.dtype)
            elif j == nq:
                z = _head_norm(yc, gk_ref[...], cos, sin)
                kf_ref[:, c * LANES:(c + 1) * LANES] = z
                _store_half_padded(kl_ref, kh_ref, c, z)
            else:
                vf_ref[:, c * LANES:(c + 1) * LANES] = yc
                _store_half_padded(vl_ref, vh_ref, c, yc)


def _project(x, g, w, gq, gk, cos, sin, *, nq, has_kv, rope):
    t = _n_rows(x)
    d = w.shape[0]
    tm = ROW_TILE
    n_col = nq + (2 if has_kv else 0)
    assert w.shape == (d, n_col * COL_TILE) and t % tm == 0
    w = jnp.swapaxes(w.reshape(d, n_col, COL_TILE), 0, 1)
    row = lambda i: (i, 0)
    x_specs, n_head = _row_specs(x, tm)
    in_specs = x_specs + [pl.BlockSpec((1, d), lambda i: (0, 0)),
                          pl.BlockSpec((n_col, d, COL_TILE), lambda i: (0, 0, 0))]
    args = _as_list(x) + [g, w]
    vec = pl.BlockSpec((1, LANES), lambda i: (0, 0))
    if nq:
        in_specs.append(vec)
        args.append(gq)
    if has_kv:
        in_specs.append(vec)
        args.append(gk)
    if rope:
        in_specs += [pl.BlockSpec((tm, LANES), row)] * 2
        args += [cos, sin]
    out_shape, out_specs = [], []
    if nq:
        out_shape.append(jax.ShapeDtypeStruct((t, nq * COL_TILE), jnp.bfloat16))
        out_specs.append(pl.BlockSpec((tm, nq * COL_TILE), row))
    if has_kv:
        for _ in range(4):
            out_shape.append(jax.ShapeDtypeStruct((t, HEADS_PER_VREG * COL_TILE), jnp.bfloat16))
            out_specs.append(pl.BlockSpec((tm, HEADS_PER_VREG * COL_TILE), row))
        for _ in range(2):
            out_shape.append(jax.ShapeDtypeStruct((t, COL_TILE), jnp.float32))
            out_specs.append(pl.BlockSpec((tm, COL_TILE), row))
    return pl.pallas_call(
        functools.partial(_proj_kernel, nq=nq, has_kv=has_kv, rope=rope,
                          n_x=len(x_specs), n_head=n_head),
        grid=(t // tm,),
        in_specs=in_specs,
        out_specs=out_specs,
        out_shape=out_shape,
        scratch_shapes=[pltpu.VMEM((tm, d), jnp.bfloat16)],
        compiler_params=_params(("parallel",)),
        name="proj",
    )(*args)


def _attn_prompt_kernel(*refs, seq, band, n_prev, unroll, has_bias, has_sink):
    it = iter(refs)
    q_ref, kl_ref, kh_ref, vl_ref, vh_ref = (next(it) for _ in range(5))
    bias_ref = next(it) if has_bias else None
    sink_ref = next(it) if has_sink else None
    o_ref = next(it)
    s_scr = (next(it), next(it))
    p_scr = (next(it), next(it))
    n_scr = (next(it), next(it))
    row = lax.broadcasted_iota(jnp.int32, (band, LANES), 0)
    nt = (((1,), (1,)), ((), ()))
    tn = (((0,), (0,)), ((), ()))
    n_groups = seq // (CHUNK * unroll)
    n_masked = min(-(-n_prev // unroll), n_groups)

    def key_start(c):
        return pl.multiple_of(jnp.maximum(c - n_prev, 0) * CHUNK, CHUNK)

    def scores(gi, slot):
        for u in range(unroll):
            c = gi * unroll + u
            r0 = pl.multiple_of(c * CHUNK, CHUNK)
            k0 = key_start(c)
            qs = jnp.concatenate([q_ref[pl.ds(r0, CHUNK), p * LANES:(p + 1) * LANES]
                                  for p in range(PAIRS)], axis=0)
            for half, k_ref in enumerate((kl_ref, kh_ref)):
                s_scr[slot][u, half] = lax.dot_general(k_ref[pl.ds(k0, band), :], qs, nt,
                                                        preferred_element_type=jnp.float32)

    def softmax(gi, slot, masked):
        for u in range(unroll):
            c = gi * unroll + u
            shift = pl.multiple_of(jnp.maximum(n_prev - c, 0) * CHUNK, CHUNK) if masked else 0
            inv = []
            for half in range(HEADS_PER_VREG):
                s = s_scr[slot][u, half]
                if has_bias:
                    s = s + bias_ref[half, pl.ds(shift, band), :]
                if masked:
                    s = jnp.where(row < band - shift, s, NEG_INF)
                m = jnp.max(s, axis=0, keepdims=True)
                if has_sink:
                    m = jnp.maximum(m, sink_ref[half])
                e = jnp.exp(s - m)
                l = jnp.sum(e, axis=0, keepdims=True)
                if has_sink:
                    l = l + jnp.exp(sink_ref[half] - m)
                p_scr[slot][u, half] = e.astype(p_scr[slot].dtype)
                inv.append(jnp.broadcast_to(1.0 / l, (LANES, LANES)))
            n_scr[slot][u] = jnp.where(row[:LANES] < HEAD_DIM, inv[0], inv[1])

    def outputs(gi, slot):
        for u in range(unroll):
            c = gi * unroll + u
            r0 = pl.multiple_of(c * CHUNK, CHUNK)
            k0 = key_start(c)
            o = (lax.dot_general(p_scr[slot][u, 0], vl_ref[pl.ds(k0, band), :], tn,
                                 preferred_element_type=jnp.float32)
                 + lax.dot_general(p_scr[slot][u, 1], vh_ref[pl.ds(k0, band), :], tn,
                                   preferred_element_type=jnp.float32))
            o = o * n_scr[slot][u].T
            for p in range(PAIRS):
                o_ref[pl.ds(r0, CHUNK), p * LANES:(p + 1) * LANES] = (
                    o[p * CHUNK:(p + 1) * CHUNK].astype(o_ref.dtype))

    def stage(g, parity, masked):
        static = isinstance(g, int)
        if not static or g < n_groups:
            scores(g, parity)
        if not static or 0 <= g - 1 < n_groups:
            softmax(g - 1, 1 - parity, masked)
        if not static or 0 <= g - 2 < n_groups:
            outputs(g - 2, parity)

    first_loop = max(n_masked + 1, 2)
    first_loop += first_loop % 2
    n_pairs = max(n_groups - first_loop, 0) // 2
    for g in range(min(first_loop, n_groups)):
        stage(g, g % 2, masked=True)
    if n_pairs:
        def body(k, carry):
            g = first_loop + 2 * k
            stage(g, 0, masked=False)
            stage(g + 1, 1, masked=False)
            return carry
        lax.fori_loop(0, n_pairs, body, 0)
    for g in range(min(first_loop, n_groups) + 2 * n_pairs, n_groups + 2):
        stage(g, g % 2, masked=(g - 1) < n_masked)


def _attend_prompt(q, kl, kh, vl, vh, bias, sink, *, nb, seq, n_prev):
    qw = q.shape[1]
    n_kv = kl.shape[1] // LANES
    band = (n_prev + 1) * CHUNK
    unroll = ATTN_UNROLL
    assert seq % (CHUNK * unroll) == 0 and seq >= band and seq >= 2 * CHUNK * unroll
    kv_spec = pl.BlockSpec((seq, LANES), lambda b, j: (b, j))
    in_specs = [pl.BlockSpec((seq, KV_GROUP_WIDTH), lambda b, j: (b, j))] + [kv_spec] * 4
    args = [q, kl, kh, vl, vh]
    if bias is not None:
        in_specs.append(pl.BlockSpec((HEADS_PER_VREG,) + bias.shape[1:], lambda b, j: (j, 0, 0)))
        args.append(bias)
    if sink is not None:
        in_specs.append(pl.BlockSpec((HEADS_PER_VREG, 1, LANES), lambda b, j: (j, 0, 0)))
        args.append(sink)
    return pl.pallas_call(
        functools.partial(_attn_prompt_kernel, seq=seq, band=band, n_prev=n_prev, unroll=unroll,
                          has_bias=bias is not None, has_sink=sink is not None),
        grid=(nb, n_kv),
        in_specs=in_specs,
        out_specs=pl.BlockSpec((seq, KV_GROUP_WIDTH), lambda b, j: (b, j)),
        out_shape=jax.ShapeDtypeStruct((nb * seq, qw), jnp.bfloat16),
        scratch_shapes=[pltpu.VMEM((unroll, HEADS_PER_VREG, band, LANES), jnp.float32)] * 2
        + [pltpu.VMEM((unroll, HEADS_PER_VREG, band, LANES), jnp.bfloat16)] * 2
        + [pltpu.VMEM((unroll, LANES, LANES), jnp.float32)] * 2,
        compiler_params=_params(("parallel", "arbitrary")),
        name="attn_prompt",
    )(*args)


def _attn_sample_kernel(*refs, lq, has_bias, has_sink, n_kv):
    it = iter(refs)
    q_ref, k_ref, v_ref = next(it), next(it), next(it)
    bias_ref = next(it) if has_bias else None
    sink_ref = next(it) if has_sink else None
    o_ref = next(it)
    for j in range(n_kv):
        qj = q_ref[:, j * KV_GROUP_WIDTH:(j + 1) * KV_GROUP_WIDTH]
        qcat = jnp.concatenate(
            [qj[:, g * HEAD_DIM:(g + 1) * HEAD_DIM] for g in range(GROUP)], axis=0)
        kj = k_ref[0, :, j * HEAD_DIM:(j + 1) * HEAD_DIM]
        vj = v_ref[0, :, j * HEAD_DIM:(j + 1) * HEAD_DIM]
        s = lax.dot_general(qcat, kj, (((1,), (1,)), ((), ())),
                            preferred_element_type=jnp.float32)
        if has_bias:
            s = s + bias_ref[j]
        m = jnp.max(s, axis=-1, keepdims=True)
        if has_sink:
            sink = sink_ref[j]
            m = jnp.maximum(m, sink)
        e = jnp.exp(s - m)
        l = jnp.sum(e, axis=-1, keepdims=True)
        if has_sink:
            l = l + jnp.exp(sink - m)
        p = (e * (1.0 / l)).astype(vj.dtype)
        o = jnp.dot(p, vj, preferred_element_type=jnp.float32)
        ocat = jnp.concatenate([o[g * lq:(g + 1) * lq] for g in range(GROUP)], axis=1)
        o_ref[:, j * KV_GROUP_WIDTH:(j + 1) * KV_GROUP_WIDTH] = ocat.astype(o_ref.dtype)


def _attend_sample(q, k, v, bias, sink, *, row0, lq):
    ns, lk, kw = k.shape
    qw = q.shape[1]
    assert row0 % lq == 0
    blk0 = row0 // lq
    in_specs = [pl.BlockSpec((lq, qw), lambda i: (blk0 + i, 0)),
                pl.BlockSpec((1, lk, kw), lambda i: (i, 0, 0)),
                pl.BlockSpec((1, lk, kw), lambda i: (i, 0, 0))]
    args = [q, k, v]
    if bias is not None:
        in_specs.append(pl.BlockSpec(bias.shape, lambda i: (0, 0, 0)))
        args.append(bias)
    if sink is not None:
        in_specs.append(pl.BlockSpec(sink.shape, lambda i: (0, 0, 0)))
        args.append(sink)
    return pl.pallas_call(
        functools.partial(_attn_sample_kernel, lq=lq, has_bias=bias is not None,
                          has_sink=sink is not None, n_kv=kw // HEAD_DIM),
        grid=(ns,),
        in_specs=in_specs,
        out_specs=pl.BlockSpec((lq, qw), lambda i: (i, 0)),
        out_shape=jax.ShapeDtypeStruct((ns * lq, qw), jnp.bfloat16),
        compiler_params=_params(("arbitrary",)),
        name="attn_sample",
    )(*args)


def _out_proj_kernel(*refs, n_o, o_head, n_x, x_head):
    o_refs, w_ref = refs[:n_o], refs[n_o]
    x_refs, y_ref = refs[n_o + 1:n_o + 1 + n_x], refs[n_o + 1 + n_x]
    y_ref[...] = _read_rows(x_refs, x_head) + jnp.dot(
        _read_rows(o_refs, o_head), w_ref[...], preferred_element_type=jnp.float32)


def _out_proj(o, w, x):
    t = _n_rows(x)
    d = w.shape[1]
    tm = ROW_TILE
    o_specs, o_head = _row_specs(o, tm)
    x_specs, x_head = _row_specs(x, tm)
    return pl.pallas_call(
        functools.partial(_out_proj_kernel, n_o=len(o_specs), o_head=o_head,
                          n_x=len(x_specs), x_head=x_head),
        grid=(t // tm,),
        in_specs=o_specs + [pl.BlockSpec(w.shape, lambda i: (0, 0))] + x_specs,
        out_specs=pl.BlockSpec((tm, d), lambda i: (i, 0)),
        out_shape=jax.ShapeDtypeStruct((t, d), jnp.float32),
        compiler_params=_params(("parallel",)),
        name="out_proj",
    )(*_as_list(o), w, *_as_list(x))


def _ffn_kernel(e_ref, n_ref, x_ref, g_ref, wg_ref, wu_ref, wd_ref, y_ref, h_scr, acc_scr, *,
                residual):
    f = pl.program_id(1)
    last = pl.num_programs(1) - 1

    @pl.when(pl.program_id(0) < n_ref[0])
    def _():
        @pl.when(f == 0)
        def _():
            h_scr[...] = _rms_rows(x_ref[...], g_ref[...]).astype(h_scr.dtype)

        h = h_scr[...]
        gate = jnp.dot(h, wg_ref[...], preferred_element_type=jnp.float32)
        up = jnp.dot(h, wu_ref[...], preferred_element_type=jnp.float32)
        act = (gate * jax.nn.sigmoid(gate)) * up
        part = jnp.dot(act.astype(jnp.bfloat16), wd_ref[...],
                       preferred_element_type=jnp.float32)

        @pl.when(f == 0)
        def _():
            acc_scr[...] = part

        @pl.when(f > 0)
        def _():
            acc_scr[...] += part

        @pl.when(f == last)
        def _():
            y_ref[...] = (x_ref[...] + acc_scr[...]) if residual else acc_scr[...]

    @pl.when((pl.program_id(0) >= n_ref[0]) & (f == last))
    def _():
        y_ref[...] = jnp.zeros_like(y_ref)


def _swiglu_blocks(x, g, wg, wu, wd, layer, block_expert, n_used, *, tm, tf, residual):
    r, d = x.shape
    ff = wg.shape[3]
    assert r % tm == 0 and ff % tf == 0
    n_f = ff // tf

    def blk(i, n):
        return jnp.minimum(i, n[0] - 1)

    def col(i, f, n):
        return jnp.where(i < n[0], f, n_f - 1)

    grid_spec = pltpu.PrefetchScalarGridSpec(
        num_scalar_prefetch=2,
        grid=(r // tm, n_f),
        in_specs=[pl.BlockSpec((tm, d), lambda i, f, e, n: (blk(i, n), 0)),
                  pl.BlockSpec((1, d), lambda i, f, e, n: (0, 0)),
                  pl.BlockSpec((None, None, d, tf),
                               lambda i, f, e, n: (layer, e[blk(i, n)], 0, col(i, f, n))),
                  pl.BlockSpec((None, None, d, tf),
                               lambda i, f, e, n: (layer, e[blk(i, n)], 0, col(i, f, n))),
                  pl.BlockSpec((None, None, tf, d),
                               lambda i, f, e, n: (layer, e[blk(i, n)], col(i, f, n), 0))],
        out_specs=pl.BlockSpec((tm, d), lambda i, f, e, n: (i, 0)),
        scratch_shapes=[pltpu.VMEM((tm, d), jnp.bfloat16),
                        pltpu.VMEM((tm, d), jnp.float32)],
    )
    return pl.pallas_call(
        functools.partial(_ffn_kernel, residual=residual),
        grid_spec=grid_spec,
        out_shape=jax.ShapeDtypeStruct((r, d), jnp.float32),
        compiler_params=_params(("arbitrary", "arbitrary")),
        name="swiglu",
    )(block_expert, n_used, x, g, wg, wu, wd)


def _router_kernel(x_ref, g_ref, w_ref, b_ref, idx_ref, gate_ref, h_ref):
    h = _rms_rows(x_ref[...], g_ref[...]).astype(jnp.bfloat16)
    h_ref[...] = h.reshape(h_ref.shape)
    logits = jnp.dot(h, w_ref[...], preferred_element_type=jnp.float32) + b_ref[...]
    lane = lax.broadcasted_iota(jnp.int32, logits.shape, 1)
    m1 = jnp.max(logits, axis=-1, keepdims=True)
    e1 = jnp.min(jnp.where(logits == m1, lane, LANES), axis=-1, keepdims=True)
    rest = jnp.where(lane == e1, -jnp.inf, logits)
    m2 = jnp.max(rest, axis=-1, keepdims=True)
    e2 = jnp.min(jnp.where(rest == m2, lane, LANES), axis=-1, keepdims=True)
    t = jnp.exp(m2 - m1)
    denom = 1.0 + t
    idx_ref[...] = jnp.where(lane == 0, e1, jnp.where(lane == 1, e2, 0))
    gate_ref[...] = jnp.where(lane == 0, 1.0 / denom, jnp.where(lane == 1, t / denom, 0.0))


def _route(x, g, w_router, b_router):
    t, d = x.shape
    tm = ROW_TILE
    n_e = w_router.shape[1]
    w = jnp.zeros((d, LANES), jnp.bfloat16).at[:, :n_e].set(w_router.astype(jnp.bfloat16))
    b = jnp.full((1, LANES), -jnp.inf, jnp.float32).at[0, :n_e].set(b_router.astype(jnp.float32))
    return pl.pallas_call(
        _router_kernel,
        grid=(t // tm,),
        in_specs=[pl.BlockSpec((tm, d), lambda i: (i, 0)),
                  pl.BlockSpec((1, d), lambda i: (0, 0)),
                  pl.BlockSpec((d, LANES), lambda i: (0, 0)),
                  pl.BlockSpec((1, LANES), lambda i: (0, 0))],
        out_specs=[pl.BlockSpec((tm, LANES), lambda i: (i, 0))] * 2
        + [pl.BlockSpec((tm, d // LANES, LANES), lambda i: (i, 0, 0))],
        out_shape=[jax.ShapeDtypeStruct((t, LANES), jnp.int32),
                   jax.ShapeDtypeStruct((t, LANES), jnp.float32),
                   jax.ShapeDtypeStruct((t, d // LANES, LANES), jnp.bfloat16)],
        compiler_params=_params(("parallel",)),
        name="router",
    )(x, g, w, b)


def _experts_kernel(e_ref, n_ref, first_ref, next_ref, dst_ref, h_hbm, wg_ref, wu_ref, wd_ref,
                    y_hbm, xbuf, ybuf, h_scr, acc_scr, gsem, ssem, *, tm):
    i, f = pl.program_id(0), pl.program_id(1)
    last = pl.num_programs(1) - 1
    n_used = n_ref[0]
    unroll = 2 * DMA_THREADS

    def per_slot(fn):
        for s_ in range(2):
            @pl.when(i % 2 == s_)
            def _():
                fn(s_)

    def gather(idx_ref, to_slot):
        def issue(r4, carry):
            for k in range(unroll):
                r = r4 * unroll + k
                pltpu.make_async_copy(h_hbm.at[idx_ref[r]], xbuf.at[to_slot, r],
                                      gsem.at[to_slot]).start(priority=k % DMA_THREADS)
            return carry
        lax.fori_loop(0, tm // unroll, issue, 0)

    def scatter(from_slot):
        def issue(r4, carry):
            for k in range(unroll):
                r = r4 * unroll + k
                pltpu.make_async_copy(ybuf.at[from_slot, r], y_hbm.at[dst_ref[r]],
                                      ssem.at[from_slot]).start(priority=k % DMA_THREADS)
            return carry
        lax.fori_loop(0, tm // unroll, issue, 0)

    def wait_gather(s_):
        pltpu.make_async_copy(h_hbm.at[pl.ds(0, tm)], xbuf.at[s_], gsem.at[s_]).wait()

    def wait_scatter(s_):
        pltpu.make_async_copy(ybuf.at[s_], y_hbm.at[pl.ds(0, tm)], ssem.at[s_]).wait()

    @pl.when(i < n_used)
    def _():
        @pl.when(f == 0)
        def _():
            @pl.when(i == 0)
            def _():
                gather(first_ref, 0)
                ybuf[1] = jnp.zeros(ybuf.shape[1:], ybuf.dtype)
                for half in range(2):
                    spare = pltpu.make_async_copy(
                        ybuf.at[1], y_hbm.at[pl.ds(y_hbm.shape[0] - (2 - half) * tm, tm)],
                        ssem.at[1])
                    spare.start()
                    spare.wait()

            def fetch(s_):
                wait_gather(s_)
                h_scr[...] = xbuf[s_].reshape(h_scr.shape)

                @pl.when(i + 1 < n_used)
                def _():
                    gather(next_ref, 1 - s_)

            per_slot(fetch)

        h = h_scr[...]
        gate = jnp.dot(h, wg_ref[...], preferred_element_type=jnp.float32)
        up = jnp.dot(h, wu_ref[...], preferred_element_type=jnp.float32)
        act = (gate * jax.nn.sigmoid(gate)) * up
        part = jnp.dot(act.astype(jnp.bfloat16), wd_ref[...],
                       preferred_element_type=jnp.float32)

        @pl.when(f == 0)
        def _():
            acc_scr[...] = part

        @pl.when(f > 0)
        def _():
            acc_scr[...] += part

        @pl.when(f == last)
        def _():
            def emit(s_):
                @pl.when(i >= 2)
                def _():
                    wait_scatter(s_)

                ybuf[s_] = acc_scr[...].reshape(ybuf.shape[1:])
                scatter(s_)

                @pl.when(i == n_used - 1)
                def _():
                    wait_scatter(s_)

                    @pl.when(i >= 1)
                    def _():
                        wait_scatter(1 - s_)

            per_slot(emit)


def _experts(h_rows, wg, wu, wd, layer, block_expert, n_used, row_tok, row_dst, n_out, *, tm, tf):
    tile = h_rows.shape[1:]
    d = wg.shape[2]
    ff = wg.shape[3]
    rows = row_tok.shape[0]
    assert rows % tm == 0 and ff % tf == 0 and tm % (2 * DMA_THREADS) == 0
    n_f = ff // tf

    def blk(i, n):
        return jnp.minimum(i, n[0] - 1)

    def col(i, f, n):
        return jnp.where(i < n[0], f, n_f - 1)

    smem = functools.partial(pl.BlockSpec, memory_space=pltpu.SMEM)
    grid_spec = pltpu.PrefetchScalarGridSpec(
        num_scalar_prefetch=2,
        grid=(rows // tm, n_f),
        in_specs=[smem((tm,), lambda i, f, e, n: (0,)),
                  smem((tm,), lambda i, f, e, n: (blk(i + 1, n),)),
                  smem((tm,), lambda i, f, e, n: (blk(i, n),)),
                  pl.BlockSpec(memory_space=pl.ANY),
                  pl.BlockSpec((None, None, d, tf),
                               lambda i, f, e, n: (layer, e[blk(i, n)], 0, col(i, f, n))),
                  pl.BlockSpec((None, None, d, tf),
                               lambda i, f, e, n: (layer, e[blk(i, n)], 0, col(i, f, n))),
                  pl.BlockSpec((None, None, tf, d),
                               lambda i, f, e, n: (layer, e[blk(i, n)], col(i, f, n), 0))],
        out_specs=pl.BlockSpec(memory_space=pl.ANY),
        scratch_shapes=[pltpu.VMEM((2, tm) + tile, jnp.bfloat16),
                        pltpu.VMEM((2, tm) + tile, jnp.float32),
                        pltpu.VMEM((tm, d), jnp.bfloat16),
                        pltpu.VMEM((tm, d), jnp.float32),
                        pltpu.SemaphoreType.DMA((2,)),
                        pltpu.SemaphoreType.DMA((2,))],
    )
    return pl.pallas_call(
        functools.partial(_experts_kernel, tm=tm),
        grid_spec=grid_spec,
        out_shape=jax.ShapeDtypeStruct((n_out,) + tile, jnp.float32),
        compiler_params=_params(("arbitrary", "arbitrary")),
        name="experts",
    )(block_expert, n_used, row_tok, row_tok, row_dst, h_rows, wg, wu, wd)


def _mix_kernel(x_ref, gate_ref, y0_ref, y1_ref, *out_refs, n_head):
    gates = gate_ref[...]
    shape = x_ref.shape
    y = gates[:, 0:1] * y0_ref[...].reshape(shape) + gates[:, 1:2] * y1_ref[...].reshape(shape)
    out = x_ref[...] + y
    if len(out_refs) == 1:
        out_refs[0][...] = out
    else:
        @pl.when(pl.program_id(0) < n_head)
        def _():
            out_refs[0][...] = out

        @pl.when(pl.program_id(0) >= n_head)
        def _():
            out_refs[1][...] = out


def _mix(x, gates, y, split_at=None):
    t, d = x.shape
    rows = COMBINE_ROWS
    tile = y.shape[1:]
    assert t % rows == 0
    n_t = t // rows
    if split_at is None:
        n_head = 0
        out_specs = pl.BlockSpec((rows, d), lambda i: (i, 0))
        out_shape = jax.ShapeDtypeStruct((t, d), jnp.float32)
    else:
        assert split_at % rows == 0 and 0 < split_at < t
        n_head = split_at // rows
        out_specs = [pl.BlockSpec((rows, d), lambda i: (jnp.minimum(i, n_head - 1), 0)),
                     pl.BlockSpec((rows, d), lambda i: (jnp.maximum(i - n_head, 0), 0))]
        out_shape = [jax.ShapeDtypeStruct((split_at, d), jnp.float32),
                     jax.ShapeDtypeStruct((t - split_at, d), jnp.float32)]
    return pl.pallas_call(
        functools.partial(_mix_kernel, n_head=n_head),
        grid=(n_t,),
        in_specs=[pl.BlockSpec((rows, d), lambda i: (i, 0)),
                  pl.BlockSpec((rows, LANES), lambda i: (i, 0)),
                  pl.BlockSpec((rows,) + tile, lambda i: (i, 0, 0)),
                  pl.BlockSpec((rows,) + tile, lambda i: (n_t + i, 0, 0))],
        out_specs=out_specs,
        out_shape=out_shape,
        compiler_params=_params(("arbitrary",)),
        name="mix",
    )(x, gates, y, y)


def _moe(x, g, w_router, b_router, wg, wu, wd, layer, split_at=None):
    t, d = x.shape
    n_e = wg.shape[1]
    n_assign = t * TOP_K
    top_idx, gates, h_rows = _route(x, g, w_router, b_router)
    flat_e = top_idx[:, :TOP_K].reshape(-1)
    onehot = (flat_e[:, None] == jnp.arange(n_e, dtype=jnp.int32)[None, :]).astype(jnp.int32)
    rank = jnp.sum((jnp.cumsum(onehot, axis=0) - onehot) * onehot, axis=1)
    counts = jnp.sum(onehot, axis=0)
    padded = (counts + MOE_ROWS - 1) // MOE_ROWS * MOE_ROWS
    ends = jnp.cumsum(padded)
    dest = jnp.sum((ends - padded)[None, :] * onehot, axis=1) + rank
    n_blocks = (n_assign + n_e * (MOE_ROWS - 1) + MOE_ROWS - 1) // MOE_ROWS
    rows = n_blocks * MOE_ROWS
    row_assign = jnp.full((rows,), -1, jnp.int32).at[dest].set(jnp.arange(n_assign, dtype=jnp.int32))
    real = row_assign >= 0
    row_tok = jnp.where(real, row_assign // TOP_K, 0)
    spare = n_assign + (jnp.arange(rows, dtype=jnp.int32) // MOE_ROWS % 2) * MOE_ROWS \
        + jnp.arange(rows, dtype=jnp.int32) % MOE_ROWS
    row_dst = jnp.where(real, (row_assign % TOP_K) * t + row_assign // TOP_K, spare)
    block_start = jnp.arange(n_blocks, dtype=jnp.int32) * MOE_ROWS
    block_expert = jnp.minimum(
        jnp.sum((ends[None, :] <= block_start[:, None]).astype(jnp.int32), axis=1), n_e - 1)
    n_used = (ends[-1:] // MOE_ROWS).astype(jnp.int32)
    y = _experts(h_rows, wg, wu, wd, layer, block_expert, n_used, row_tok, row_dst,
                 n_assign + 2 * MOE_ROWS, tm=MOE_ROWS, tf=MOE_COL_TILE)
    return _mix(x, gates, y, split_at)


def _rope_tables(pos):
    half = HEAD_DIM // 2
    inv_freq = 1.0 / (ROPE_THETA ** (jnp.arange(0, HEAD_DIM, 2, dtype=jnp.float32) / HEAD_DIM))
    ang = pos.astype(jnp.float32)[:, None] * inv_freq[None, :]
    cos, sin = jnp.cos(ang), jnp.sin(ang)
    reps = LANES // half
    signs = jnp.tile(jnp.concatenate([-jnp.ones((half,)), jnp.ones((half,))]), HEADS_PER_VREG)
    return jnp.tile(cos, (1, reps)), jnp.tile(sin, (1, reps)) * signs[None, :].astype(jnp.float32)


def _rel_bias(table, lq, n_before):
    n_keys = n_before + lq
    rel = n_before + (lq - 1) - jnp.arange(n_keys + lq - 1)
    diag = table.astype(jnp.float32)[:, jnp.clip(rel, REL_MIN, REL_MAX) - REL_MIN]
    return jnp.stack([diag[:, lq - 1 - q:lq - 1 - q + n_keys] for q in range(lq)], axis=1)


def _pair_major(a):
    h = a.shape[0]
    a = a.reshape((h // GROUP, PAIRS, HEADS_PER_VREG) + a.shape[1:])
    return jnp.swapaxes(a, 1, 2)


def _prompt_bias(table, n_prev):
    pad = n_prev * CHUNK
    b = _pair_major(_rel_bias(table, CHUNK, pad))
    kvh = b.shape[0]
    b = jnp.transpose(b, (0, 1, 4, 2, 3)).reshape(kvh * HEADS_PER_VREG, pad + CHUNK, LANES)
    return jnp.pad(b, ((0, 0), (0, pad), (0, 0)))


def _prompt_sink(sink):
    s = _pair_major(sink.astype(jnp.float32))
    s = jnp.repeat(s[..., None], CHUNK, axis=-1)
    return s.reshape(-1, 1, LANES)


def _sample_bias(table, lq, n_cached):
    b = _rel_bias(table, lq, n_cached)
    return b.reshape(b.shape[0] // GROUP, GROUP * lq, n_cached + lq)


def _sample_sink(sink, lq):
    h = sink.shape[0]
    s = jnp.repeat(sink.astype(jnp.float32).reshape(h // GROUP, GROUP, 1), lq, axis=2)
    return s.reshape(h // GROUP, GROUP * lq, 1)


def _head_gain(g):
    return jnp.tile(g.astype(jnp.float32), HEADS_PER_VREG)[None, :]


def kernel(x_prompt, x_sample, cache_k_a, cache_v_a, cache_k_b, cache_v_b, g_attn, g_ffn, w_qkv_a, g_q_a, g_k_a, sink_a, w_o_a, g_kv_b, w_kv_b, g_k_b, w_q_b, g_q_b, rel_bias_b, w_o_b, w_gate_ffn, w_up_ffn, w_down_ffn, w_router, b_router, w_gate_moe, w_up_moe, w_down_moe):
    bf16 = jnp.bfloat16
    nb, seq, d = x_prompt.shape
    ns, seq_s, _ = x_sample.shape
    tp, ts = nb * seq, ns * seq_s
    t = tp + ts
    depth = g_attn.shape[0]
    n_a = w_qkv_a.shape[0]
    n_kv = cache_k_b.shape[2]
    kvw = n_kv * HEAD_DIM
    keep_a, keep_b = min(WINDOW_A, seq), min(BAND_B, seq)
    assert t % ROW_TILE == 0

    x = (x_prompt.reshape(tp, d), x_sample.reshape(ts, d))
    pos = jnp.concatenate([jnp.tile(jnp.arange(seq), nb),
                           jnp.tile(PAST_LEN + jnp.arange(seq_s), ns)])
    cos, sin = _rope_tables(pos)
    row = lambda v: v.astype(jnp.float32)[None, :]

    def attend(q, kv_pad, k_f32, v_f32, cache_k, cache_v, n_prev, table, sink):
        bias_p = None if table is None else _prompt_bias(table, n_prev)
        sink_p = None if sink is None else _prompt_sink(sink)
        o_p = _attend_prompt(q, *kv_pad, bias_p, sink_p, nb=nb, seq=seq, n_prev=n_prev)
        n_cached = cache_k.shape[1]
        ks = jnp.concatenate([cache_k.reshape(ns, n_cached, kvw),
                              k_f32[tp:].reshape(ns, seq_s, kvw)], axis=1).astype(bf16)
        vs = jnp.concatenate([cache_v.reshape(ns, n_cached, kvw),
                              v_f32[tp:].reshape(ns, seq_s, kvw)], axis=1).astype(bf16)
        bias_s = None if table is None else _sample_bias(table, seq_s, n_cached)
        sink_s = None if sink is None else _sample_sink(sink, seq_s)
        return o_p, _attend_sample(q, ks, vs, bias_s, sink_s, row0=tp, lq=seq_s)

    def tail(c, n_keep):
        rows = jnp.stack([c[(b + 1) * seq - n_keep:(b + 1) * seq] for b in range(nb)])
        return rows.reshape(nb, n_keep, n_kv, HEAD_DIM)

    def rolled(cache, c):
        new = c[tp:].reshape(ns, seq_s, n_kv, HEAD_DIM)
        return jnp.concatenate([cache, new], axis=1)[:, seq_s:]

    ffn_w = [w.astype(bf16)[:, None] for w in (w_gate_ffn, w_up_ffn, w_down_ffn)]
    moe_w = [w.astype(bf16) for w in (w_gate_moe, w_up_moe, w_down_moe)]

    ka_p, va_p, ka_s, va_s = [], [], [], []
    kvb_pad = kb_f = vb_f = None
    for layer in range(depth):
        g_in = row(g_attn[layer])
        if layer < n_a:
            i = layer
            q, kl, kh, vl, vh, k_f32, v_f32 = _project(
                x, g_in, w_qkv_a[i].astype(bf16), _head_gain(g_q_a[i]), _head_gain(g_k_a[i]),
                cos, sin, nq=(w_qkv_a.shape[2] - 2 * kvw) // COL_TILE, has_kv=True, rope=True)
            o = attend(q, (kl, kh, vl, vh), k_f32, v_f32, cache_k_a[i], cache_v_a[i],
                       PREV_CHUNKS_A, None, sink_a[i])
            x = _out_proj(o, w_o_a[i].astype(bf16), x)
            ka_p.append(tail(k_f32, keep_a))
            va_p.append(tail(v_f32, keep_a))
            ka_s.append(rolled(cache_k_a[i], k_f32))
            va_s.append(rolled(cache_v_a[i], v_f32))
        else:
            jb = layer - n_a
            (q,) = _project(x, g_in, w_q_b[jb].astype(bf16), _head_gain(g_q_b[jb]), None,
                            None, None, nq=w_q_b.shape[2] // COL_TILE, has_kv=False, rope=False)
            o = attend(q, kvb_pad, kb_f, vb_f, cache_k_b, cache_v_b, PREV_CHUNKS_B,
                       rel_bias_b[jb], None)
            x = _out_proj(o, w_o_b[jb].astype(bf16), x)

        g_mid = row(g_ffn[layer])
        m = layer // 2
        if layer % 2 == 0:
            n_tiles = t // ROW_TILE
            x = _swiglu_blocks(x, g_mid, *ffn_w, m, jnp.zeros((n_tiles,), jnp.int32),
                               jnp.full((1,), n_tiles, jnp.int32), tm=ROW_TILE, tf=COL_TILE,
                               residual=True)
        else:
            x = _moe(x, g_mid, w_router[m], b_router[m], *moe_w, m,
                     split_at=tp if layer == depth - 1 else None)

        if layer == n_a - 1:
            *kvb_pad, kb_f, vb_f = _project(x, row(g_kv_b), w_kv_b.astype(bf16), None,
                                            _head_gain(g_k_b), None, None,
                                            nq=0, has_kv=True, rope=False)

    if not isinstance(x, tuple):
        x = (x[:tp], x[tp:])
    y_prompt = x[0].reshape(nb, seq, d)
    y_sample = x[1].reshape(ns, seq_s, d)
    return (y_prompt, y_sample, jnp.stack(ka_p), jnp.stack(va_p), tail(kb_f, keep_b),
            tail(vb_f, keep_b), jnp.stack(ka_s), jnp.stack(va_s),
            rolled(cache_k_b, kb_f), rolled(cache_v_b, vb_f))
```

```python
import functools

import jax
import jax.numpy as jnp
from jax import lax
from jax.experimental import pallas as pl
from jax.experimental.pallas import tpu as pltpu

CHUNK = 64
HEAD_DIM = 64
GROUP = 4
WINDOW_A = 128
PREV_CHUNKS_A = WINDOW_A // CHUNK
PREV_CHUNKS_B = 8
BAND_B = PREV_CHUNKS_B * CHUNK
REL_MIN = -(CHUNK - 1)
REL_MAX = 256
ROPE_THETA = 10000.0
TOP_K = 2
RMS_EPS = 1e-6
NEG_INF = -1e30
PAST_LEN = 1024

LANES = 128
HEADS_PER_VREG = LANES // HEAD_DIM
KV_GROUP_WIDTH = GROUP * HEAD_DIM
PAIRS = GROUP // HEADS_PER_VREG

ROW_TILE = 512
COL_TILE = 512
MOE_ROWS = 512
MOE_COL_TILE = 1024
COMBINE_ROWS = 256
ATTN_UNROLL = 4
DMA_THREADS = 2
VMEM_LIMIT = 56 * 1024 * 1024


def _params(semantics):
    return pltpu.CompilerParams(dimension_semantics=semantics, vmem_limit_bytes=VMEM_LIMIT)


def _rms_rows(x, g):
    ms = jnp.sum(x * x, axis=-1, keepdims=True) * (1.0 / x.shape[-1])
    return (x * lax.rsqrt(ms + RMS_EPS)) * g


def _head_norm(y, gain, cos, sin):
    lane = lax.broadcasted_iota(jnp.int32, y.shape, 1)
    low = lane < HEAD_DIM
    ss = y * y
    s_lo = jnp.sum(jnp.where(low, ss, 0.0), axis=-1, keepdims=True)
    s_hi = jnp.sum(jnp.where(low, 0.0, ss), axis=-1, keepdims=True)
    ms = jnp.where(low, s_lo, s_hi) * (1.0 / HEAD_DIM)
    yn = (y * lax.rsqrt(ms + RMS_EPS)) * gain
    if cos is None:
        return yn
    half = HEAD_DIM // 2
    fwd = pltpu.roll(yn, half, 1)
    bwd = pltpu.roll(yn, LANES - half, 1)
    rot = jnp.where((lane % HEAD_DIM) < half, bwd, fwd)
    return yn * cos + rot * sin


def _store_half_padded(lo_ref, hi_ref, c, z):
    lane = lax.broadcasted_iota(jnp.int32, z.shape, 1)
    low = lane < HEAD_DIM
    zr = pltpu.roll(z, HEAD_DIM, 1)
    dt = lo_ref.dtype
    a, b = 2 * c * LANES, (2 * c + 1) * LANES
    lo_ref[:, a:a + LANES] = jnp.where(low, z, 0.0).astype(dt)
    lo_ref[:, b:b + LANES] = jnp.where(low, zr, 0.0).astype(dt)
    hi_ref[:, a:a + LANES] = jnp.where(low, 0.0, zr).astype(dt)
    hi_ref[:, b:b + LANES] = jnp.where(low, 0.0, z).astype(dt)


def _row_specs(x, tm):
    def spec(cols, fn):
        return pl.BlockSpec((tm, cols), lambda i: (fn(i), 0))

    if not isinstance(x, tuple):
        return [spec(x.shape[1], lambda i: i)], 0
    head, tail = x
    assert head.shape[0] % tm == 0 and tail.shape[0] % tm == 0 and head.shape[1] == tail.shape[1]
    n_head = head.shape[0] // tm
    return [spec(head.shape[1], lambda i: jnp.minimum(i, n_head - 1)),
            spec(head.shape[1], lambda i: jnp.maximum(i - n_head, 0))], n_head


def _read_rows(x_refs, n_head):
    if len(x_refs) == 1:
        return x_refs[0][...]
    return jnp.where(pl.program_id(0) < n_head, x_refs[0][...], x_refs[1][...])


def _n_rows(x):
    return x[0].shape[0] + x[1].shape[0] if isinstance(x, tuple) else x.shape[0]


def _as_list(x):
    return list(x) if isinstance(x, tuple) else [x]


def _proj_kernel(*refs, nq, has_kv, rope, n_x, n_head):
    it = iter(refs)
    x_refs = [next(it) for _ in range(n_x)]
    g_ref, w_ref = next(it), next(it)
    gq_ref = next(it) if nq else None
    gk_ref = next(it) if has_kv else None
    cos_ref, sin_ref = (next(it), next(it)) if rope else (None, None)
    q_ref = next(it) if nq else None
    if has_kv:
        kl_ref, kh_ref, vl_ref, vh_ref, kf_ref, vf_ref = (next(it) for _ in range(6))
    h_scr = next(it)
    h_scr[...] = _rms_rows(_read_rows(x_refs, n_head), g_ref[...]).astype(h_scr.dtype)
    cos = cos_ref[...] if rope else None
    sin = sin_ref[...] if rope else None
    groups = COL_TILE // LANES
    for j in range(nq + (2 if has_kv else 0)):
        y = jnp.dot(h_scr[...], w_ref[j], preferred_element_type=jnp.float32)
        for c in range(groups):
            yc = y[:, c * LANES:(c + 1) * LANES]
            if j < nq:
                z = _head_norm(yc, gq_ref[...], cos, sin) * (HEAD_DIM ** -0.5)
                q_ref[:, (j * groups + c) * LANES:(j * groups + c + 1) * LANES] = z.astype(q_ref.dtype)
            elif j == nq:
                z = _head_norm(yc, gk_ref[...], cos, sin)
                kf_ref[:, c * LANES:(c + 1) * LANES] = z
                _store_half_padded(kl_ref, kh_ref, c, z)
            else:
                vf_ref[:, c * LANES:(c + 1) * LANES] = yc
                _store_half_padded(vl_ref, vh_ref, c, yc)


def _project(x, g, w, gq, gk, cos, sin, *, nq, has_kv, rope):
    t = _n_rows(x)
    d = w.shape[0]
    tm = ROW_TILE
    n_col = nq + (2 if has_kv else 0)
    assert w.shape == (d, n_col * COL_TILE) and t % tm == 0
    w = jnp.swapaxes(w.reshape(d, n_col, COL_TILE), 0, 1)
    row = lambda i: (i, 0)
    x_specs, n_head = _row_specs(x, tm)
    in_specs = x_specs + [pl.BlockSpec((1, d), lambda i: (0, 0)),
                          pl.BlockSpec((n_col, d, COL_TILE), lambda i: (0, 0, 0))]
    args = _as_list(x) + [g, w]
    vec = pl.BlockSpec((1, LANES), lambda i: (0, 0))
    if nq:
        in_specs.append(vec)
        args.append(gq)
    if has_kv:
        in_specs.append(vec)
        args.append(gk)
    if rope:
        in_specs += [pl.BlockSpec((tm, LANES), row)] * 2
        args += [cos, sin]
    out_shape, out_specs = [], []
    if nq:
        out_shape.append(jax.ShapeDtypeStruct((t, nq * COL_TILE), jnp.bfloat16))
        out_specs.append(pl.BlockSpec((tm, nq * COL_TILE), row))
    if has_kv:
        for _ in range(4):
            out_shape.append(jax.ShapeDtypeStruct((t, HEADS_PER_VREG * COL_TILE), jnp.bfloat16))
            out_specs.append(pl.BlockSpec((tm, HEADS_PER_VREG * COL_TILE), row))
        for _ in range(2):
            out_shape.append(jax.ShapeDtypeStruct((t, COL_TILE), jnp.float32))
            out_specs.append(pl.BlockSpec((tm, COL_TILE), row))
    return pl.pallas_call(
        functools.partial(_proj_kernel, nq=nq, has_kv=has_kv, rope=rope,
                          n_x=len(x_specs), n_head=n_head),
        grid=(t // tm,),
        in_specs=in_specs,
        out_specs=out_specs,
        out_shape=out_shape,
        scratch_shapes=[pltpu.VMEM((tm, d), jnp.bfloat16)],
        compiler_params=_params(("parallel",)),
        name="proj",
    )(*args)


def _attn_prompt_kernel(*refs, seq, band, n_prev, unroll, has_bias, has_sink):
    it = iter(refs)
    q_ref, kl_ref, kh_ref, vl_ref, vh_ref = (next(it) for _ in range(5))
    bias_ref = next(it) if has_bias else None
    sink_ref = next(it) if has_sink else None
    o_ref = next(it)
    s_scr = (next(it), next(it))
    p_scr = (next(it), next(it))
    n_scr = (next(it), next(it))
    row = lax.broadcasted_iota(jnp.int32, (band, LANES), 0)
    nt = (((1,), (1,)), ((), ()))
    tn = (((0,), (0,)), ((), ()))
    n_groups = seq // (CHUNK * unroll)
    n_masked = min(-(-n_prev // unroll), n_groups)

    def key_start(c):
        return pl.multiple_of(jnp.maximum(c - n_prev, 0) * CHUNK, CHUNK)

    def scores(gi, slot):
        for u in range(unroll):
            c = gi * unroll + u
            r0 = pl.multiple_of(c * CHUNK, CHUNK)
            k0 = key_start(c)
            qs = jnp.concatenate([q_ref[pl.ds(r0, CHUNK), p * LANES:(p + 1) * LANES]
                                  for p in range(PAIRS)], axis=0)
            for half, k_ref in enumerate((kl_ref, kh_ref)):
                s_scr[slot][u, half] = lax.dot_general(k_ref[pl.ds(k0, band), :], qs, nt,
                                                        preferred_element_type=jnp.float32)

    def softmax(gi, slot, masked):
        for u in range(unroll):
            c = gi * unroll + u
            shift = pl.multiple_of(jnp.maximum(n_prev - c, 0) * CHUNK, CHUNK) if masked else 0
            inv = []
            for half in range(HEADS_PER_VREG):
                s = s_scr[slot][u, half]
                if has_bias:
                    s = s + bias_ref[half, pl.ds(shift, band), :]
                if masked:
                    s = jnp.where(row < band - shift, s, NEG_INF)
                m = jnp.max(s, axis=0, keepdims=True)
                if has_sink:
                    m = jnp.maximum(m, sink_ref[half])
                e = jnp.exp(s - m)
                l = jnp.sum(e, axis=0, keepdims=True)
                if has_sink:
                    l = l + jnp.exp(sink_ref[half] - m)
                p_scr[slot][u, half] = e.astype(p_scr[slot].dtype)
                inv.append(jnp.broadcast_to(1.0 / l, (LANES, LANES)))
            n_scr[slot][u] = jnp.where(row[:LANES] < HEAD_DIM, inv[0], inv[1])

    def outputs(gi, slot):
        for u in range(unroll):
            c = gi * unroll + u
            r0 = pl.multiple_of(c * CHUNK, CHUNK)
            k0 = key_start(c)
            o = (lax.dot_general(p_scr[slot][u, 0], vl_ref[pl.ds(k0, band), :], tn,
                                 preferred_element_type=jnp.float32)
                 + lax.dot_general(p_scr[slot][u, 1], vh_ref[pl.ds(k0, band), :], tn,
                                   preferred_element_type=jnp.float32))
            o = o * n_scr[slot][u].T
            for p in range(PAIRS):
                o_ref[pl.ds(r0, CHUNK), p * LANES:(p + 1) * LANES] = (
                    o[p * CHUNK:(p + 1) * CHUNK].astype(o_ref.dtype))

    def stage(g, parity, masked):
        static = isinstance(g, int)
        if not static or g < n_groups:
            scores(g, parity)
        if not static or 0 <= g - 1 < n_groups:
            softmax(g - 1, 1 - parity, masked)
        if not static or 0 <= g - 2 < n_groups:
            outputs(g - 2, parity)

    first_loop = max(n_masked + 1, 2)
    first_loop += first_loop % 2
    n_pairs = max(n_groups - first_loop, 0) // 2
    for g in range(min(first_loop, n_groups)):
        stage(g, g % 2, masked=True)
    if n_pairs:
        def body(k, carry):
            g = first_loop + 2 * k
            stage(g, 0, masked=False)
            stage(g + 1, 1, masked=False)
            return carry
        lax.fori_loop(0, n_pairs, body, 0)
    for g in range(min(first_loop, n_groups) + 2 * n_pairs, n_groups + 2):
        stage(g, g % 2, masked=(g - 1) < n_masked)


def _attend_prompt(q, kl, kh, vl, vh, bias, sink, *, nb, seq, n_prev):
    qw = q.shape[1]
    n_kv = kl.shape[1] // LANES
    band = (n_prev + 1) * CHUNK
    unroll = ATTN_UNROLL
    assert seq % (CHUNK * unroll) == 0 and seq >= band and seq >= 2 * CHUNK * unroll
    kv_spec = pl.BlockSpec((seq, LANES), lambda b, j: (b, j))
    in_specs = [pl.BlockSpec((seq, KV_GROUP_WIDTH), lambda b, j: (b, j))] + [kv_spec] * 4
    args = [q, kl, kh, vl, vh]
    if bias is not None:
        in_specs.append(pl.BlockSpec((HEADS_PER_VREG,) + bias.shape[1:], lambda b, j: (j, 0, 0)))
        args.append(bias)
    if sink is not None:
        in_specs.append(pl.BlockSpec((HEADS_PER_VREG, 1, LANES), lambda b, j: (j, 0, 0)))
        args.append(sink)
    return pl.pallas_call(
        functools.partial(_attn_prompt_kernel, seq=seq, band=band, n_prev=n_prev, unroll=unroll,
                          has_bias=bias is not None, has_sink=sink is not None),
        grid=(nb, n_kv),
        in_specs=in_specs,
        out_specs=pl.BlockSpec((seq, KV_GROUP_WIDTH), lambda b, j: (b, j)),
        out_shape=jax.ShapeDtypeStruct((nb * seq, qw), jnp.bfloat16),
        scratch_shapes=[pltpu.VMEM((unroll, HEADS_PER_VREG, band, LANES), jnp.float32)] * 2
        + [pltpu.VMEM((unroll, HEADS_PER_VREG, band, LANES), jnp.bfloat16)] * 2
        + [pltpu.VMEM((unroll, LANES, LANES), jnp.float32)] * 2,
        compiler_params=_params(("parallel", "arbitrary")),
        name="attn_prompt",
    )(*args)


def _attn_sample_kernel(*refs, lq, has_bias, has_sink, n_kv):
    it = iter(refs)
    q_ref, k_ref, v_ref = next(it), next(it), next(it)
    bias_ref = next(it) if has_bias else None
    sink_ref = next(it) if has_sink else None
    o_ref = next(it)
    for j in range(n_kv):
        qj = q_ref[:, j * KV_GROUP_WIDTH:(j + 1) * KV_GROUP_WIDTH]
        qcat = jnp.concatenate(
            [qj[:, g * HEAD_DIM:(g + 1) * HEAD_DIM] for g in range(GROUP)], axis=0)
        kj = k_ref[0, :, j * HEAD_DIM:(j + 1) * HEAD_DIM]
        vj = v_ref[0, :, j * HEAD_DIM:(j + 1) * HEAD_DIM]
        s = lax.dot_general(qcat, kj, (((1,), (1,)), ((), ())),
                            preferred_element_type=jnp.float32)
        if has_bias:
            s = s + bias_ref[j]
        m = jnp.max(s, axis=-1, keepdims=True)
        if has_sink:
            sink = sink_ref[j]
            m = jnp.maximum(m, sink)
        e = jnp.exp(s - m)
        l = jnp.sum(e, axis=-1, keepdims=True)
        if has_sink:
            l = l + jnp.exp(sink - m)
        p = (e * (1.0 / l)).astype(vj.dtype)
        o = jnp.dot(p, vj, preferred_element_type=jnp.float32)
        ocat = jnp.concatenate([o[g * lq:(g + 1) * lq] for g in range(GROUP)], axis=1)
        o_ref[:, j * KV_GROUP_WIDTH:(j + 1) * KV_GROUP_WIDTH] = ocat.astype(o_ref.dtype)


def _attend_sample(q, k, v, bias, sink, *, row0, lq):
    ns, lk, kw = k.shape
    qw = q.shape[1]
    assert row0 % lq == 0
    blk0 = row0 // lq
    in_specs = [pl.BlockSpec((lq, qw), lambda i: (blk0 + i, 0)),
                pl.BlockSpec((1, lk, kw), lambda i: (i, 0, 0)),
                pl.BlockSpec((1, lk, kw), lambda i: (i, 0, 0))]
    args = [q, k, v]
    if bias is not None:
        in_specs.append(pl.BlockSpec(bias.shape, lambda i: (0, 0, 0)))
        args.append(bias)
    if sink is not None:
        in_specs.append(pl.BlockSpec(sink.shape, lambda i: (0, 0, 0)))
        args.append(sink)
    return pl.pallas_call(
        functools.partial(_attn_sample_kernel, lq=lq, has_bias=bias is not None,
                          has_sink=sink is not None, n_kv=kw // HEAD_DIM),
        grid=(ns,),
        in_specs=in_specs,
        out_specs=pl.BlockSpec((lq, qw), lambda i: (i, 0)),
        out_shape=jax.ShapeDtypeStruct((ns * lq, qw), jnp.bfloat16),
        compiler_params=_params(("arbitrary",)),
        name="attn_sample",
    )(*args)


def _out_proj_kernel(*refs, n_o, o_head, n_x, x_head):
    o_refs, w_ref = refs[:n_o], refs[n_o]
    x_refs, y_ref = refs[n_o + 1:n_o + 1 + n_x], refs[n_o + 1 + n_x]
    y_ref[...] = _read_rows(x_refs, x_head) + jnp.dot(
        _read_rows(o_refs, o_head), w_ref[...], preferred_element_type=jnp.float32)


def _out_proj(o, w, x):
    t = _n_rows(x)
    d = w.shape[1]
    tm = ROW_TILE
    o_specs, o_head = _row_specs(o, tm)
    x_specs, x_head = _row_specs(x, tm)
    return pl.pallas_call(
        functools.partial(_out_proj_kernel, n_o=len(o_specs), o_head=o_head,
                          n_x=len(x_specs), x_head=x_head),
        grid=(t // tm,),
        in_specs=o_specs + [pl.BlockSpec(w.shape, lambda i: (0, 0))] + x_specs,
        out_specs=pl.BlockSpec((tm, d), lambda i: (i, 0)),
        out_shape=jax.ShapeDtypeStruct((t, d), jnp.float32),
        compiler_params=_params(("parallel",)),
        name="out_proj",
    )(*_as_list(o), w, *_as_list(x))


def _swiglu_stage(h_scr, act_ref, wg_ref, wu_ref):
    h = h_scr[...]
    gate = jnp.dot(h, wg_ref[...], preferred_element_type=jnp.float32)
    up = jnp.dot(h, wu_ref[...], preferred_element_type=jnp.float32)
    act_ref[...] = ((gate * jax.nn.sigmoid(gate)) * up).astype(act_ref.dtype)


def _ffn_kernel(e_ref, n_ref, x_ref, g_ref, wg_ref, wu_ref, wd_ref, y_ref, h_scr, acc_scr,
                act0, act1, *, n_f):
    s = pl.program_id(1)
    act = (act0, act1)

    @pl.when(pl.program_id(0) < n_ref[0])
    def _():
        @pl.when(s == 0)
        def _():
            h_scr[...] = _rms_rows(x_ref[...], g_ref[...]).astype(h_scr.dtype)
            acc_scr[...] = jnp.zeros_like(acc_scr)
            _swiglu_stage(h_scr, act[0], wg_ref, wu_ref)

        for parity in range(2):
            @pl.when((s > 0) & (s < n_f) & (s % 2 == parity))
            def _():
                acc_scr[...] += jnp.dot(act[1 - parity][...], wd_ref[...],
                                        preferred_element_type=jnp.float32)
                _swiglu_stage(h_scr, act[parity], wg_ref, wu_ref)

        @pl.when(s == n_f)
        def _():
            part = jnp.dot(act[(n_f - 1) % 2][...], wd_ref[...],
                           preferred_element_type=jnp.float32)
            y_ref[...] = x_ref[...] + (acc_scr[...] + part)

    @pl.when((pl.program_id(0) >= n_ref[0]) & (s == n_f))
    def _():
        y_ref[...] = jnp.zeros_like(y_ref)


def _swiglu_blocks(x, g, wg, wu, wd, layer, block_expert, n_used, *, tm, tf):
    r, d = x.shape
    ff = wg.shape[3]
    assert r % tm == 0 and ff % tf == 0
    n_f = ff // tf

    def blk(i, n):
        return jnp.minimum(i, n[0] - 1)

    def up_col(i, s, n):
        return jnp.where(i < n[0], jnp.minimum(s, n_f - 1), n_f - 1)

    def down_row(i, s, n):
        return jnp.where(i < n[0], jnp.maximum(s - 1, 0), n_f - 1)

    grid_spec = pltpu.PrefetchScalarGridSpec(
        num_scalar_prefetch=2,
        grid=(r // tm, n_f + 1),
        in_specs=[pl.BlockSpec((tm, d), lambda i, s, e, n: (blk(i, n), 0)),
                  pl.BlockSpec((1, d), lambda i, s, e, n: (0, 0)),
                  pl.BlockSpec((None, None, d, tf),
                               lambda i, s, e, n: (layer, e[blk(i, n)], 0, up_col(i, s, n))),
                  pl.BlockSpec((None, None, d, tf),
                               lambda i, s, e, n: (layer, e[blk(i, n)], 0, up_col(i, s, n))),
                  pl.BlockSpec((None, None, tf, d),
                               lambda i, s, e, n: (layer, e[blk(i, n)], down_row(i, s, n), 0))],
        out_specs=pl.BlockSpec((tm, d), lambda i, s, e, n: (i, 0)),
        scratch_shapes=[pltpu.VMEM((tm, d), jnp.bfloat16),
                        pltpu.VMEM((tm, d), jnp.float32),
                        pltpu.VMEM((tm, tf), jnp.bfloat16),
                        pltpu.VMEM((tm, tf), jnp.bfloat16)],
    )
    return pl.pallas_call(
        functools.partial(_ffn_kernel, n_f=n_f),
        grid_spec=grid_spec,
        out_shape=jax.ShapeDtypeStruct((r, d), jnp.float32),
        compiler_params=_params(("arbitrary", "arbitrary")),
        name="swiglu",
    )(block_expert, n_used, x, g, wg, wu, wd)


def _router_kernel(x_ref, g_ref, w_ref, b_ref, idx_ref, gate_ref, h_ref):
    h = _rms_rows(x_ref[...], g_ref[...]).astype(jnp.bfloat16)
    h_ref[...] = h.reshape(h_ref.shape)
    logits = jnp.dot(h, w_ref[...], preferred_element_type=jnp.float32) + b_ref[...]
    lane = lax.broadcasted_iota(jnp.int32, logits.shape, 1)
    m1 = jnp.max(logits, axis=-1, keepdims=True)
    e1 = jnp.min(jnp.where(logits == m1, lane, LANES), axis=-1, keepdims=True)
    rest = jnp.where(lane == e1, -jnp.inf, logits)
    m2 = jnp.max(rest, axis=-1, keepdims=True)
    e2 = jnp.min(jnp.where(rest == m2, lane, LANES), axis=-1, keepdims=True)
    t = jnp.exp(m2 - m1)
    denom = 1.0 + t
    idx_ref[...] = jnp.where(lane == 0, e1, jnp.where(lane == 1, e2, 0))
    gate_ref[...] = jnp.where(lane == 0, 1.0 / denom, jnp.where(lane == 1, t / denom, 0.0))


def _route(x, g, w_router, b_router):
    t, d = x.shape
    tm = ROW_TILE
    n_e = w_router.shape[1]
    w = jnp.zeros((d, LANES), jnp.bfloat16).at[:, :n_e].set(w_router.astype(jnp.bfloat16))
    b = jnp.full((1, LANES), -jnp.inf, jnp.float32).at[0, :n_e].set(b_router.astype(jnp.float32))
    return pl.pallas_call(
        _router_kernel,
        grid=(t // tm,),
        in_specs=[pl.BlockSpec((tm, d), lambda i: (i, 0)),
                  pl.BlockSpec((1, d), lambda i: (0, 0)),
                  pl.BlockSpec((d, LANES), lambda i: (0, 0)),
                  pl.BlockSpec((1, LANES), lambda i: (0, 0))],
        out_specs=[pl.BlockSpec((tm, LANES), lambda i: (i, 0))] * 2
        + [pl.BlockSpec((tm, d // LANES, LANES), lambda i: (i, 0, 0))],
        out_shape=[jax.ShapeDtypeStruct((t, LANES), jnp.int32),
                   jax.ShapeDtypeStruct((t, LANES), jnp.float32),
                   jax.ShapeDtypeStruct((t, d // LANES, LANES), jnp.bfloat16)],
        compiler_params=_params(("parallel",)),
        name="router",
    )(x, g, w, b)


def _experts_kernel(e_ref, n_ref, first_ref, next_ref, prev_dst_ref, dst_ref, h_hbm,
                    wg_ref, wu_ref, wd_ref, y_hbm, xbuf, ybuf, h_scr, acc_scr, act0, act1,
                    gsem, ssem, *, tm, n_f):
    i, s = pl.program_id(0), pl.program_id(1)
    act = (act0, act1)
    n_used = n_ref[0]
    slot = i % 2
    other = 1 - slot
    share = tm // n_f
    unroll = 2 * DMA_THREADS

    def gather_copy(idx_ref, r, to_slot):
        return pltpu.make_async_copy(h_hbm.at[idx_ref[r]], xbuf.at[to_slot, r], gsem.at[to_slot])

    def scatter_copy(idx_ref, r, from_slot):
        return pltpu.make_async_copy(ybuf.at[from_slot, r], y_hbm.at[idx_ref[r]],
                                     ssem.at[from_slot])

    def issue_all(copy_of_row):
        def issue(r4, carry):
            for k in range(unroll):
                copy_of_row(r4 * unroll + k).start(priority=k % DMA_THREADS)
            return carry
        lax.fori_loop(0, tm // unroll, issue, 0)

    def wait_gather(s_):
        pltpu.make_async_copy(h_hbm.at[pl.ds(0, tm)], xbuf.at[s_], gsem.at[s_]).wait()

    def wait_scatter(s_):
        pltpu.make_async_copy(ybuf.at[s_], y_hbm.at[pl.ds(0, tm)], ssem.at[s_]).wait()

    def row_copies(step):
        for k in range(share):
            r = step * share + k
            gather_copy(next_ref, r, other).start(priority=k % DMA_THREADS)
            scatter_copy(prev_dst_ref, r, other).start(priority=(k + 1) % DMA_THREADS)

    @pl.when(i < n_used)
    def _():
        @pl.when(s == 0)
        def _():
            @pl.when(i == 0)
            def _():
                issue_all(lambda r: gather_copy(first_ref, r, 0))
                ybuf[1] = jnp.zeros(ybuf.shape[1:], ybuf.dtype)
                for half in range(2):
                    spare = pltpu.make_async_copy(
                        ybuf.at[1], y_hbm.at[pl.ds(y_hbm.shape[0] - (3 - half) * tm, tm)],
                        ssem.at[1])
                    spare.start()
                    spare.wait()

            wait_gather(slot)
            h_scr[...] = xbuf[slot].reshape(h_scr.shape)
            acc_scr[...] = jnp.zeros_like(acc_scr)
            row_copies(0)
            _swiglu_stage(h_scr, act[0], wg_ref, wu_ref)

        for parity in range(2):
            @pl.when((s > 0) & (s < n_f) & (s % 2 == parity))
            def _():
                row_copies(s)
                acc_scr[...] += jnp.dot(act[1 - parity][...], wd_ref[...],
                                        preferred_element_type=jnp.float32)
                _swiglu_stage(h_scr, act[parity], wg_ref, wu_ref)

        @pl.when(s == n_f)
        def _():
            part = jnp.dot(act[(n_f - 1) % 2][...], wd_ref[...],
                           preferred_element_type=jnp.float32)

            @pl.when(i >= 1)
            def _():
                wait_scatter(slot)

            ybuf[slot] = (acc_scr[...] + part).reshape(ybuf.shape[1:])

            @pl.when(i == n_used - 1)
            def _():
                issue_all(lambda r: scatter_copy(dst_ref, r, slot))
                wait_scatter(slot)
                wait_scatter(other)
                wait_gather(other)


def _experts(h_rows, wg, wu, wd, layer, block_expert, n_used, row_tok, row_dst, n_out, *, tm, tf):
    tile = h_rows.shape[1:]
    d = wg.shape[2]
    ff = wg.shape[3]
    rows = row_tok.shape[0]
    assert rows % tm == 0 and row_dst.shape[0] == rows + tm and ff % tf == 0
    n_f = ff // tf
    assert tm % (n_f * 2 * DMA_THREADS) == 0

    def blk(i, n):
        return jnp.minimum(i, n[0] - 1)

    def up_col(i, s, n):
        return jnp.where(i < n[0], jnp.minimum(s, n_f - 1), n_f - 1)

    def down_row(i, s, n):
        return jnp.where(i < n[0], jnp.maximum(s - 1, 0), n_f - 1)

    smem = functools.partial(pl.BlockSpec, memory_space=pltpu.SMEM)
    grid_spec = pltpu.PrefetchScalarGridSpec(
        num_scalar_prefetch=2,
        grid=(rows // tm, n_f + 1),
        in_specs=[smem((tm,), lambda i, f, e, n: (0,)),
                  smem((tm,), lambda i, f, e, n: (blk(i + 1, n),)),
                  smem((tm,), lambda i, f, e, n: (blk(i, n),)),
                  smem((tm,), lambda i, f, e, n: (blk(i, n) + 1,)),
                  pl.BlockSpec(memory_space=pl.ANY),
                  pl.BlockSpec((None, None, d, tf),
                               lambda i, f, e, n: (layer, e[blk(i, n)], 0, up_col(i, f, n))),
                  pl.BlockSpec((None, None, d, tf),
                               lambda i, f, e, n: (layer, e[blk(i, n)], 0, up_col(i, f, n))),
                  pl.BlockSpec((None, None, tf, d),
                               lambda i, f, e, n: (layer, e[blk(i, n)], down_row(i, f, n), 0))],
        out_specs=pl.BlockSpec(memory_space=pl.ANY),
        scratch_shapes=[pltpu.VMEM((2, tm) + tile, jnp.bfloat16),
                        pltpu.VMEM((2, tm) + tile, jnp.float32),
                        pltpu.VMEM((tm, d), jnp.bfloat16),
                        pltpu.VMEM((tm, d), jnp.float32),
                        pltpu.VMEM((tm, tf), jnp.bfloat16),
                        pltpu.VMEM((tm, tf), jnp.bfloat16),
                        pltpu.SemaphoreType.DMA((2,)),
                        pltpu.SemaphoreType.DMA((2,))],
    )
    return pl.pallas_call(
        functools.partial(_experts_kernel, tm=tm, n_f=n_f),
        grid_spec=grid_spec,
        out_shape=jax.ShapeDtypeStruct((n_out,) + tile, jnp.float32),
        compiler_params=_params(("arbitrary", "arbitrary")),
        name="experts",
    )(block_expert, n_used, row_tok, row_tok, row_dst, row_dst, h_rows, wg, wu, wd)


def _mix_kernel(x_ref, gate_ref, y0_ref, y1_ref, *out_refs, n_head):
    gates = gate_ref[...]
    shape = x_ref.shape
    y = gates[:, 0:1] * y0_ref[...].reshape(shape) + gates[:, 1:2] * y1_ref[...].reshape(shape)
    out = x_ref[...] + y
    if len(out_refs) == 1:
        out_refs[0][...] = out
    else:
        @pl.when(pl.program_id(0) < n_head)
        def _():
            out_refs[0][...] = out

        @pl.when(pl.program_id(0) >= n_head)
        def _():
            out_refs[1][...] = out


def _mix(x, gates, y, split_at=None):
    t, d = x.shape
    rows = COMBINE_ROWS
    tile = y.shape[1:]
    assert t % rows == 0
    n_t = t // rows
    if split_at is None:
        n_head = 0
        out_specs = pl.BlockSpec((rows, d), lambda i: (i, 0))
        out_shape = jax.ShapeDtypeStruct((t, d), jnp.float32)
    else:
        assert split_at % rows == 0 and 0 < split_at < t
        n_head = split_at // rows
        out_specs = [pl.BlockSpec((rows, d), lambda i: (jnp.minimum(i, n_head - 1), 0)),
                     pl.BlockSpec((rows, d), lambda i: (jnp.maximum(i - n_head, 0), 0))]
        out_shape = [jax.ShapeDtypeStruct((split_at, d), jnp.float32),
                     jax.ShapeDtypeStruct((t - split_at, d), jnp.float32)]
    return pl.pallas_call(
        functools.partial(_mix_kernel, n_head=n_head),
        grid=(n_t,),
        in_specs=[pl.BlockSpec((rows, d), lambda i: (i, 0)),
                  pl.BlockSpec((rows, LANES), lambda i: (i, 0)),
                  pl.BlockSpec((rows,) + tile, lambda i: (i, 0, 0)),
                  pl.BlockSpec((rows,) + tile, lambda i: (n_t + i, 0, 0))],
        out_specs=out_specs,
        out_shape=out_shape,
        compiler_params=_params(("arbitrary",)),
        name="mix",
    )(x, gates, y, y)


def _moe(x, g, w_router, b_router, wg, wu, wd, layer, split_at=None):
    t, d = x.shape
    n_e = wg.shape[1]
    n_assign = t * TOP_K
    top_idx, gates, h_rows = _route(x, g, w_router, b_router)
    flat_e = top_idx[:, :TOP_K].reshape(-1)
    onehot = (flat_e[:, None] == jnp.arange(n_e, dtype=jnp.int32)[None, :]).astype(jnp.int32)
    rank = jnp.sum((jnp.cumsum(onehot, axis=0) - onehot) * onehot, axis=1)
    counts = jnp.sum(onehot, axis=0)
    padded = (counts + MOE_ROWS - 1) // MOE_ROWS * MOE_ROWS
    ends = jnp.cumsum(padded)
    dest = jnp.sum((ends - padded)[None, :] * onehot, axis=1) + rank
    n_blocks = (n_assign + n_e * (MOE_ROWS - 1) + MOE_ROWS - 1) // MOE_ROWS
    rows = n_blocks * MOE_ROWS
    row_assign = jnp.full((rows,), -1, jnp.int32).at[dest].set(jnp.arange(n_assign, dtype=jnp.int32))
    real = row_assign >= 0
    row_tok = jnp.where(real, row_assign // TOP_K, 0)
    spare = n_assign + (jnp.arange(rows, dtype=jnp.int32) // MOE_ROWS % 2) * MOE_ROWS \
        + jnp.arange(rows, dtype=jnp.int32) % MOE_ROWS
    row_dst = jnp.where(real, (row_assign % TOP_K) * t + row_assign // TOP_K, spare)
    row_dst = jnp.concatenate(
        [n_assign + 2 * MOE_ROWS + jnp.arange(MOE_ROWS, dtype=jnp.int32), row_dst])
    block_start = jnp.arange(n_blocks, dtype=jnp.int32) * MOE_ROWS
    block_expert = jnp.minimum(
        jnp.sum((ends[None, :] <= block_start[:, None]).astype(jnp.int32), axis=1), n_e - 1)
    n_used = (ends[-1:] // MOE_ROWS).astype(jnp.int32)
    y = _experts(h_rows, wg, wu, wd, layer, block_expert, n_used, row_tok, row_dst,
                 n_assign + 3 * MOE_ROWS, tm=MOE_ROWS, tf=MOE_COL_TILE)
    return _mix(x, gates, y, split_at)


def _rope_tables(pos):
    half = HEAD_DIM // 2
    inv_freq = 1.0 / (ROPE_THETA ** (jnp.arange(0, HEAD_DIM, 2, dtype=jnp.float32) / HEAD_DIM))
    ang = pos.astype(jnp.float32)[:, None] * inv_freq[None, :]
    cos, sin = jnp.cos(ang), jnp.sin(ang)
    reps = LANES // half
    signs = jnp.tile(jnp.concatenate([-jnp.ones((half,)), jnp.ones((half,))]), HEADS_PER_VREG)
    return jnp.tile(cos, (1, reps)), jnp.tile(sin, (1, reps)) * signs[None, :].astype(jnp.float32)


def _rel_bias(table, lq, n_before):
    n_keys = n_before + lq
    rel = n_before + (lq - 1) - jnp.arange(n_keys + lq - 1)
    diag = table.astype(jnp.float32)[:, jnp.clip(rel, REL_MIN, REL_MAX) - REL_MIN]
    return jnp.stack([diag[:, lq - 1 - q:lq - 1 - q + n_keys] for q in range(lq)], axis=1)


def _pair_major(a):
    h = a.shape[0]
    a = a.reshape((h // GROUP, PAIRS, HEADS_PER_VREG) + a.shape[1:])
    return jnp.swapaxes(a, 1, 2)


def _prompt_bias(table, n_prev):
    pad = n_prev * CHUNK
    b = _pair_major(_rel_bias(table, CHUNK, pad))
    kvh = b.shape[0]
    b = jnp.transpose(b, (0, 1, 4, 2, 3)).reshape(kvh * HEADS_PER_VREG, pad + CHUNK, LANES)
    return jnp.pad(b, ((0, 0), (0, pad), (0, 0)))


def _prompt_sink(sink):
    s = _pair_major(sink.astype(jnp.float32))
    s = jnp.repeat(s[..., None], CHUNK, axis=-1)
    return s.reshape(-1, 1, LANES)


def _sample_bias(table, lq, n_cached):
    b = _rel_bias(table, lq, n_cached)
    return b.reshape(b.shape[0] // GROUP, GROUP * lq, n_cached + lq)


def _sample_sink(sink, lq):
    h = sink.shape[0]
    s = jnp.repeat(sink.astype(jnp.float32).reshape(h // GROUP, GROUP, 1), lq, axis=2)
    return s.reshape(h // GROUP, GROUP * lq, 1)


def _head_gain(g):
    return jnp.tile(g.astype(jnp.float32), HEADS_PER_VREG)[None, :]


def kernel(x_prompt, x_sample, cache_k_a, cache_v_a, cache_k_b, cache_v_b, g_attn, g_ffn, w_qkv_a, g_q_a, g_k_a, sink_a, w_o_a, g_kv_b, w_kv_b, g_k_b, w_q_b, g_q_b, rel_bias_b, w_o_b, w_gate_ffn, w_up_ffn, w_down_ffn, w_router, b_router, w_gate_moe, w_up_moe, w_down_moe):
    bf16 = jnp.bfloat16
    nb, seq, d = x_prompt.shape
    ns, seq_s, _ = x_sample.shape
    tp, ts = nb * seq, ns * seq_s
    t = tp + ts
    depth = g_attn.shape[0]
    n_a = w_qkv_a.shape[0]
    n_kv = cache_k_b.shape[2]
    kvw = n_kv * HEAD_DIM
    keep_a, keep_b = min(WINDOW_A, seq), min(BAND_B, seq)
    assert t % ROW_TILE == 0

    x = (x_prompt.reshape(tp, d), x_sample.reshape(ts, d))
    pos = jnp.concatenate([jnp.tile(jnp.arange(seq), nb),
                           jnp.tile(PAST_LEN + jnp.arange(seq_s), ns)])
    cos, sin = _rope_tables(pos)
    row = lambda v: v.astype(jnp.float32)[None, :]

    def attend(q, kv_pad, k_f32, v_f32, cache_k, cache_v, n_prev, table, sink):
        bias_p = None if table is None else _prompt_bias(table, n_prev)
        sink_p = None if sink is None else _prompt_sink(sink)
        o_p = _attend_prompt(q, *kv_pad, bias_p, sink_p, nb=nb, seq=seq, n_prev=n_prev)
        n_cached = cache_k.shape[1]
        ks = jnp.concatenate([cache_k.reshape(ns, n_cached, kvw),
                              k_f32[tp:].reshape(ns, seq_s, kvw)], axis=1).astype(bf16)
        vs = jnp.concatenate([cache_v.reshape(ns, n_cached, kvw),
                              v_f32[tp:].reshape(ns, seq_s, kvw)], axis=1).astype(bf16)
        bias_s = None if table is None else _sample_bias(table, seq_s, n_cached)
        sink_s = None if sink is None else _sample_sink(sink, seq_s)
        return o_p, _attend_sample(q, ks, vs, bias_s, sink_s, row0=tp, lq=seq_s)

    def tail(c, n_keep):
        rows = jnp.stack([c[(b + 1) * seq - n_keep:(b + 1) * seq] for b in range(nb)])
        return rows.reshape(nb, n_keep, n_kv, HEAD_DIM)

    def rolled(cache, c):
        new = c[tp:].reshape(ns, seq_s, n_kv, HEAD_DIM)
        return jnp.concatenate([cache, new], axis=1)[:, seq_s:]

    ffn_w = [w.astype(bf16)[:, None] for w in (w_gate_ffn, w_up_ffn, w_down_ffn)]
    moe_w = [w.astype(bf16) for w in (w_gate_moe, w_up_moe, w_down_moe)]

    ka_p, va_p, ka_s, va_s = [], [], [], []
    kvb_pad = kb_f = vb_f = None
    for layer in range(depth):
        g_in = row(g_attn[layer])
        if layer < n_a:
            i = layer
            q, kl, kh, vl, vh, k_f32, v_f32 = _project(
                x, g_in, w_qkv_a[i].astype(bf16), _head_gain(g_q_a[i]), _head_gain(g_k_a[i]),
                cos, sin, nq=(w_qkv_a.shape[2] - 2 * kvw) // COL_TILE, has_kv=True, rope=True)
            o = attend(q, (kl, kh, vl, vh), k_f32, v_f32, cache_k_a[i], cache_v_a[i],
                       PREV_CHUNKS_A, None, sink_a[i])
            x = _out_proj(o, w_o_a[i].astype(bf16), x)
            ka_p.append(tail(k_f32, keep_a))
            va_p.append(tail(v_f32, keep_a))
            ka_s.append(rolled(cache_k_a[i], k_f32))
            va_s.append(rolled(cache_v_a[i], v_f32))
        else:
            jb = layer - n_a
            (q,) = _project(x, g_in, w_q_b[jb].astype(bf16), _head_gain(g_q_b[jb]), None,
                            None, None, nq=w_q_b.shape[2] // COL_TILE, has_kv=False, rope=False)
            o = attend(q, kvb_pad, kb_f, vb_f, cache_k_b, cache_v_b, PREV_CHUNKS_B,
                       rel_bias_b[jb], None)
            x = _out_proj(o, w_o_b[jb].astype(bf16), x)

        g_mid = row(g_ffn[layer])
        m = layer // 2
        if layer % 2 == 0:
            n_tiles = t // ROW_TILE
            x = _swiglu_blocks(x, g_mid, *ffn_w, m, jnp.zeros((n_tiles,), jnp.int32),
                               jnp.full((1,), n_tiles, jnp.int32), tm=ROW_TILE, tf=COL_TILE)
        else:
            x = _moe(x, g_mid, w_router[m], b_router[m], *moe_w, m,
                     split_at=tp if layer == depth - 1 else None)

        if layer == n_a - 1:
            *kvb_pad, kb_f, vb_f = _project(x, row(g_kv_b), w_kv_b.astype(bf16), None,
                                            _head_gain(g_k_b), None, None,
                                            nq=0, has_kv=True, rope=False)

    if not isinstance(x, tuple):
        x = (x[:tp], x[tp:])
    y_prompt = x[0].reshape(nb, seq, d)
    y_sample = x[1].reshape(ns, seq_s, d)
    return (y_prompt, y_sample, jnp.stack(ka_p), jnp.stack(va_p), tail(kb_f, keep_b),
            tail(vb_f, keep_b), jnp.stack(ka_s), jnp.stack(va_s),
            rolled(cache_k_b, kb_f), rolled(cache_v_b, vb_f))
```

```python
import functools

import jax
import jax.numpy as jnp
from jax import lax
from jax.experimental import pallas as pl
from jax.experimental.pallas import tpu as pltpu

CHUNK = 64
HEAD_DIM = 64
GROUP = 4
WINDOW_A = 128
PREV_CHUNKS_A = WINDOW_A // CHUNK
PREV_CHUNKS_B = 8
BAND_B = PREV_CHUNKS_B * CHUNK
REL_MIN = -(CHUNK - 1)
REL_MAX = 256
ROPE_THETA = 10000.0
TOP_K = 2
RMS_EPS = 1e-6
NEG_INF = -1e30
PAST_LEN = 1024

LANES = 128
HEADS_PER_VREG = LANES // HEAD_DIM
KV_GROUP_WIDTH = GROUP * HEAD_DIM
PAIRS = GROUP // HEADS_PER_VREG

ROW_TILE = 512
COL_TILE = 512
MOE_ROWS = 512
MOE_COL_TILE = 1024
COMBINE_ROWS = 256
ATTN_UNROLL = 4
DMA_THREADS = 2
VMEM_LIMIT = 56 * 1024 * 1024


def _params(semantics):
    return pltpu.CompilerParams(dimension_semantics=semantics, vmem_limit_bytes=VMEM_LIMIT)


def _rms_rows(x, g):
    ms = jnp.sum(x * x, axis=-1, keepdims=True) * (1.0 / x.shape[-1])
    return (x * lax.rsqrt(ms + RMS_EPS)) * g


def _head_norm(y, gain, cos, sin):
    lane = lax.broadcasted_iota(jnp.int32, y.shape, 1)
    low = lane < HEAD_DIM
    ss = y * y
    s_lo = jnp.sum(jnp.where(low, ss, 0.0), axis=-1, keepdims=True)
    s_hi = jnp.sum(jnp.where(low, 0.0, ss), axis=-1, keepdims=True)
    ms = jnp.where(low, s_lo, s_hi) * (1.0 / HEAD_DIM)
    yn = (y * lax.rsqrt(ms + RMS_EPS)) * gain
    if cos is None:
        return yn
    half = HEAD_DIM // 2
    fwd = pltpu.roll(yn, half, 1)
    bwd = pltpu.roll(yn, LANES - half, 1)
    rot = jnp.where((lane % HEAD_DIM) < half, bwd, fwd)
    return yn * cos + rot * sin


def _store_half_padded(lo_ref, hi_ref, c, z):
    lane = lax.broadcasted_iota(jnp.int32, z.shape, 1)
    low = lane < HEAD_DIM
    zr = pltpu.roll(z, HEAD_DIM, 1)
    dt = lo_ref.dtype
    a, b = 2 * c * LANES, (2 * c + 1) * LANES
    lo_ref[:, a:a + LANES] = jnp.where(low, z, 0.0).astype(dt)
    lo_ref[:, b:b + LANES] = jnp.where(low, zr, 0.0).astype(dt)
    hi_ref[:, a:a + LANES] = jnp.where(low, 0.0, zr).astype(dt)
    hi_ref[:, b:b + LANES] = jnp.where(low, 0.0, z).astype(dt)


def _row_specs(x, tm):
    def spec(cols, fn):
        return pl.BlockSpec((tm, cols), lambda i: (fn(i), 0))

    if not isinstance(x, tuple):
        return [spec(x.shape[1], lambda i: i)], 0
    head, tail = x
    assert head.shape[0] % tm == 0 and tail.shape[0] % tm == 0 and head.shape[1] == tail.shape[1]
    n_head = head.shape[0] // tm
    return [spec(head.shape[1], lambda i: jnp.minimum(i, n_head - 1)),
            spec(head.shape[1], lambda i: jnp.maximum(i - n_head, 0))], n_head


def _read_rows(x_refs, n_head):
    if len(x_refs) == 1:
        return x_refs[0][...]
    return jnp.where(pl.program_id(0) < n_head, x_refs[0][...], x_refs[1][...])


def _n_rows(x):
    return x[0].shape[0] + x[1].shape[0] if isinstance(x, tuple) else x.shape[0]


def _as_list(x):
    return list(x) if isinstance(x, tuple) else [x]


def _proj_kernel(*refs, nq, has_kv, rope, n_x, n_head):
    it = iter(refs)
    x_refs = [next(it) for _ in range(n_x)]
    g_ref, w_ref = next(it), next(it)
    gq_ref = next(it) if nq else None
    gk_ref = next(it) if has_kv else None
    cos_ref, sin_ref = (next(it), next(it)) if rope else (None, None)
    q_ref = next(it) if nq else None
    if has_kv:
        kl_ref, kh_ref, vl_ref, vh_ref, kf_ref, vf_ref = (next(it) for _ in range(6))
    h_scr = next(it)
    h_scr[...] = _rms_rows(_read_rows(x_refs, n_head), g_ref[...]).astype(h_scr.dtype)
    cos = cos_ref[...] if rope else None
    sin = sin_ref[...] if rope else None
    groups = COL_TILE // LANES
    for j in range(nq + (2 if has_kv else 0)):
        y = jnp.dot(h_scr[...], w_ref[j], preferred_element_type=jnp.float32)
        for c in range(groups):
            yc = y[:, c * LANES:(c + 1) * LANES]
            if j < nq:
                z = _head_norm(yc, gq_ref[...], cos, sin) * (HEAD_DIM ** -0.5)
                q_ref[:, (j * groups + c) * LANES:(j * groups + c + 1) * LANES] = z.astype(q_ref.dtype)
            elif j == nq:
                z = _head_norm(yc, gk_ref[...], cos, sin)
                kf_ref[:, c * LANES:(c + 1) * LANES] = z
                _store_half_padded(kl_ref, kh_ref, c, z)
            else:
                vf_ref[:, c * LANES:(c + 1) * LANES] = yc
                _store_half_padded(vl_ref, vh_ref, c, yc)


def _project(x, g, w, gq, gk, cos, sin, *, nq, has_kv, rope):
    t = _n_rows(x)
    d = w.shape[0]
    tm = ROW_TILE
    n_col = nq + (2 if has_kv else 0)
    assert w.shape == (d, n_col * COL_TILE) and t % tm == 0
    w = jnp.swapaxes(w.reshape(d, n_col, COL_TILE), 0, 1)
    row = lambda i: (i, 0)
    x_specs, n_head = _row_specs(x, tm)
    in_specs = x_specs + [pl.BlockSpec((1, d), lambda i: (0, 0)),
                          pl.BlockSpec((n_col, d, COL_TILE), lambda i: (0, 0, 0))]
    args = _as_list(x) + [g, w]
    vec = pl.BlockSpec((1, LANES), lambda i: (0, 0))
    if nq:
        in_specs.append(vec)
        args.append(gq)
    if has_kv:
        in_specs.append(vec)
        args.append(gk)
    if rope:
        in_specs += [pl.BlockSpec((tm, LANES), row)] * 2
        args += [cos, sin]
    out_shape, out_specs = [], []
    if nq:
        out_shape.append(jax.ShapeDtypeStruct((t, nq * COL_TILE), jnp.bfloat16))
        out_specs.append(pl.BlockSpec((tm, nq * COL_TILE), row))
    if has_kv:
        for _ in range(4):
            out_shape.append(jax.ShapeDtypeStruct((t, HEADS_PER_VREG * COL_TILE), jnp.bfloat16))
            out_specs.append(pl.BlockSpec((tm, HEADS_PER_VREG * COL_TILE), row))
        for _ in range(2):
            out_shape.append(jax.ShapeDtypeStruct((t, COL_TILE), jnp.float32))
            out_specs.append(pl.BlockSpec((tm, COL_TILE), row))
    return pl.pallas_call(
        functools.partial(_proj_kernel, nq=nq, has_kv=has_kv, rope=rope,
                          n_x=len(x_specs), n_head=n_head),
        grid=(t // tm,),
        in_specs=in_specs,
        out_specs=out_specs,
        out_shape=out_shape,
        scratch_shapes=[pltpu.VMEM((tm, d), jnp.bfloat16)],
        compiler_params=_params(("parallel",)),
        name="proj",
    )(*args)


def _attn_prompt_kernel(*refs, seq, band, n_prev, unroll, has_bias, has_sink):
    it = iter(refs)
    q_ref, kl_ref, kh_ref, vl_ref, vh_ref = (next(it) for _ in range(5))
    bias_ref = next(it) if has_bias else None
    sink_ref = next(it) if has_sink else None
    o_ref = next(it)
    s_scr = (next(it), next(it))
    p_scr = (next(it), next(it))
    n_scr = (next(it), next(it))
    row = lax.broadcasted_iota(jnp.int32, (band, LANES), 0)
    nt = (((1,), (1,)), ((), ()))
    tn = (((0,), (0,)), ((), ()))
    n_groups = seq // (CHUNK * unroll)
    n_masked = min(-(-n_prev // unroll), n_groups)

    def key_start(c):
        return pl.multiple_of(jnp.maximum(c - n_prev, 0) * CHUNK, CHUNK)

    def scores(gi, slot):
        for u in range(unroll):
            c = gi * unroll + u
            r0 = pl.multiple_of(c * CHUNK, CHUNK)
            k0 = key_start(c)
            qs = jnp.concatenate([q_ref[pl.ds(r0, CHUNK), p * LANES:(p + 1) * LANES]
                                  for p in range(PAIRS)], axis=0)
            for half, k_ref in enumerate((kl_ref, kh_ref)):
                s_scr[slot][u, half] = lax.dot_general(k_ref[pl.ds(k0, band), :], qs, nt,
                                                        preferred_element_type=jnp.float32)

    def softmax(gi, slot, masked):
        for u in range(unroll):
            c = gi * unroll + u
            shift = pl.multiple_of(jnp.maximum(n_prev - c, 0) * CHUNK, CHUNK) if masked else 0
            inv = []
            for half in range(HEADS_PER_VREG):
                s = s_scr[slot][u, half]
                if has_bias:
                    s = s + bias_ref[half, pl.ds(shift, band), :]
                if masked:
                    s = jnp.where(row < band - shift, s, NEG_INF)
                m = jnp.max(s, axis=0, keepdims=True)
                if has_sink:
                    m = jnp.maximum(m, sink_ref[half])
                e = jnp.exp(s - m)
                l = jnp.sum(e, axis=0, keepdims=True)
                if has_sink:
                    l = l + jnp.exp(sink_ref[half] - m)
                p_scr[slot][u, half] = e.astype(p_scr[slot].dtype)
                inv.append(jnp.broadcast_to(1.0 / l, (LANES, LANES)))
            n_scr[slot][u] = jnp.where(row[:LANES] < HEAD_DIM, inv[0], inv[1])

    def outputs(gi, slot):
        for u in range(unroll):
            c = gi * unroll + u
            r0 = pl.multiple_of(c * CHUNK, CHUNK)
            k0 = key_start(c)
            o = (lax.dot_general(p_scr[slot][u, 0], vl_ref[pl.ds(k0, band), :], tn,
                                 preferred_element_type=jnp.float32)
                 + lax.dot_general(p_scr[slot][u, 1], vh_ref[pl.ds(k0, band), :], tn,
                                   preferred_element_type=jnp.float32))
            o = o * n_scr[slot][u].T
            for p in range(PAIRS):
                o_ref[pl.ds(r0, CHUNK), p * LANES:(p + 1) * LANES] = (
                    o[p * CHUNK:(p + 1) * CHUNK].astype(o_ref.dtype))

    def stage(g, parity, masked):
        static = isinstance(g, int)
        if not static or g < n_groups:
            scores(g, parity)
        if not static or 0 <= g - 1 < n_groups:
            softmax(g - 1, 1 - parity, masked)
        if not static or 0 <= g - 2 < n_groups:
            outputs(g - 2, parity)

    first_loop = max(n_masked + 1, 2)
    first_loop += first_loop % 2
    n_pairs = max(n_groups - first_loop, 0) // 2
    for g in range(min(first_loop, n_groups)):
        stage(g, g % 2, masked=True)
    if n_pairs:
        def body(k, carry):
            g = first_loop + 2 * k
            stage(g, 0, masked=False)
            stage(g + 1, 1, masked=False)
            return carry
        lax.fori_loop(0, n_pairs, body, 0)
    for g in range(min(first_loop, n_groups) + 2 * n_pairs, n_groups + 2):
        stage(g, g % 2, masked=(g - 1) < n_masked)


def _attend_prompt(q, kl, kh, vl, vh, bias, sink, *, nb, seq, n_prev):
    qw = q.shape[1]
    n_kv = kl.shape[1] // LANES
    band = (n_prev + 1) * CHUNK
    unroll = ATTN_UNROLL
    assert seq % (CHUNK * unroll) == 0 and seq >= band and seq >= 2 * CHUNK * unroll
    kv_spec = pl.BlockSpec((seq, LANES), lambda b, j: (b, j))
    in_specs = [pl.BlockSpec((seq, KV_GROUP_WIDTH), lambda b, j: (b, j))] + [kv_spec] * 4
    args = [q, kl, kh, vl, vh]
    if bias is not None:
        in_specs.append(pl.BlockSpec((HEADS_PER_VREG,) + bias.shape[1:], lambda b, j: (j, 0, 0)))
        args.append(bias)
    if sink is not None:
        in_specs.append(pl.BlockSpec((HEADS_PER_VREG, 1, LANES), lambda b, j: (j, 0, 0)))
        args.append(sink)
    return pl.pallas_call(
        functools.partial(_attn_prompt_kernel, seq=seq, band=band, n_prev=n_prev, unroll=unroll,
                          has_bias=bias is not None, has_sink=sink is not None),
        grid=(nb, n_kv),
        in_specs=in_specs,
        out_specs=pl.BlockSpec((seq, KV_GROUP_WIDTH), lambda b, j: (b, j)),
        out_shape=jax.ShapeDtypeStruct((nb * seq, qw), jnp.bfloat16),
        scratch_shapes=[pltpu.VMEM((unroll, HEADS_PER_VREG, band, LANES), jnp.float32)] * 2
        + [pltpu.VMEM((unroll, HEADS_PER_VREG, band, LANES), jnp.bfloat16)] * 2
        + [pltpu.VMEM((unroll, LANES, LANES), jnp.float32)] * 2,
        compiler_params=_params(("parallel", "arbitrary")),
        name="attn_prompt",
    )(*args)


def _attn_sample_kernel(*refs, lq, has_bias, has_sink, n_kv):
    it = iter(refs)
    q_ref, k_ref, v_ref = next(it), next(it), next(it)
    bias_ref = next(it) if has_bias else None
    sink_ref = next(it) if has_sink else None
    o_ref = next(it)
    for j in range(n_kv):
        qj = q_ref[:, j * KV_GROUP_WIDTH:(j + 1) * KV_GROUP_WIDTH]
        qcat = jnp.concatenate(
            [qj[:, g * HEAD_DIM:(g + 1) * HEAD_DIM] for g in range(GROUP)], axis=0)
        kj = k_ref[0, :, j * HEAD_DIM:(j + 1) * HEAD_DIM]
        vj = v_ref[0, :, j * HEAD_DIM:(j + 1) * HEAD_DIM]
        s = lax.dot_general(qcat, kj, (((1,), (1,)), ((), ())),
                            preferred_element_type=jnp.float32)
        if has_bias:
            s = s + bias_ref[j]
        m = jnp.max(s, axis=-1, keepdims=True)
        if has_sink:
            sink = sink_ref[j]
            m = jnp.maximum(m, sink)
        e = jnp.exp(s - m)
        l = jnp.sum(e, axis=-1, keepdims=True)
        if has_sink:
            l = l + jnp.exp(sink - m)
        p = (e * (1.0 / l)).astype(vj.dtype)
        o = jnp.dot(p, vj, preferred_element_type=jnp.float32)
        ocat = jnp.concatenate([o[g * lq:(g + 1) * lq] for g in range(GROUP)], axis=1)
        o_ref[:, j * KV_GROUP_WIDTH:(j + 1) * KV_GROUP_WIDTH] = ocat.astype(o_ref.dtype)


def _attend_sample(q, k, v, bias, sink, *, row0, lq):
    ns, lk, kw = k.shape
    qw = q.shape[1]
    assert row0 % lq == 0
    blk0 = row0 // lq
    in_specs = [pl.BlockSpec((lq, qw), lambda i: (blk0 + i, 0)),
                pl.BlockSpec((1, lk, kw), lambda i: (i, 0, 0)),
                pl.BlockSpec((1, lk, kw), lambda i: (i, 0, 0))]
    args = [q, k, v]
    if bias is not None:
        in_specs.append(pl.BlockSpec(bias.shape, lambda i: (0, 0, 0)))
        args.append(bias)
    if sink is not None:
        in_specs.append(pl.BlockSpec(sink.shape, lambda i: (0, 0, 0)))
        args.append(sink)
    return pl.pallas_call(
        functools.partial(_attn_sample_kernel, lq=lq, has_bias=bias is not None,
                          has_sink=sink is not None, n_kv=kw // HEAD_DIM),
        grid=(ns,),
        in_specs=in_specs,
        out_specs=pl.BlockSpec((lq, qw), lambda i: (i, 0)),
        out_shape=jax.ShapeDtypeStruct((ns * lq, qw), jnp.bfloat16),
        compiler_params=_params(("arbitrary",)),
        name="attn_sample",
    )(*args)


def _out_proj_kernel(*refs, n_o, o_head, n_x, x_head):
    o_refs, w_ref = refs[:n_o], refs[n_o]
    x_refs, y_ref = refs[n_o + 1:n_o + 1 + n_x], refs[n_o + 1 + n_x]
    y_ref[...] = _read_rows(x_refs, x_head) + jnp.dot(
        _read_rows(o_refs, o_head), w_ref[...], preferred_element_type=jnp.float32)


def _out_proj(o, w, x):
    t = _n_rows(x)
    d = w.shape[1]
    tm = ROW_TILE
    o_specs, o_head = _row_specs(o, tm)
    x_specs, x_head = _row_specs(x, tm)
    return pl.pallas_call(
        functools.partial(_out_proj_kernel, n_o=len(o_specs), o_head=o_head,
                          n_x=len(x_specs), x_head=x_head),
        grid=(t // tm,),
        in_specs=o_specs + [pl.BlockSpec(w.shape, lambda i: (0, 0))] + x_specs,
        out_specs=pl.BlockSpec((tm, d), lambda i: (i, 0)),
        out_shape=jax.ShapeDtypeStruct((t, d), jnp.float32),
        compiler_params=_params(("parallel",)),
        name="out_proj",
    )(*_as_list(o), w, *_as_list(x))


def _column_tiles(w, tf):
    *lead, d, ff = w.shape
    assert ff % tf == 0
    return jnp.swapaxes(w.reshape(*lead, d, ff // tf, tf), -3, -2)


def _swiglu_stage(h_scr, act_ref, wg_ref, wu_ref):
    h = h_scr[...]
    gate = jnp.dot(h, wg_ref[...], preferred_element_type=jnp.float32)
    up = jnp.dot(h, wu_ref[...], preferred_element_type=jnp.float32)
    act_ref[...] = ((gate * jax.nn.sigmoid(gate)) * up).astype(act_ref.dtype)


def _ffn_kernel(e_ref, n_ref, x_ref, g_ref, wg_ref, wu_ref, wd_ref, y_ref, h_scr, acc_scr,
                act0, act1, *, n_f):
    s = pl.program_id(1)
    act = (act0, act1)

    @pl.when(pl.program_id(0) < n_ref[0])
    def _():
        @pl.when(s == 0)
        def _():
            h_scr[...] = _rms_rows(x_ref[...], g_ref[...]).astype(h_scr.dtype)
            acc_scr[...] = jnp.zeros_like(acc_scr)
            _swiglu_stage(h_scr, act[0], wg_ref, wu_ref)

        for parity in range(2):
            @pl.when((s > 0) & (s < n_f) & (s % 2 == parity))
            def _():
                acc_scr[...] += jnp.dot(act[1 - parity][...], wd_ref[...],
                                        preferred_element_type=jnp.float32)
                _swiglu_stage(h_scr, act[parity], wg_ref, wu_ref)

        @pl.when(s == n_f)
        def _():
            part = jnp.dot(act[(n_f - 1) % 2][...], wd_ref[...],
                           preferred_element_type=jnp.float32)
            y_ref[...] = x_ref[...] + (acc_scr[...] + part)

    @pl.when((pl.program_id(0) >= n_ref[0]) & (s == n_f))
    def _():
        y_ref[...] = jnp.zeros_like(y_ref)


def _swiglu_blocks(x, g, wg, wu, wd, layer, block_expert, n_used, *, tm, tf):
    r, d = x.shape
    n_f = wg.shape[2]
    assert r % tm == 0 and wg.shape[3:] == (d, tf) and wd.shape[2:] == (n_f * tf, d)

    def blk(i, n):
        return jnp.minimum(i, n[0] - 1)

    def up_col(i, s, n):
        return jnp.where(i < n[0], jnp.minimum(s, n_f - 1), n_f - 1)

    def down_row(i, s, n):
        return jnp.where(i < n[0], jnp.maximum(s - 1, 0), n_f - 1)

    grid_spec = pltpu.PrefetchScalarGridSpec(
        num_scalar_prefetch=2,
        grid=(r // tm, n_f + 1),
        in_specs=[pl.BlockSpec((tm, d), lambda i, s, e, n: (blk(i, n), 0)),
                  pl.BlockSpec((1, d), lambda i, s, e, n: (0, 0)),
                  pl.BlockSpec((None, None, None, d, tf),
                               lambda i, s, e, n: (layer, e[blk(i, n)], up_col(i, s, n), 0, 0)),
                  pl.BlockSpec((None, None, None, d, tf),
                               lambda i, s, e, n: (layer, e[blk(i, n)], up_col(i, s, n), 0, 0)),
                  pl.BlockSpec((None, None, tf, d),
                               lambda i, s, e, n: (layer, e[blk(i, n)], down_row(i, s, n), 0))],
        out_specs=pl.BlockSpec((tm, d), lambda i, s, e, n: (i, 0)),
        scratch_shapes=[pltpu.VMEM((tm, d), jnp.bfloat16),
                        pltpu.VMEM((tm, d), jnp.float32),
                        pltpu.VMEM((tm, tf), jnp.bfloat16),
                        pltpu.VMEM((tm, tf), jnp.bfloat16)],
    )
    return pl.pallas_call(
        functools.partial(_ffn_kernel, n_f=n_f),
        grid_spec=grid_spec,
        out_shape=jax.ShapeDtypeStruct((r, d), jnp.float32),
        compiler_params=_params(("arbitrary", "arbitrary")),
        name="swiglu",
    )(block_expert, n_used, x, g, wg, wu, wd)


def _router_kernel(x_ref, g_ref, w_ref, b_ref, idx_ref, gate_ref, h_ref):
    h = _rms_rows(x_ref[...], g_ref[...]).astype(jnp.bfloat16)
    h_ref[...] = h.reshape(h_ref.shape)
    logits = jnp.dot(h, w_ref[...], preferred_element_type=jnp.float32) + b_ref[...]
    lane = lax.broadcasted_iota(jnp.int32, logits.shape, 1)
    m1 = jnp.max(logits, axis=-1, keepdims=True)
    e1 = jnp.min(jnp.where(logits == m1, lane, LANES), axis=-1, keepdims=True)
    rest = jnp.where(lane == e1, -jnp.inf, logits)
    m2 = jnp.max(rest, axis=-1, keepdims=True)
    e2 = jnp.min(jnp.where(rest == m2, lane, LANES), axis=-1, keepdims=True)
    t = jnp.exp(m2 - m1)
    denom = 1.0 + t
    idx_ref[...] = jnp.where(lane == 0, e1, jnp.where(lane == 1, e2, 0))
    gate_ref[...] = jnp.where(lane == 0, 1.0 / denom, jnp.where(lane == 1, t / denom, 0.0))


def _route(x, g, w_router, b_router):
    t, d = x.shape
    tm = ROW_TILE
    n_e = w_router.shape[1]
    w = jnp.zeros((d, LANES), jnp.bfloat16).at[:, :n_e].set(w_router.astype(jnp.bfloat16))
    b = jnp.full((1, LANES), -jnp.inf, jnp.float32).at[0, :n_e].set(b_router.astype(jnp.float32))
    return pl.pallas_call(
        _router_kernel,
        grid=(t // tm,),
        in_specs=[pl.BlockSpec((tm, d), lambda i: (i, 0)),
                  pl.BlockSpec((1, d), lambda i: (0, 0)),
                  pl.BlockSpec((d, LANES), lambda i: (0, 0)),
                  pl.BlockSpec((1, LANES), lambda i: (0, 0))],
        out_specs=[pl.BlockSpec((tm, LANES), lambda i: (i, 0))] * 2
        + [pl.BlockSpec((tm, d // LANES, LANES), lambda i: (i, 0, 0))],
        out_shape=[jax.ShapeDtypeStruct((t, LANES), jnp.int32),
                   jax.ShapeDtypeStruct((t, LANES), jnp.float32),
                   jax.ShapeDtypeStruct((t, d // LANES, LANES), jnp.bfloat16)],
        compiler_params=_params(("parallel",)),
        name="router",
    )(x, g, w, b)


def _experts_kernel(e_ref, n_ref, first_ref, next_ref, prev_dst_ref, dst_ref, h_hbm,
                    wg_ref, wu_ref, wd_ref, y_hbm, xbuf, ybuf, h_scr, acc_scr, act0, act1,
                    gsem, ssem, *, tm, n_f):
    i, s = pl.program_id(0), pl.program_id(1)
    act = (act0, act1)
    n_used = n_ref[0]
    slot = i % 2
    other = 1 - slot
    share = tm // n_f
    unroll = 2 * DMA_THREADS

    def gather_copy(idx_ref, r, to_slot):
        return pltpu.make_async_copy(h_hbm.at[idx_ref[r]], xbuf.at[to_slot, r], gsem.at[to_slot])

    def scatter_copy(idx_ref, r, from_slot):
        return pltpu.make_async_copy(ybuf.at[from_slot, r], y_hbm.at[idx_ref[r]],
                                     ssem.at[from_slot])

    def issue_all(copy_of_row):
        def issue(r4, carry):
            for k in range(unroll):
                copy_of_row(r4 * unroll + k).start(priority=k % DMA_THREADS)
            return carry
        lax.fori_loop(0, tm // unroll, issue, 0)

    def wait_gather(s_):
        pltpu.make_async_copy(h_hbm.at[pl.ds(0, tm)], xbuf.at[s_], gsem.at[s_]).wait()

    def wait_scatter(s_):
        pltpu.make_async_copy(ybuf.at[s_], y_hbm.at[pl.ds(0, tm)], ssem.at[s_]).wait()

    def row_copies(step):
        for k in range(share):
            r = step * share + k
            gather_copy(next_ref, r, other).start(priority=k % DMA_THREADS)
            scatter_copy(prev_dst_ref, r, other).start(priority=(k + 1) % DMA_THREADS)

    @pl.when(i < n_used)
    def _():
        @pl.when(s == 0)
        def _():
            @pl.when(i == 0)
            def _():
                issue_all(lambda r: gather_copy(first_ref, r, 0))
                ybuf[1] = jnp.zeros(ybuf.shape[1:], ybuf.dtype)
                for half in range(2):
                    spare = pltpu.make_async_copy(
                        ybuf.at[1], y_hbm.at[pl.ds(y_hbm.shape[0] - (3 - half) * tm, tm)],
                        ssem.at[1])
                    spare.start()
                    spare.wait()

            wait_gather(slot)
            h_scr[...] = xbuf[slot].reshape(h_scr.shape)
            acc_scr[...] = jnp.zeros_like(acc_scr)
            row_copies(0)
            _swiglu_stage(h_scr, act[0], wg_ref, wu_ref)

        for parity in range(2):
            @pl.when((s > 0) & (s < n_f) & (s % 2 == parity))
            def _():
                row_copies(s)
                acc_scr[...] += jnp.dot(act[1 - parity][...], wd_ref[...],
                                        preferred_element_type=jnp.float32)
                _swiglu_stage(h_scr, act[parity], wg_ref, wu_ref)

        @pl.when(s == n_f)
        def _():
            part = jnp.dot(act[(n_f - 1) % 2][...], wd_ref[...],
                           preferred_element_type=jnp.float32)

            @pl.when(i >= 1)
            def _():
                wait_scatter(slot)

            ybuf[slot] = (acc_scr[...] + part).reshape(ybuf.shape[1:])

            @pl.when(i == n_used - 1)
            def _():
                issue_all(lambda r: scatter_copy(dst_ref, r, slot))
                wait_scatter(slot)
                wait_scatter(other)
                wait_gather(other)


def _experts(h_rows, wg, wu, wd, layer, block_expert, n_used, row_tok, row_dst, n_out, *, tm, tf):
    tile = h_rows.shape[1:]
    n_f, d = wg.shape[2], wg.shape[3]
    rows = row_tok.shape[0]
    assert rows % tm == 0 and row_dst.shape[0] == rows + tm
    assert wg.shape[4] == tf and wd.shape[2:] == (n_f * tf, d)
    assert tm % (n_f * 2 * DMA_THREADS) == 0

    def blk(i, n):
        return jnp.minimum(i, n[0] - 1)

    def up_col(i, s, n):
        return jnp.where(i < n[0], jnp.minimum(s, n_f - 1), n_f - 1)

    def down_row(i, s, n):
        return jnp.where(i < n[0], jnp.maximum(s - 1, 0), n_f - 1)

    smem = functools.partial(pl.BlockSpec, memory_space=pltpu.SMEM)
    grid_spec = pltpu.PrefetchScalarGridSpec(
        num_scalar_prefetch=2,
        grid=(rows // tm, n_f + 1),
        in_specs=[smem((tm,), lambda i, f, e, n: (0,)),
                  smem((tm,), lambda i, f, e, n: (blk(i + 1, n),)),
                  smem((tm,), lambda i, f, e, n: (blk(i, n),)),
                  smem((tm,), lambda i, f, e, n: (blk(i, n) + 1,)),
                  pl.BlockSpec(memory_space=pl.ANY),
                  pl.BlockSpec((None, None, None, d, tf),
                               lambda i, f, e, n: (layer, e[blk(i, n)], up_col(i, f, n), 0, 0)),
                  pl.BlockSpec((None, None, None, d, tf),
                               lambda i, f, e, n: (layer, e[blk(i, n)], up_col(i, f, n), 0, 0)),
                  pl.BlockSpec((None, None, tf, d),
                               lambda i, f, e, n: (layer, e[blk(i, n)], down_row(i, f, n), 0))],
        out_specs=pl.BlockSpec(memory_space=pl.ANY),
        scratch_shapes=[pltpu.VMEM((2, tm) + tile, jnp.bfloat16),
                        pltpu.VMEM((2, tm) + tile, jnp.float32),
                        pltpu.VMEM((tm, d), jnp.bfloat16),
                        pltpu.VMEM((tm, d), jnp.float32),
                        pltpu.VMEM((tm, tf), jnp.bfloat16),
                        pltpu.VMEM((tm, tf), jnp.bfloat16),
                        pltpu.SemaphoreType.DMA((2,)),
                        pltpu.SemaphoreType.DMA((2,))],
    )
    return pl.pallas_call(
        functools.partial(_experts_kernel, tm=tm, n_f=n_f),
        grid_spec=grid_spec,
        out_shape=jax.ShapeDtypeStruct((n_out,) + tile, jnp.float32),
        compiler_params=_params(("arbitrary", "arbitrary")),
        name="experts",
    )(block_expert, n_used, row_tok, row_tok, row_dst, row_dst, h_rows, wg, wu, wd)


def _mix_kernel(x_ref, gate_ref, y0_ref, y1_ref, *out_refs, n_head):
    gates = gate_ref[...]
    shape = x_ref.shape
    y = gates[:, 0:1] * y0_ref[...].reshape(shape) + gates[:, 1:2] * y1_ref[...].reshape(shape)
    out = x_ref[...] + y
    if len(out_refs) == 1:
        out_refs[0][...] = out
    else:
        @pl.when(pl.program_id(0) < n_head)
        def _():
            out_refs[0][...] = out

        @pl.when(pl.program_id(0) >= n_head)
        def _():
            out_refs[1][...] = out


def _mix(x, gates, y, split_at=None):
    t, d = x.shape
    rows = COMBINE_ROWS
    tile = y.shape[1:]
    assert t % rows == 0
    n_t = t // rows
    if split_at is None:
        n_head = 0
        out_specs = pl.BlockSpec((rows, d), lambda i: (i, 0))
        out_shape = jax.ShapeDtypeStruct((t, d), jnp.float32)
    else:
        assert split_at % rows == 0 and 0 < split_at < t
        n_head = split_at // rows
        out_specs = [pl.BlockSpec((rows, d), lambda i: (jnp.minimum(i, n_head - 1), 0)),
                     pl.BlockSpec((rows, d), lambda i: (jnp.maximum(i - n_head, 0), 0))]
        out_shape = [jax.ShapeDtypeStruct((split_at, d), jnp.float32),
                     jax.ShapeDtypeStruct((t - split_at, d), jnp.float32)]
    return pl.pallas_call(
        functools.partial(_mix_kernel, n_head=n_head),
        grid=(n_t,),
        in_specs=[pl.BlockSpec((rows, d), lambda i: (i, 0)),
                  pl.BlockSpec((rows, LANES), lambda i: (i, 0)),
                  pl.BlockSpec((rows,) + tile, lambda i: (i, 0, 0)),
                  pl.BlockSpec((rows,) + tile, lambda i: (n_t + i, 0, 0))],
        out_specs=out_specs,
        out_shape=out_shape,
        compiler_params=_params(("arbitrary",)),
        name="mix",
    )(x, gates, y, y)


def _moe(x, g, w_router, b_router, wg, wu, wd, layer, split_at=None):
    t, d = x.shape
    n_e = wg.shape[1]
    n_assign = t * TOP_K
    top_idx, gates, h_rows = _route(x, g, w_router, b_router)
    flat_e = top_idx[:, :TOP_K].reshape(-1)
    onehot = (flat_e[:, None] == jnp.arange(n_e, dtype=jnp.int32)[None, :]).astype(jnp.int32)
    rank = jnp.sum((jnp.cumsum(onehot, axis=0) - onehot) * onehot, axis=1)
    counts = jnp.sum(onehot, axis=0)
    padded = (counts + MOE_ROWS - 1) // MOE_ROWS * MOE_ROWS
    ends = jnp.cumsum(padded)
    dest = jnp.sum((ends - padded)[None, :] * onehot, axis=1) + rank
    n_blocks = (n_assign + n_e * (MOE_ROWS - 1) + MOE_ROWS - 1) // MOE_ROWS
    rows = n_blocks * MOE_ROWS
    row_assign = jnp.full((rows,), -1, jnp.int32).at[dest].set(jnp.arange(n_assign, dtype=jnp.int32))
    real = row_assign >= 0
    row_tok = jnp.where(real, row_assign // TOP_K, 0)
    spare = n_assign + (jnp.arange(rows, dtype=jnp.int32) // MOE_ROWS % 2) * MOE_ROWS \
        + jnp.arange(rows, dtype=jnp.int32) % MOE_ROWS
    row_dst = jnp.where(real, (row_assign % TOP_K) * t + row_assign // TOP_K, spare)
    row_dst = jnp.concatenate(
        [n_assign + 2 * MOE_ROWS + jnp.arange(MOE_ROWS, dtype=jnp.int32), row_dst])
    block_start = jnp.arange(n_blocks, dtype=jnp.int32) * MOE_ROWS
    block_expert = jnp.minimum(
        jnp.sum((ends[None, :] <= block_start[:, None]).astype(jnp.int32), axis=1), n_e - 1)
    n_used = (ends[-1:] // MOE_ROWS).astype(jnp.int32)
    y = _experts(h_rows, wg, wu, wd, layer, block_expert, n_used, row_tok, row_dst,
                 n_assign + 3 * MOE_ROWS, tm=MOE_ROWS, tf=MOE_COL_TILE)
    return _mix(x, gates, y, split_at)


def _rope_tables(pos):
    half = HEAD_DIM // 2
    inv_freq = 1.0 / (ROPE_THETA ** (jnp.arange(0, HEAD_DIM, 2, dtype=jnp.float32) / HEAD_DIM))
    ang = pos.astype(jnp.float32)[:, None] * inv_freq[None, :]
    cos, sin = jnp.cos(ang), jnp.sin(ang)
    reps = LANES // half
    signs = jnp.tile(jnp.concatenate([-jnp.ones((half,)), jnp.ones((half,))]), HEADS_PER_VREG)
    return jnp.tile(cos, (1, reps)), jnp.tile(sin, (1, reps)) * signs[None, :].astype(jnp.float32)


def _rel_bias(table, lq, n_before):
    n_keys = n_before + lq
    p = n_keys + lq
    rel = n_before + (lq - 1) - jnp.arange(p)
    diag = table.astype(jnp.float32)[:, jnp.clip(rel, REL_MIN, REL_MAX) - REL_MIN]
    skew = jnp.tile(diag, (1, lq))[:, :lq * (p - 1)].reshape(-1, lq, p - 1)
    return skew[:, :, lq - 1:lq - 1 + n_keys]


def _pair_major(a):
    h = a.shape[0]
    a = a.reshape((h // GROUP, PAIRS, HEADS_PER_VREG) + a.shape[1:])
    return jnp.swapaxes(a, 1, 2)


def _prompt_bias(table, n_prev):
    pad = n_prev * CHUNK
    b = _pair_major(_rel_bias(table, CHUNK, pad))
    kvh = b.shape[0]
    b = jnp.transpose(b, (0, 1, 4, 2, 3)).reshape(kvh * HEADS_PER_VREG, pad + CHUNK, LANES)
    return jnp.pad(b, ((0, 0), (0, pad), (0, 0)))


def _prompt_sink(sink):
    s = _pair_major(sink.astype(jnp.float32))
    s = jnp.repeat(s[..., None], CHUNK, axis=-1)
    return s.reshape(-1, 1, LANES)


def _sample_bias(table, lq, n_cached):
    b = _rel_bias(table, lq, n_cached)
    return b.reshape(b.shape[0] // GROUP, GROUP * lq, n_cached + lq)


def _sample_sink(sink, lq):
    h = sink.shape[0]
    s = jnp.repeat(sink.astype(jnp.float32).reshape(h // GROUP, GROUP, 1), lq, axis=2)
    return s.reshape(h // GROUP, GROUP * lq, 1)


def _head_gain(g):
    return jnp.tile(g.astype(jnp.float32), HEADS_PER_VREG)[None, :]


def kernel(x_prompt, x_sample, cache_k_a, cache_v_a, cache_k_b, cache_v_b, g_attn, g_ffn, w_qkv_a, g_q_a, g_k_a, sink_a, w_o_a, g_kv_b, w_kv_b, g_k_b, w_q_b, g_q_b, rel_bias_b, w_o_b, w_gate_ffn, w_up_ffn, w_down_ffn, w_router, b_router, w_gate_moe, w_up_moe, w_down_moe):
    bf16 = jnp.bfloat16
    nb, seq, d = x_prompt.shape
    ns, seq_s, _ = x_sample.shape
    tp, ts = nb * seq, ns * seq_s
    t = tp + ts
    depth = g_attn.shape[0]
    n_a = w_qkv_a.shape[0]
    n_kv = cache_k_b.shape[2]
    kvw = n_kv * HEAD_DIM
    keep_a, keep_b = min(WINDOW_A, seq), min(BAND_B, seq)
    assert t % ROW_TILE == 0

    x = (x_prompt.reshape(tp, d), x_sample.reshape(ts, d))
    pos = jnp.concatenate([jnp.tile(jnp.arange(seq), nb),
                           jnp.tile(PAST_LEN + jnp.arange(seq_s), ns)])
    cos, sin = _rope_tables(pos)
    row = lambda v: v.astype(jnp.float32)[None, :]

    def attend(q, kv_pad, k_f32, v_f32, cache_k, cache_v, n_prev, table, sink):
        bias_p = None if table is None else _prompt_bias(table, n_prev)
        sink_p = None if sink is None else _prompt_sink(sink)
        o_p = _attend_prompt(q, *kv_pad, bias_p, sink_p, nb=nb, seq=seq, n_prev=n_prev)
        n_cached = cache_k.shape[1]
        ks = jnp.concatenate([cache_k.reshape(ns, n_cached, kvw),
                              k_f32[tp:].reshape(ns, seq_s, kvw)], axis=1).astype(bf16)
        vs = jnp.concatenate([cache_v.reshape(ns, n_cached, kvw),
                              v_f32[tp:].reshape(ns, seq_s, kvw)], axis=1).astype(bf16)
        bias_s = None if table is None else _sample_bias(table, seq_s, n_cached)
        sink_s = None if sink is None else _sample_sink(sink, seq_s)
        return o_p, _attend_sample(q, ks, vs, bias_s, sink_s, row0=tp, lq=seq_s)

    def tail(c, n_keep):
        rows = jnp.stack([c[(b + 1) * seq - n_keep:(b + 1) * seq] for b in range(nb)])
        return rows.reshape(nb, n_keep, n_kv, HEAD_DIM)

    def rolled(cache, c):
        new = c[tp:].reshape(ns, seq_s, n_kv, HEAD_DIM)
        return jnp.concatenate([cache, new], axis=1)[:, seq_s:]

    ffn_w = [_column_tiles(w.astype(bf16)[:, None], COL_TILE) for w in (w_gate_ffn, w_up_ffn)]
    ffn_w.append(w_down_ffn.astype(bf16)[:, None])
    moe_w = [_column_tiles(w.astype(bf16), MOE_COL_TILE) for w in (w_gate_moe, w_up_moe)]
    moe_w.append(w_down_moe.astype(bf16))

    ka_p, va_p, ka_s, va_s = [], [], [], []
    kvb_pad = kb_f = vb_f = None
    for layer in range(depth):
        g_in = row(g_attn[layer])
        if layer < n_a:
            i = layer
            q, kl, kh, vl, vh, k_f32, v_f32 = _project(
                x, g_in, w_qkv_a[i].astype(bf16), _head_gain(g_q_a[i]), _head_gain(g_k_a[i]),
                cos, sin, nq=(w_qkv_a.shape[2] - 2 * kvw) // COL_TILE, has_kv=True, rope=True)
            o = attend(q, (kl, kh, vl, vh), k_f32, v_f32, cache_k_a[i], cache_v_a[i],
                       PREV_CHUNKS_A, None, sink_a[i])
            x = _out_proj(o, w_o_a[i].astype(bf16), x)
            ka_p.append(tail(k_f32, keep_a))
            va_p.append(tail(v_f32, keep_a))
            ka_s.append(rolled(cache_k_a[i], k_f32))
            va_s.append(rolled(cache_v_a[i], v_f32))
        else:
            jb = layer - n_a
            (q,) = _project(x, g_in, w_q_b[jb].astype(bf16), _head_gain(g_q_b[jb]), None,
                            None, None, nq=w_q_b.shape[2] // COL_TILE, has_kv=False, rope=False)
            o = attend(q, kvb_pad, kb_f, vb_f, cache_k_b, cache_v_b, PREV_CHUNKS_B,
                       rel_bias_b[jb], None)
            x = _out_proj(o, w_o_b[jb].astype(bf16), x)

        g_mid = row(g_ffn[layer])
        m = layer // 2
        if layer % 2 == 0:
            n_tiles = t // ROW_TILE
            x = _swiglu_blocks(x, g_mid, *ffn_w, m, jnp.zeros((n_tiles,), jnp.int32),
                               jnp.full((1,), n_tiles, jnp.int32), tm=ROW_TILE, tf=COL_TILE)
        else:
            x = _moe(x, g_mid, w_router[m], b_router[m], *moe_w, m,
                     split_at=tp if layer == depth - 1 else None)

        if layer == n_a - 1:
            *kvb_pad, kb_f, vb_f = _project(x, row(g_kv_b), w_kv_b.astype(bf16), None,
                                            _head_gain(g_k_b), None, None,
                                            nq=0, has_kv=True, rope=False)

    if not isinstance(x, tuple):
        x = (x[:tp], x[tp:])
    y_prompt = x[0].reshape(nb, seq, d)
    y_sample = x[1].reshape(ns, seq_s, d)
    return (y_prompt, y_sample, jnp.stack(ka_p), jnp.stack(va_p), tail(kb_f, keep_b),
            tail(vb_f, keep_b), jnp.stack(ka_s), jnp.stack(va_s),
            rolled(cache_k_b, kb_f), rolled(cache_v_b, vb_f))
```

```python
import functools

import jax
import jax.numpy as jnp
from jax import lax
from jax.experimental import pallas as pl
from jax.experimental.pallas import tpu as pltpu

CHUNK = 64
HEAD_DIM = 64
GROUP = 4
WINDOW_A = 128
PREV_CHUNKS_A = WINDOW_A // CHUNK
PREV_CHUNKS_B = 8
BAND_B = PREV_CHUNKS_B * CHUNK
REL_MIN = -(CHUNK - 1)
REL_MAX = 256
ROPE_THETA = 10000.0
TOP_K = 2
RMS_EPS = 1e-6
NEG_INF = -1e30
PAST_LEN = 1024

LANES = 128
HEADS_PER_VREG = LANES // HEAD_DIM
KV_GROUP_WIDTH = GROUP * HEAD_DIM
PAIRS = GROUP // HEADS_PER_VREG

ROW_TILE = 512
COL_TILE = 512
FFN_ROWS = 640
MOE_ROWS = 640
MOE_COL_TILE = 1024
COMBINE_ROWS = 256
ATTN_UNROLL = 4
DMA_THREADS = 2
VMEM_LIMIT = 56 * 1024 * 1024


def _params(semantics):
    return pltpu.CompilerParams(dimension_semantics=semantics, vmem_limit_bytes=VMEM_LIMIT)


def _rms_rows(x, g):
    ms = jnp.sum(x * x, axis=-1, keepdims=True) * (1.0 / x.shape[-1])
    return (x * lax.rsqrt(ms + RMS_EPS)) * g


def _head_norm(y, gain, cos, sin):
    lane = lax.broadcasted_iota(jnp.int32, y.shape, 1)
    low = lane < HEAD_DIM
    ss = y * y
    s_lo = jnp.sum(jnp.where(low, ss, 0.0), axis=-1, keepdims=True)
    s_hi = jnp.sum(jnp.where(low, 0.0, ss), axis=-1, keepdims=True)
    ms = jnp.where(low, s_lo, s_hi) * (1.0 / HEAD_DIM)
    yn = (y * lax.rsqrt(ms + RMS_EPS)) * gain
    if cos is None:
        return yn
    half = HEAD_DIM // 2
    fwd = pltpu.roll(yn, half, 1)
    bwd = pltpu.roll(yn, LANES - half, 1)
    rot = jnp.where((lane % HEAD_DIM) < half, bwd, fwd)
    return yn * cos + rot * sin


def _store_half_padded(lo_ref, hi_ref, c, z):
    lane = lax.broadcasted_iota(jnp.int32, z.shape, 1)
    low = lane < HEAD_DIM
    zr = pltpu.roll(z, HEAD_DIM, 1)
    dt = lo_ref.dtype
    a, b = 2 * c * LANES, (2 * c + 1) * LANES
    lo_ref[:, a:a + LANES] = jnp.where(low, z, 0.0).astype(dt)
    lo_ref[:, b:b + LANES] = jnp.where(low, zr, 0.0).astype(dt)
    hi_ref[:, a:a + LANES] = jnp.where(low, 0.0, zr).astype(dt)
    hi_ref[:, b:b + LANES] = jnp.where(low, 0.0, z).astype(dt)


def _row_specs(x, tm):
    def spec(cols, fn):
        return pl.BlockSpec((tm, cols), lambda i: (fn(i), 0))

    if not isinstance(x, tuple):
        return [spec(x.shape[1], lambda i: i)], 0
    head, tail = x
    assert head.shape[0] % tm == 0 and tail.shape[0] % tm == 0 and head.shape[1] == tail.shape[1]
    n_head = head.shape[0] // tm
    return [spec(head.shape[1], lambda i: jnp.minimum(i, n_head - 1)),
            spec(head.shape[1], lambda i: jnp.maximum(i - n_head, 0))], n_head


def _read_rows(x_refs, n_head):
    if len(x_refs) == 1:
        return x_refs[0][...]
    return jnp.where(pl.program_id(0) < n_head, x_refs[0][...], x_refs[1][...])


def _n_rows(x):
    return x[0].shape[0] + x[1].shape[0] if isinstance(x, tuple) else x.shape[0]


def _as_list(x):
    return list(x) if isinstance(x, tuple) else [x]


def _proj_kernel(*refs, nq, has_kv, rope, n_x, n_head):
    it = iter(refs)
    x_refs = [next(it) for _ in range(n_x)]
    g_ref, w_ref = next(it), next(it)
    gq_ref = next(it) if nq else None
    gk_ref = next(it) if has_kv else None
    cos_ref, sin_ref = (next(it), next(it)) if rope else (None, None)
    q_ref = next(it) if nq else None
    if has_kv:
        kl_ref, kh_ref, vl_ref, vh_ref, kf_ref, vf_ref = (next(it) for _ in range(6))
    h_scr = next(it)
    h_scr[...] = _rms_rows(_read_rows(x_refs, n_head), g_ref[...]).astype(h_scr.dtype)
    cos = cos_ref[...] if rope else None
    sin = sin_ref[...] if rope else None
    groups = COL_TILE // LANES
    for j in range(nq + (2 if has_kv else 0)):
        y = jnp.dot(h_scr[...], w_ref[j], preferred_element_type=jnp.float32)
        for c in range(groups):
            yc = y[:, c * LANES:(c + 1) * LANES]
            if j < nq:
                z = _head_norm(yc, gq_ref[...], cos, sin) * (HEAD_DIM ** -0.5)
                q_ref[:, (j * groups + c) * LANES:(j * groups + c + 1) * LANES] = z.astype(q_ref.dtype)
            elif j == nq:
                z = _head_norm(yc, gk_ref[...], cos, sin)
                kf_ref[:, c * LANES:(c + 1) * LANES] = z
                _store_half_padded(kl_ref, kh_ref, c, z)
            else:
                vf_ref[:, c * LANES:(c + 1) * LANES] = yc
                _store_half_padded(vl_ref, vh_ref, c, yc)


def _project(x, g, w, gq, gk, cos, sin, *, nq, has_kv, rope):
    t = _n_rows(x)
    d = w.shape[0]
    tm = ROW_TILE
    n_col = nq + (2 if has_kv else 0)
    assert w.shape == (d, n_col * COL_TILE) and t % tm == 0
    w = jnp.swapaxes(w.reshape(d, n_col, COL_TILE), 0, 1)
    row = lambda i: (i, 0)
    x_specs, n_head = _row_specs(x, tm)
    in_specs = x_specs + [pl.BlockSpec((1, d), lambda i: (0, 0)),
                          pl.BlockSpec((n_col, d, COL_TILE), lambda i: (0, 0, 0))]
    args = _as_list(x) + [g, w]
    vec = pl.BlockSpec((1, LANES), lambda i: (0, 0))
    if nq:
        in_specs.append(vec)
        args.append(gq)
    if has_kv:
        in_specs.append(vec)
        args.append(gk)
    if rope:
        in_specs += [pl.BlockSpec((tm, LANES), row)] * 2
        args += [cos, sin]
    out_shape, out_specs = [], []
    if nq:
        out_shape.append(jax.ShapeDtypeStruct((t, nq * COL_TILE), jnp.bfloat16))
        out_specs.append(pl.BlockSpec((tm, nq * COL_TILE), row))
    if has_kv:
        for _ in range(4):
            out_shape.append(jax.ShapeDtypeStruct((t, HEADS_PER_VREG * COL_TILE), jnp.bfloat16))
            out_specs.append(pl.BlockSpec((tm, HEADS_PER_VREG * COL_TILE), row))
        for _ in range(2):
            out_shape.append(jax.ShapeDtypeStruct((t, COL_TILE), jnp.float32))
            out_specs.append(pl.BlockSpec((tm, COL_TILE), row))
    return pl.pallas_call(
        functools.partial(_proj_kernel, nq=nq, has_kv=has_kv, rope=rope,
                          n_x=len(x_specs), n_head=n_head),
        grid=(t // tm,),
        in_specs=in_specs,
        out_specs=out_specs,
        out_shape=out_shape,
        scratch_shapes=[pltpu.VMEM((tm, d), jnp.bfloat16)],
        compiler_params=_params(("parallel",)),
        name="proj",
    )(*args)


def _attn_prompt_kernel(*refs, seq, band, n_prev, unroll, has_bias, has_sink):
    it = iter(refs)
    q_ref, kl_ref, kh_ref, vl_ref, vh_ref = (next(it) for _ in range(5))
    bias_ref = next(it) if has_bias else None
    sink_ref = next(it) if has_sink else None
    o_ref = next(it)
    s_scr = (next(it), next(it))
    p_scr = (next(it), next(it))
    n_scr = (next(it), next(it))
    row = lax.broadcasted_iota(jnp.int32, (band, LANES), 0)
    nt = (((1,), (1,)), ((), ()))
    tn = (((0,), (0,)), ((), ()))
    n_groups = seq // (CHUNK * unroll)
    n_masked = min(-(-n_prev // unroll), n_groups)

    def key_start(c):
        return pl.multiple_of(jnp.maximum(c - n_prev, 0) * CHUNK, CHUNK)

    def scores(gi, slot):
        for u in range(unroll):
            c = gi * unroll + u
            r0 = pl.multiple_of(c * CHUNK, CHUNK)
            k0 = key_start(c)
            qs = jnp.concatenate([q_ref[pl.ds(r0, CHUNK), p * LANES:(p + 1) * LANES]
                                  for p in range(PAIRS)], axis=0)
            for half, k_ref in enumerate((kl_ref, kh_ref)):
                s_scr[slot][u, half] = lax.dot_general(k_ref[pl.ds(k0, band), :], qs, nt,
                                                        preferred_element_type=jnp.float32)

    def softmax(gi, slot, masked):
        for u in range(unroll):
            c = gi * unroll + u
            shift = pl.multiple_of(jnp.maximum(n_prev - c, 0) * CHUNK, CHUNK) if masked else 0
            inv = []
            for half in range(HEADS_PER_VREG):
                s = s_scr[slot][u, half]
                if has_bias:
                    s = s + bias_ref[half, pl.ds(shift, band), :]
                if masked:
                    s = jnp.where(row < band - shift, s, NEG_INF)
                m = jnp.max(s, axis=0, keepdims=True)
                if has_sink:
                    m = jnp.maximum(m, sink_ref[half])
                e = jnp.exp(s - m)
                l = jnp.sum(e, axis=0, keepdims=True)
                if has_sink:
                    l = l + jnp.exp(sink_ref[half] - m)
                p_scr[slot][u, half] = e.astype(p_scr[slot].dtype)
                inv.append(jnp.broadcast_to(1.0 / l, (LANES, LANES)))
            n_scr[slot][u] = jnp.where(row[:LANES] < HEAD_DIM, inv[0], inv[1])

    def outputs(gi, slot):
        for u in range(unroll):
            c = gi * unroll + u
            r0 = pl.multiple_of(c * CHUNK, CHUNK)
            k0 = key_start(c)
            o = (lax.dot_general(p_scr[slot][u, 0], vl_ref[pl.ds(k0, band), :], tn,
                                 preferred_element_type=jnp.float32)
                 + lax.dot_general(p_scr[slot][u, 1], vh_ref[pl.ds(k0, band), :], tn,
                                   preferred_element_type=jnp.float32))
            o = o * n_scr[slot][u].T
            for p in range(PAIRS):
                o_ref[pl.ds(r0, CHUNK), p * LANES:(p + 1) * LANES] = (
                    o[p * CHUNK:(p + 1) * CHUNK].astype(o_ref.dtype))

    def stage(g, parity, masked):
        static = isinstance(g, int)
        if not static or g < n_groups:
            scores(g, parity)
        if not static or 0 <= g - 1 < n_groups:
            softmax(g - 1, 1 - parity, masked)
        if not static or 0 <= g - 2 < n_groups:
            outputs(g - 2, parity)

    first_loop = max(n_masked + 1, 2)
    first_loop += first_loop % 2
    n_pairs = max(n_groups - first_loop, 0) // 2
    for g in range(min(first_loop, n_groups)):
        stage(g, g % 2, masked=True)
    if n_pairs:
        def body(k, carry):
            g = first_loop + 2 * k
            stage(g, 0, masked=False)
            stage(g + 1, 1, masked=False)
            return carry
        lax.fori_loop(0, n_pairs, body, 0)
    for g in range(min(first_loop, n_groups) + 2 * n_pairs, n_groups + 2):
        stage(g, g % 2, masked=(g - 1) < n_masked)


def _attend_prompt(q, kl, kh, vl, vh, bias, sink, *, nb, seq, n_prev):
    qw = q.shape[1]
    n_kv = kl.shape[1] // LANES
    band = (n_prev + 1) * CHUNK
    unroll = ATTN_UNROLL
    assert seq % (CHUNK * unroll) == 0 and seq >= band and seq >= 2 * CHUNK * unroll
    kv_spec = pl.BlockSpec((seq, LANES), lambda b, j: (b, j))
    in_specs = [pl.BlockSpec((seq, KV_GROUP_WIDTH), lambda b, j: (b, j))] + [kv_spec] * 4
    args = [q, kl, kh, vl, vh]
    if bias is not None:
        in_specs.append(pl.BlockSpec((HEADS_PER_VREG,) + bias.shape[1:], lambda b, j: (j, 0, 0)))
        args.append(bias)
    if sink is not None:
        in_specs.append(pl.BlockSpec((HEADS_PER_VREG, 1, LANES), lambda b, j: (j, 0, 0)))
        args.append(sink)
    return pl.pallas_call(
        functools.partial(_attn_prompt_kernel, seq=seq, band=band, n_prev=n_prev, unroll=unroll,
                          has_bias=bias is not None, has_sink=sink is not None),
        grid=(nb, n_kv),
        in_specs=in_specs,
        out_specs=pl.BlockSpec((seq, KV_GROUP_WIDTH), lambda b, j: (b, j)),
        out_shape=jax.ShapeDtypeStruct((nb * seq, qw), jnp.bfloat16),
        scratch_shapes=[pltpu.VMEM((unroll, HEADS_PER_VREG, band, LANES), jnp.float32)] * 2
        + [pltpu.VMEM((unroll, HEADS_PER_VREG, band, LANES), jnp.bfloat16)] * 2
        + [pltpu.VMEM((unroll, LANES, LANES), jnp.float32)] * 2,
        compiler_params=_params(("parallel", "arbitrary")),
        name="attn_prompt",
    )(*args)


def _attn_sample_kernel(*refs, lq, has_bias, has_sink, n_kv):
    it = iter(refs)
    q_ref, k_ref, v_ref = next(it), next(it), next(it)
    bias_ref = next(it) if has_bias else None
    sink_ref = next(it) if has_sink else None
    o_ref = next(it)
    for j in range(n_kv):
        qj = q_ref[:, j * KV_GROUP_WIDTH:(j + 1) * KV_GROUP_WIDTH]
        qcat = jnp.concatenate(
            [qj[:, g * HEAD_DIM:(g + 1) * HEAD_DIM] for g in range(GROUP)], axis=0)
        kj = k_ref[0, :, j * HEAD_DIM:(j + 1) * HEAD_DIM]
        vj = v_ref[0, :, j * HEAD_DIM:(j + 1) * HEAD_DIM]
        s = lax.dot_general(qcat, kj, (((1,), (1,)), ((), ())),
                            preferred_element_type=jnp.float32)
        if has_bias:
            s = s + bias_ref[j]
        m = jnp.max(s, axis=-1, keepdims=True)
        if has_sink:
            sink = sink_ref[j]
            m = jnp.maximum(m, sink)
        e = jnp.exp(s - m)
        l = jnp.sum(e, axis=-1, keepdims=True)
        if has_sink:
            l = l + jnp.exp(sink - m)
        p = (e * (1.0 / l)).astype(vj.dtype)
        o = jnp.dot(p, vj, preferred_element_type=jnp.float32)
        ocat = jnp.concatenate([o[g * lq:(g + 1) * lq] for g in range(GROUP)], axis=1)
        o_ref[:, j * KV_GROUP_WIDTH:(j + 1) * KV_GROUP_WIDTH] = ocat.astype(o_ref.dtype)


def _attend_sample(q, k, v, bias, sink, *, row0, lq):
    ns, lk, kw = k.shape
    qw = q.shape[1]
    assert row0 % lq == 0
    blk0 = row0 // lq
    in_specs = [pl.BlockSpec((lq, qw), lambda i: (blk0 + i, 0)),
                pl.BlockSpec((1, lk, kw), lambda i: (i, 0, 0)),
                pl.BlockSpec((1, lk, kw), lambda i: (i, 0, 0))]
    args = [q, k, v]
    if bias is not None:
        in_specs.append(pl.BlockSpec(bias.shape, lambda i: (0, 0, 0)))
        args.append(bias)
    if sink is not None:
        in_specs.append(pl.BlockSpec(sink.shape, lambda i: (0, 0, 0)))
        args.append(sink)
    return pl.pallas_call(
        functools.partial(_attn_sample_kernel, lq=lq, has_bias=bias is not None,
                          has_sink=sink is not None, n_kv=kw // HEAD_DIM),
        grid=(ns,),
        in_specs=in_specs,
        out_specs=pl.BlockSpec((lq, qw), lambda i: (i, 0)),
        out_shape=jax.ShapeDtypeStruct((ns * lq, qw), jnp.bfloat16),
        compiler_params=_params(("arbitrary",)),
        name="attn_sample",
    )(*args)


def _out_proj_kernel(*refs, n_o, o_head, n_x, x_head):
    o_refs, w_ref = refs[:n_o], refs[n_o]
    x_refs, y_ref = refs[n_o + 1:n_o + 1 + n_x], refs[n_o + 1 + n_x]
    y_ref[...] = _read_rows(x_refs, x_head) + jnp.dot(
        _read_rows(o_refs, o_head), w_ref[...], preferred_element_type=jnp.float32)


def _out_proj(o, w, x):
    t = _n_rows(x)
    d = w.shape[1]
    tm = ROW_TILE
    o_specs, o_head = _row_specs(o, tm)
    x_specs, x_head = _row_specs(x, tm)
    return pl.pallas_call(
        functools.partial(_out_proj_kernel, n_o=len(o_specs), o_head=o_head,
                          n_x=len(x_specs), x_head=x_head),
        grid=(t // tm,),
        in_specs=o_specs + [pl.BlockSpec(w.shape, lambda i: (0, 0))] + x_specs,
        out_specs=pl.BlockSpec((tm, d), lambda i: (i, 0)),
        out_shape=jax.ShapeDtypeStruct((t, d), jnp.float32),
        compiler_params=_params(("parallel",)),
        name="out_proj",
    )(*_as_list(o), w, *_as_list(x))


def _column_tiles(w, tf):
    *lead, d, ff = w.shape
    assert ff % tf == 0
    return jnp.swapaxes(w.reshape(*lead, d, ff // tf, tf), -3, -2)


def _swiglu_stage(h_scr, act_ref, wg_ref, wu_ref):
    h = h_scr[...]
    gate = jnp.dot(h, wg_ref[...], preferred_element_type=jnp.float32)
    up = jnp.dot(h, wu_ref[...], preferred_element_type=jnp.float32)
    act_ref[...] = ((gate * jax.nn.sigmoid(gate)) * up).astype(act_ref.dtype)


def _ffn_kernel(e_ref, n_ref, x_ref, g_ref, wg_ref, wu_ref, wd_ref, y_ref, h_scr, acc_scr,
                act0, act1, *, n_f):
    s = pl.program_id(1)
    act = (act0, act1)

    @pl.when(pl.program_id(0) < n_ref[0])
    def _():
        @pl.when(s == 0)
        def _():
            h_scr[...] = _rms_rows(x_ref[...], g_ref[...]).astype(h_scr.dtype)
            acc_scr[...] = jnp.zeros_like(acc_scr)
            _swiglu_stage(h_scr, act[0], wg_ref, wu_ref)

        for parity in range(2):
            @pl.when((s > 0) & (s < n_f) & (s % 2 == parity))
            def _():
                acc_scr[...] += jnp.dot(act[1 - parity][...], wd_ref[...],
                                        preferred_element_type=jnp.float32)
                _swiglu_stage(h_scr, act[parity], wg_ref, wu_ref)

        @pl.when(s == n_f)
        def _():
            part = jnp.dot(act[(n_f - 1) % 2][...], wd_ref[...],
                           preferred_element_type=jnp.float32)
            y_ref[...] = x_ref[...] + (acc_scr[...] + part)

    @pl.when((pl.program_id(0) >= n_ref[0]) & (s == n_f))
    def _():
        y_ref[...] = jnp.zeros_like(y_ref)


def _swiglu_blocks(x, g, wg, wu, wd, layer, block_expert, n_used, *, tm, tf):
    r, d = x.shape
    n_f = wg.shape[2]
    assert r % tm == 0 and wg.shape[3:] == (d, tf) and wd.shape[2:] == (n_f * tf, d)

    def blk(i, n):
        return jnp.minimum(i, n[0] - 1)

    def up_col(i, s, n):
        return jnp.where(i < n[0], jnp.minimum(s, n_f - 1), n_f - 1)

    def down_row(i, s, n):
        return jnp.where(i < n[0], jnp.maximum(s - 1, 0), n_f - 1)

    grid_spec = pltpu.PrefetchScalarGridSpec(
        num_scalar_prefetch=2,
        grid=(r // tm, n_f + 1),
        in_specs=[pl.BlockSpec((tm, d), lambda i, s, e, n: (blk(i, n), 0)),
                  pl.BlockSpec((1, d), lambda i, s, e, n: (0, 0)),
                  pl.BlockSpec((None, None, None, d, tf),
                               lambda i, s, e, n: (layer, e[blk(i, n)], up_col(i, s, n), 0, 0)),
                  pl.BlockSpec((None, None, None, d, tf),
                               lambda i, s, e, n: (layer, e[blk(i, n)], up_col(i, s, n), 0, 0)),
                  pl.BlockSpec((None, None, tf, d),
                               lambda i, s, e, n: (layer, e[blk(i, n)], down_row(i, s, n), 0))],
        out_specs=pl.BlockSpec((tm, d), lambda i, s, e, n: (i, 0)),
        scratch_shapes=[pltpu.VMEM((tm, d), jnp.bfloat16),
                        pltpu.VMEM((tm, d), jnp.float32),
                        pltpu.VMEM((tm, tf), jnp.bfloat16),
                        pltpu.VMEM((tm, tf), jnp.bfloat16)],
    )
    return pl.pallas_call(
        functools.partial(_ffn_kernel, n_f=n_f),
        grid_spec=grid_spec,
        out_shape=jax.ShapeDtypeStruct((r, d), jnp.float32),
        compiler_params=_params(("arbitrary", "arbitrary")),
        name="swiglu",
    )(block_expert, n_used, x, g, wg, wu, wd)


def _router_kernel(x_ref, g_ref, w_ref, b_ref, idx_ref, gate_ref, h_ref):
    h = _rms_rows(x_ref[...], g_ref[...]).astype(jnp.bfloat16)
    h_ref[...] = h.reshape(h_ref.shape)
    logits = jnp.dot(h, w_ref[...], preferred_element_type=jnp.float32) + b_ref[...]
    lane = lax.broadcasted_iota(jnp.int32, logits.shape, 1)
    m1 = jnp.max(logits, axis=-1, keepdims=True)
    e1 = jnp.min(jnp.where(logits == m1, lane, LANES), axis=-1, keepdims=True)
    rest = jnp.where(lane == e1, -jnp.inf, logits)
    m2 = jnp.max(rest, axis=-1, keepdims=True)
    e2 = jnp.min(jnp.where(rest == m2, lane, LANES), axis=-1, keepdims=True)
    t = jnp.exp(m2 - m1)
    denom = 1.0 + t
    idx_ref[...] = jnp.where(lane == 0, e1, jnp.where(lane == 1, e2, 0))
    gate_ref[...] = jnp.where(lane == 0, 1.0 / denom, jnp.where(lane == 1, t / denom, 0.0))


def _route(x, g, w_router, b_router):
    t, d = x.shape
    tm = ROW_TILE
    n_e = w_router.shape[1]
    w = jnp.zeros((d, LANES), jnp.bfloat16).at[:, :n_e].set(w_router.astype(jnp.bfloat16))
    b = jnp.full((1, LANES), -jnp.inf, jnp.float32).at[0, :n_e].set(b_router.astype(jnp.float32))
    return pl.pallas_call(
        _router_kernel,
        grid=(t // tm,),
        in_specs=[pl.BlockSpec((tm, d), lambda i: (i, 0)),
                  pl.BlockSpec((1, d), lambda i: (0, 0)),
                  pl.BlockSpec((d, LANES), lambda i: (0, 0)),
                  pl.BlockSpec((1, LANES), lambda i: (0, 0))],
        out_specs=[pl.BlockSpec((tm, LANES), lambda i: (i, 0))] * 2
        + [pl.BlockSpec((tm, d // LANES, LANES), lambda i: (i, 0, 0))],
        out_shape=[jax.ShapeDtypeStruct((t, LANES), jnp.int32),
                   jax.ShapeDtypeStruct((t, LANES), jnp.float32),
                   jax.ShapeDtypeStruct((t, d // LANES, LANES), jnp.bfloat16)],
        compiler_params=_params(("parallel",)),
        name="router",
    )(x, g, w, b)


def _experts_kernel(e_ref, n_ref, first_ref, next_ref, prev_dst_ref, dst_ref, h_hbm,
                    wg_ref, wu_ref, wd_ref, y_hbm, xbuf, ybuf, h_scr, acc_scr, act0, act1,
                    gsem, ssem, *, tm, n_f):
    i, s = pl.program_id(0), pl.program_id(1)
    act = (act0, act1)
    n_used = n_ref[0]
    slot = i % 2
    other = 1 - slot
    share = tm // n_f
    unroll = 2 * DMA_THREADS

    def gather_copy(idx_ref, r, to_slot):
        return pltpu.make_async_copy(h_hbm.at[idx_ref[r]], xbuf.at[to_slot, r], gsem.at[to_slot])

    def scatter_copy(idx_ref, r, from_slot):
        return pltpu.make_async_copy(ybuf.at[from_slot, r], y_hbm.at[idx_ref[r]],
                                     ssem.at[from_slot])

    def issue_all(copy_of_row):
        def issue(r4, carry):
            for k in range(unroll):
                copy_of_row(r4 * unroll + k).start(priority=k % DMA_THREADS)
            return carry
        lax.fori_loop(0, tm // unroll, issue, 0)

    def wait_gather(s_):
        pltpu.make_async_copy(h_hbm.at[pl.ds(0, tm)], xbuf.at[s_], gsem.at[s_]).wait()

    def wait_scatter(s_):
        pltpu.make_async_copy(ybuf.at[s_], y_hbm.at[pl.ds(0, tm)], ssem.at[s_]).wait()

    def row_copies(step):
        for k in range(share):
            r = step * share + k
            gather_copy(next_ref, r, other).start(priority=1)
            scatter_copy(prev_dst_ref, r, other).start(priority=1)

    @pl.when(i < n_used)
    def _():
        @pl.when(s == 0)
        def _():
            @pl.when(i == 0)
            def _():
                issue_all(lambda r: gather_copy(first_ref, r, 0))
                ybuf[1] = jnp.zeros(ybuf.shape[1:], ybuf.dtype)
                for half in range(2):
                    spare = pltpu.make_async_copy(
                        ybuf.at[1], y_hbm.at[pl.ds(y_hbm.shape[0] - (3 - half) * tm, tm)],
                        ssem.at[1])
                    spare.start()
                    spare.wait()

            wait_gather(slot)
            h_scr[...] = xbuf[slot].reshape(h_scr.shape)
            acc_scr[...] = jnp.zeros_like(acc_scr)
            row_copies(0)
            _swiglu_stage(h_scr, act[0], wg_ref, wu_ref)

        for parity in range(2):
            @pl.when((s > 0) & (s < n_f) & (s % 2 == parity))
            def _():
                row_copies(s)
                acc_scr[...] += jnp.dot(act[1 - parity][...], wd_ref[...],
                                        preferred_element_type=jnp.float32)
                _swiglu_stage(h_scr, act[parity], wg_ref, wu_ref)

        @pl.when(s == n_f)
        def _():
            part = jnp.dot(act[(n_f - 1) % 2][...], wd_ref[...],
                           preferred_element_type=jnp.float32)

            @pl.when(i >= 1)
            def _():
                wait_scatter(slot)

            ybuf[slot] = (acc_scr[...] + part).reshape(ybuf.shape[1:])

            @pl.when(i == n_used - 1)
            def _():
                issue_all(lambda r: scatter_copy(dst_ref, r, slot))
                wait_scatter(slot)
                wait_scatter(other)
                wait_gather(other)


def _experts(h_rows, wg, wu, wd, layer, block_expert, n_used, row_tok, row_dst, n_out, *, tm, tf):
    tile = h_rows.shape[1:]
    n_f, d = wg.shape[2], wg.shape[3]
    rows = row_tok.shape[0]
    assert rows % tm == 0 and row_dst.shape[0] == rows + tm
    row_tok, row_dst = _index_blocks(row_tok, tm), _index_blocks(row_dst, tm)
    assert wg.shape[4] == tf and wd.shape[2:] == (n_f * tf, d)
    assert tm % (n_f * 2 * DMA_THREADS) == 0

    def blk(i, n):
        return jnp.minimum(i, n[0] - 1)

    def up_col(i, s, n):
        return jnp.where(i < n[0], jnp.minimum(s, n_f - 1), n_f - 1)

    def down_row(i, s, n):
        return jnp.where(i < n[0], jnp.maximum(s - 1, 0), n_f - 1)

    smem = functools.partial(pl.BlockSpec, memory_space=pltpu.SMEM)
    grid_spec = pltpu.PrefetchScalarGridSpec(
        num_scalar_prefetch=2,
        grid=(rows // tm, n_f + 1),
        in_specs=[smem((None, None, tm), lambda i, f, e, n: (0, 0, 0)),
                  smem((None, None, tm), lambda i, f, e, n: (blk(i + 1, n), 0, 0)),
                  smem((None, None, tm), lambda i, f, e, n: (blk(i, n), 0, 0)),
                  smem((None, None, tm), lambda i, f, e, n: (blk(i, n) + 1, 0, 0)),
                  pl.BlockSpec(memory_space=pl.ANY),
                  pl.BlockSpec((None, None, None, d, tf),
                               lambda i, f, e, n: (layer, e[blk(i, n)], up_col(i, f, n), 0, 0)),
                  pl.BlockSpec((None, None, None, d, tf),
                               lambda i, f, e, n: (layer, e[blk(i, n)], up_col(i, f, n), 0, 0)),
                  pl.BlockSpec((None, None, tf, d),
                               lambda i, f, e, n: (layer, e[blk(i, n)], down_row(i, f, n), 0))],
        out_specs=pl.BlockSpec(memory_space=pl.ANY),
        scratch_shapes=[pltpu.VMEM((2, tm) + tile, jnp.bfloat16),
                        pltpu.VMEM((2, tm) + tile, jnp.float32),
                        pltpu.VMEM((tm, d), jnp.bfloat16),
                        pltpu.VMEM((tm, d), jnp.float32),
                        pltpu.VMEM((tm, tf), jnp.bfloat16),
                        pltpu.VMEM((tm, tf), jnp.bfloat16),
                        pltpu.SemaphoreType.DMA((2,)),
                        pltpu.SemaphoreType.DMA((2,))],
    )
    return pl.pallas_call(
        functools.partial(_experts_kernel, tm=tm, n_f=n_f),
        grid_spec=grid_spec,
        out_shape=jax.ShapeDtypeStruct((n_out,) + tile, jnp.float32),
        compiler_params=_params(("arbitrary", "arbitrary")),
        name="experts",
    )(block_expert, n_used, row_tok, row_tok, row_dst, row_dst, h_rows, wg, wu, wd)


def _index_blocks(idx, tm):
    return idx.reshape(-1, 1, tm)


def _mix_kernel(x_ref, gate_ref, y0_ref, y1_ref, *out_refs, n_head):
    gates = gate_ref[...]
    shape = x_ref.shape
    y = gates[:, 0:1] * y0_ref[...].reshape(shape) + gates[:, 1:2] * y1_ref[...].reshape(shape)
    out = x_ref[...] + y
    if len(out_refs) == 1:
        out_refs[0][...] = out
    else:
        @pl.when(pl.program_id(0) < n_head)
        def _():
            out_refs[0][...] = out

        @pl.when(pl.program_id(0) >= n_head)
        def _():
            out_refs[1][...] = out


def _mix(x, gates, y, split_at=None):
    t, d = x.shape
    rows = COMBINE_ROWS
    tile = y.shape[1:]
    assert t % rows == 0
    n_t = t // rows
    if split_at is None:
        n_head = 0
        out_specs = pl.BlockSpec((rows, d), lambda i: (i, 0))
        out_shape = jax.ShapeDtypeStruct((t, d), jnp.float32)
    else:
        assert split_at % rows == 0 and 0 < split_at < t
        n_head = split_at // rows
        out_specs = [pl.BlockSpec((rows, d), lambda i: (jnp.minimum(i, n_head - 1), 0)),
                     pl.BlockSpec((rows, d), lambda i: (jnp.maximum(i - n_head, 0), 0))]
        out_shape = [jax.ShapeDtypeStruct((split_at, d), jnp.float32),
                     jax.ShapeDtypeStruct((t - split_at, d), jnp.float32)]
    return pl.pallas_call(
        functools.partial(_mix_kernel, n_head=n_head),
        grid=(n_t,),
        in_specs=[pl.BlockSpec((rows, d), lambda i: (i, 0)),
                  pl.BlockSpec((rows, LANES), lambda i: (i, 0)),
                  pl.BlockSpec((rows,) + tile, lambda i: (i, 0, 0)),
                  pl.BlockSpec((rows,) + tile, lambda i: (n_t + i, 0, 0))],
        out_specs=out_specs,
        out_shape=out_shape,
        compiler_params=_params(("arbitrary",)),
        name="mix",
    )(x, gates, y, y)


def _moe(x, g, w_router, b_router, wg, wu, wd, layer, split_at=None):
    t, d = x.shape
    n_e = wg.shape[1]
    n_assign = t * TOP_K
    top_idx, gates, h_rows = _route(x, g, w_router, b_router)
    flat_e = top_idx[:, :TOP_K].reshape(-1)
    onehot = (flat_e[:, None] == jnp.arange(n_e, dtype=jnp.int32)[None, :]).astype(jnp.int32)
    rank = jnp.sum((jnp.cumsum(onehot, axis=0) - onehot) * onehot, axis=1)
    counts = jnp.sum(onehot, axis=0)
    padded = (counts + MOE_ROWS - 1) // MOE_ROWS * MOE_ROWS
    ends = jnp.cumsum(padded)
    dest = jnp.sum((ends - padded)[None, :] * onehot, axis=1) + rank
    n_blocks = (n_assign + n_e * (MOE_ROWS - 1) + MOE_ROWS - 1) // MOE_ROWS
    rows = n_blocks * MOE_ROWS
    row_assign = jnp.full((rows,), -1, jnp.int32).at[dest].set(jnp.arange(n_assign, dtype=jnp.int32))
    real = row_assign >= 0
    row_tok = jnp.where(real, row_assign // TOP_K, 0)
    spare = n_assign + (jnp.arange(rows, dtype=jnp.int32) // MOE_ROWS % 2) * MOE_ROWS \
        + jnp.arange(rows, dtype=jnp.int32) % MOE_ROWS
    row_dst = jnp.where(real, (row_assign % TOP_K) * t + row_assign // TOP_K, spare)
    row_dst = jnp.concatenate(
        [n_assign + 2 * MOE_ROWS + jnp.arange(MOE_ROWS, dtype=jnp.int32), row_dst])
    block_start = jnp.arange(n_blocks, dtype=jnp.int32) * MOE_ROWS
    block_expert = jnp.minimum(
        jnp.sum((ends[None, :] <= block_start[:, None]).astype(jnp.int32), axis=1), n_e - 1)
    n_used = (ends[-1:] // MOE_ROWS).astype(jnp.int32)
    y = _experts(h_rows, wg, wu, wd, layer, block_expert, n_used, row_tok, row_dst,
                 n_assign + 3 * MOE_ROWS, tm=MOE_ROWS, tf=MOE_COL_TILE)
    return _mix(x, gates, y, split_at)


def _rope_tables(pos):
    half = HEAD_DIM // 2
    inv_freq = 1.0 / (ROPE_THETA ** (jnp.arange(0, HEAD_DIM, 2, dtype=jnp.float32) / HEAD_DIM))
    ang = pos.astype(jnp.float32)[:, None] * inv_freq[None, :]
    cos, sin = jnp.cos(ang), jnp.sin(ang)
    reps = LANES // half
    signs = jnp.tile(jnp.concatenate([-jnp.ones((half,)), jnp.ones((half,))]), HEADS_PER_VREG)
    return jnp.tile(cos, (1, reps)), jnp.tile(sin, (1, reps)) * signs[None, :].astype(jnp.float32)


def _rel_bias(table, lq, n_before):
    n_keys = n_before + lq
    p = n_keys + lq
    rel = n_before + (lq - 1) - jnp.arange(p)
    diag = table.astype(jnp.float32)[:, jnp.clip(rel, REL_MIN, REL_MAX) - REL_MIN]
    skew = jnp.tile(diag, (1, lq))[:, :lq * (p - 1)].reshape(-1, lq, p - 1)
    return skew[:, :, lq - 1:lq - 1 + n_keys]


def _pair_major(a):
    h = a.shape[0]
    a = a.reshape((h // GROUP, PAIRS, HEADS_PER_VREG) + a.shape[1:])
    return jnp.swapaxes(a, 1, 2)


def _prompt_bias(table, n_prev):
    pad = n_prev * CHUNK
    b = _pair_major(_rel_bias(table, CHUNK, pad))
    kvh = b.shape[0]
    b = jnp.transpose(b, (0, 1, 4, 2, 3)).reshape(kvh * HEADS_PER_VREG, pad + CHUNK, LANES)
    return jnp.pad(b, ((0, 0), (0, pad), (0, 0)))


def _prompt_sink(sink):
    s = _pair_major(sink.astype(jnp.float32))
    s = jnp.repeat(s[..., None], CHUNK, axis=-1)
    return s.reshape(-1, 1, LANES)


def _sample_bias(table, lq, n_cached):
    b = _rel_bias(table, lq, n_cached)
    return b.reshape(b.shape[0] // GROUP, GROUP * lq, n_cached + lq)


def _sample_sink(sink, lq):
    h = sink.shape[0]
    s = jnp.repeat(sink.astype(jnp.float32).reshape(h // GROUP, GROUP, 1), lq, axis=2)
    return s.reshape(h // GROUP, GROUP * lq, 1)


def _head_gain(g):
    return jnp.tile(g.astype(jnp.float32), HEADS_PER_VREG)[None, :]


def kernel(x_prompt, x_sample, cache_k_a, cache_v_a, cache_k_b, cache_v_b, g_attn, g_ffn, w_qkv_a, g_q_a, g_k_a, sink_a, w_o_a, g_kv_b, w_kv_b, g_k_b, w_q_b, g_q_b, rel_bias_b, w_o_b, w_gate_ffn, w_up_ffn, w_down_ffn, w_router, b_router, w_gate_moe, w_up_moe, w_down_moe):
    bf16 = jnp.bfloat16
    nb, seq, d = x_prompt.shape
    ns, seq_s, _ = x_sample.shape
    tp, ts = nb * seq, ns * seq_s
    t = tp + ts
    depth = g_attn.shape[0]
    n_a = w_qkv_a.shape[0]
    n_kv = cache_k_b.shape[2]
    kvw = n_kv * HEAD_DIM
    keep_a, keep_b = min(WINDOW_A, seq), min(BAND_B, seq)
    assert t % ROW_TILE == 0 and t % FFN_ROWS == 0

    x = (x_prompt.reshape(tp, d), x_sample.reshape(ts, d))
    pos = jnp.concatenate([jnp.tile(jnp.arange(seq), nb),
                           jnp.tile(PAST_LEN + jnp.arange(seq_s), ns)])
    cos, sin = _rope_tables(pos)
    row = lambda v: v.astype(jnp.float32)[None, :]

    def attend(q, kv_pad, k_f32, v_f32, cache_k, cache_v, n_prev, table, sink):
        bias_p = None if table is None else _prompt_bias(table, n_prev)
        sink_p = None if sink is None else _prompt_sink(sink)
        o_p = _attend_prompt(q, *kv_pad, bias_p, sink_p, nb=nb, seq=seq, n_prev=n_prev)
        n_cached = cache_k.shape[1]
        ks = jnp.concatenate([cache_k.reshape(ns, n_cached, kvw),
                              k_f32[tp:].reshape(ns, seq_s, kvw)], axis=1).astype(bf16)
        vs = jnp.concatenate([cache_v.reshape(ns, n_cached, kvw),
                              v_f32[tp:].reshape(ns, seq_s, kvw)], axis=1).astype(bf16)
        bias_s = None if table is None else _sample_bias(table, seq_s, n_cached)
        sink_s = None if sink is None else _sample_sink(sink, seq_s)
        return o_p, _attend_sample(q, ks, vs, bias_s, sink_s, row0=tp, lq=seq_s)

    def tail(c, n_keep):
        rows = jnp.stack([c[(b + 1) * seq - n_keep:(b + 1) * seq] for b in range(nb)])
        return rows.reshape(nb, n_keep, n_kv, HEAD_DIM)

    def rolled(cache, c):
        new = c[tp:].reshape(ns, seq_s, n_kv, HEAD_DIM)
        return jnp.concatenate([cache, new], axis=1)[:, seq_s:]

    ffn_w = [_column_tiles(w.astype(bf16)[:, None], COL_TILE) for w in (w_gate_ffn, w_up_ffn)]
    ffn_w.append(w_down_ffn.astype(bf16)[:, None])
    moe_w = [_column_tiles(w.astype(bf16), MOE_COL_TILE) for w in (w_gate_moe, w_up_moe)]
    moe_w.append(w_down_moe.astype(bf16))

    ka_p, va_p, ka_s, va_s = [], [], [], []
    kvb_pad = kb_f = vb_f = None
    for layer in range(depth):
        g_in = row(g_attn[layer])
        if layer < n_a:
            i = layer
            q, kl, kh, vl, vh, k_f32, v_f32 = _project(
                x, g_in, w_qkv_a[i].astype(bf16), _head_gain(g_q_a[i]), _head_gain(g_k_a[i]),
                cos, sin, nq=(w_qkv_a.shape[2] - 2 * kvw) // COL_TILE, has_kv=True, rope=True)
            o = attend(q, (kl, kh, vl, vh), k_f32, v_f32, cache_k_a[i], cache_v_a[i],
                       PREV_CHUNKS_A, None, sink_a[i])
            x = _out_proj(o, w_o_a[i].astype(bf16), x)
            ka_p.append(tail(k_f32, keep_a))
            va_p.append(tail(v_f32, keep_a))
            ka_s.append(rolled(cache_k_a[i], k_f32))
            va_s.append(rolled(cache_v_a[i], v_f32))
        else:
            jb = layer - n_a
            (q,) = _project(x, g_in, w_q_b[jb].astype(bf16), _head_gain(g_q_b[jb]), None,
                            None, None, nq=w_q_b.shape[2] // COL_TILE, has_kv=False, rope=False)
            o = attend(q, kvb_pad, kb_f, vb_f, cache_k_b, cache_v_b, PREV_CHUNKS_B,
                       rel_bias_b[jb], None)
            x = _out_proj(o, w_o_b[jb].astype(bf16), x)

        g_mid = row(g_ffn[layer])
        m = layer // 2
        if layer % 2 == 0:
            n_tiles = t // FFN_ROWS
            x = _swiglu_blocks(x, g_mid, *ffn_w, m, jnp.zeros((n_tiles,), jnp.int32),
                               jnp.full((1,), n_tiles, jnp.int32), tm=FFN_ROWS, tf=COL_TILE)
        else:
            x = _moe(x, g_mid, w_router[m], b_router[m], *moe_w, m,
                     split_at=tp if layer == depth - 1 else None)

        if layer == n_a - 1:
            *kvb_pad, kb_f, vb_f = _project(x, row(g_kv_b), w_kv_b.astype(bf16), None,
                                            _head_gain(g_k_b), None, None,
                                            nq=0, has_kv=True, rope=False)

    if not isinstance(x, tuple):
        x = (x[:tp], x[tp:])
    y_prompt = x[0].reshape(nb, seq, d)
    y_sample = x[1].reshape(ns, seq_s, d)
    return (y_prompt, y_sample, jnp.stack(ka_p), jnp.stack(va_p), tail(kb_f, keep_b),
            tail(vb_f, keep_b), jnp.stack(ka_s), jnp.stack(va_s),
            rolled(cache_k_b, kb_f), rolled(cache_v_b, vb_f))
```

```python
import functools

import jax
import jax.numpy as jnp
from jax import lax
from jax.experimental import pallas as pl
from jax.experimental.pallas import tpu as pltpu

CHUNK = 64
HEAD_DIM = 64
GROUP = 4
WINDOW_A = 128
PREV_CHUNKS_A = WINDOW_A // CHUNK
PREV_CHUNKS_B = 8
BAND_B = PREV_CHUNKS_B * CHUNK
REL_MIN = -(CHUNK - 1)
REL_MAX = 256
ROPE_THETA = 10000.0
TOP_K = 2
RMS_EPS = 1e-6
NEG_INF = -1e30
PAST_LEN = 1024

LANES = 128
HEADS_PER_VREG = LANES // HEAD_DIM
KV_GROUP_WIDTH = GROUP * HEAD_DIM
PAIRS = GROUP // HEADS_PER_VREG

ROW_TILE = 512
COL_TILE = 512
FFN_ROWS = 640
MOE_ROWS = 640
MOE_COL_TILE = 1024
COMBINE_ROWS = 256
ATTN_UNROLL_WIDE = 4
ATTN_UNROLL_NARROW = 8
DMA_THREADS = 2
VMEM_LIMIT = 56 * 1024 * 1024


def _params(semantics):
    return pltpu.CompilerParams(dimension_semantics=semantics, vmem_limit_bytes=VMEM_LIMIT)


def _rms_rows(x, g):
    ms = jnp.sum(x * x, axis=-1, keepdims=True) * (1.0 / x.shape[-1])
    return (x * lax.rsqrt(ms + RMS_EPS)) * g


def _head_norm(y, gain, cos, sin):
    lane = lax.broadcasted_iota(jnp.int32, y.shape, 1)
    low = lane < HEAD_DIM
    ss = y * y
    s_lo = jnp.sum(jnp.where(low, ss, 0.0), axis=-1, keepdims=True)
    s_hi = jnp.sum(jnp.where(low, 0.0, ss), axis=-1, keepdims=True)
    ms = jnp.where(low, s_lo, s_hi) * (1.0 / HEAD_DIM)
    yn = (y * lax.rsqrt(ms + RMS_EPS)) * gain
    if cos is None:
        return yn
    half = HEAD_DIM // 2
    fwd = pltpu.roll(yn, half, 1)
    bwd = pltpu.roll(yn, LANES - half, 1)
    rot = jnp.where((lane % HEAD_DIM) < half, bwd, fwd)
    return yn * cos + rot * sin


def _store_half_padded(lo_ref, hi_ref, c, z):
    lane = lax.broadcasted_iota(jnp.int32, z.shape, 1)
    low = lane < HEAD_DIM
    zr = pltpu.roll(z, HEAD_DIM, 1)
    dt = lo_ref.dtype
    a, b = 2 * c * LANES, (2 * c + 1) * LANES
    lo_ref[:, a:a + LANES] = jnp.where(low, z, 0.0).astype(dt)
    lo_ref[:, b:b + LANES] = jnp.where(low, zr, 0.0).astype(dt)
    hi_ref[:, a:a + LANES] = jnp.where(low, 0.0, zr).astype(dt)
    hi_ref[:, b:b + LANES] = jnp.where(low, 0.0, z).astype(dt)


def _row_specs(x, tm):
    def spec(cols, fn):
        return pl.BlockSpec((tm, cols), lambda i: (fn(i), 0))

    if not isinstance(x, tuple):
        return [spec(x.shape[1], lambda i: i)], 0
    head, tail = x
    assert head.shape[0] % tm == 0 and tail.shape[0] % tm == 0 and head.shape[1] == tail.shape[1]
    n_head = head.shape[0] // tm
    return [spec(head.shape[1], lambda i: jnp.minimum(i, n_head - 1)),
            spec(head.shape[1], lambda i: jnp.maximum(i - n_head, 0))], n_head


def _read_rows(x_refs, n_head):
    if len(x_refs) == 1:
        return x_refs[0][...]
    return jnp.where(pl.program_id(0) < n_head, x_refs[0][...], x_refs[1][...])


def _n_rows(x):
    return x[0].shape[0] + x[1].shape[0] if isinstance(x, tuple) else x.shape[0]


def _as_list(x):
    return list(x) if isinstance(x, tuple) else [x]


def _proj_kernel(*refs, nq, has_kv, rope, n_x, n_head):
    it = iter(refs)
    x_refs = [next(it) for _ in range(n_x)]
    g_ref, w_ref = next(it), next(it)
    gq_ref = next(it) if nq else None
    gk_ref = next(it) if has_kv else None
    cos_ref, sin_ref = (next(it), next(it)) if rope else (None, None)
    q_ref = next(it) if nq else None
    if has_kv:
        kl_ref, kh_ref, vl_ref, vh_ref, kf_ref, vf_ref = (next(it) for _ in range(6))
    h_scr = next(it)
    h_scr[...] = _rms_rows(_read_rows(x_refs, n_head), g_ref[...]).astype(h_scr.dtype)
    cos = cos_ref[...] if rope else None
    sin = sin_ref[...] if rope else None
    groups = COL_TILE // LANES
    for j in range(nq + (2 if has_kv else 0)):
        y = jnp.dot(h_scr[...], w_ref[j], preferred_element_type=jnp.float32)
        for c in range(groups):
            yc = y[:, c * LANES:(c + 1) * LANES]
            if j < nq:
                z = _head_norm(yc, gq_ref[...], cos, sin) * (HEAD_DIM ** -0.5)
                q_ref[:, (j * groups + c) * LANES:(j * groups + c + 1) * LANES] = z.astype(q_ref.dtype)
            elif j == nq:
                z = _head_norm(yc, gk_ref[...], cos, sin)
                kf_ref[:, c * LANES:(c + 1) * LANES] = z
                _store_half_padded(kl_ref, kh_ref, c, z)
            else:
                vf_ref[:, c * LANES:(c + 1) * LANES] = yc
                _store_half_padded(vl_ref, vh_ref, c, yc)


def _project(x, g, w, gq, gk, cos, sin, *, nq, has_kv, rope):
    t = _n_rows(x)
    d = w.shape[0]
    tm = ROW_TILE
    n_col = nq + (2 if has_kv else 0)
    assert w.shape == (d, n_col * COL_TILE) and t % tm == 0
    w = jnp.swapaxes(w.reshape(d, n_col, COL_TILE), 0, 1)
    row = lambda i: (i, 0)
    x_specs, n_head = _row_specs(x, tm)
    in_specs = x_specs + [pl.BlockSpec((1, d), lambda i: (0, 0)),
                          pl.BlockSpec((n_col, d, COL_TILE), lambda i: (0, 0, 0))]
    args = _as_list(x) + [g, w]
    vec = pl.BlockSpec((1, LANES), lambda i: (0, 0))
    if nq:
        in_specs.append(vec)
        args.append(gq)
    if has_kv:
        in_specs.append(vec)
        args.append(gk)
    if rope:
        in_specs += [pl.BlockSpec((tm, LANES), row)] * 2
        args += [cos, sin]
    out_shape, out_specs = [], []
    if nq:
        out_shape.append(jax.ShapeDtypeStruct((t, nq * COL_TILE), jnp.bfloat16))
        out_specs.append(pl.BlockSpec((tm, nq * COL_TILE), row))
    if has_kv:
        for _ in range(4):
            out_shape.append(jax.ShapeDtypeStruct((t, HEADS_PER_VREG * COL_TILE), jnp.bfloat16))
            out_specs.append(pl.BlockSpec((tm, HEADS_PER_VREG * COL_TILE), row))
        for _ in range(2):
            out_shape.append(jax.ShapeDtypeStruct((t, COL_TILE), jnp.float32))
            out_specs.append(pl.BlockSpec((tm, COL_TILE), row))
    return pl.pallas_call(
        functools.partial(_proj_kernel, nq=nq, has_kv=has_kv, rope=rope,
                          n_x=len(x_specs), n_head=n_head),
        grid=(t // tm,),
        in_specs=in_specs,
        out_specs=out_specs,
        out_shape=out_shape,
        scratch_shapes=[pltpu.VMEM((tm, d), jnp.bfloat16)],
        compiler_params=_params(("parallel",)),
        name="proj",
    )(*args)


def _attn_prompt_kernel(*refs, seq, band, n_prev, unroll, has_bias, has_sink):
    it = iter(refs)
    q_ref, kl_ref, kh_ref, vl_ref, vh_ref = (next(it) for _ in range(5))
    bias_ref = next(it) if has_bias else None
    sink_ref = next(it) if has_sink else None
    o_ref = next(it)
    s_scr = (next(it), next(it))
    p_scr = (next(it), next(it))
    n_scr = (next(it), next(it))
    row = lax.broadcasted_iota(jnp.int32, (band, LANES), 0)
    nt = (((1,), (1,)), ((), ()))
    tn = (((0,), (0,)), ((), ()))
    n_groups = seq // (CHUNK * unroll)
    n_masked = min(-(-n_prev // unroll), n_groups)

    def key_start(c):
        return pl.multiple_of(jnp.maximum(c - n_prev, 0) * CHUNK, CHUNK)

    def scores(gi, slot):
        for u in range(unroll):
            c = gi * unroll + u
            r0 = pl.multiple_of(c * CHUNK, CHUNK)
            k0 = key_start(c)
            qs = jnp.concatenate([q_ref[pl.ds(r0, CHUNK), p * LANES:(p + 1) * LANES]
                                  for p in range(PAIRS)], axis=0)
            for half, k_ref in enumerate((kl_ref, kh_ref)):
                s_scr[slot][u, half] = lax.dot_general(k_ref[pl.ds(k0, band), :], qs, nt,
                                                        preferred_element_type=jnp.float32)

    def softmax(gi, slot, masked):
        for u in range(unroll):
            c = gi * unroll + u
            shift = pl.multiple_of(jnp.maximum(n_prev - c, 0) * CHUNK, CHUNK) if masked else 0
            inv = []
            for half in range(HEADS_PER_VREG):
                s = s_scr[slot][u, half]
                if has_bias:
                    s = s + bias_ref[half, pl.ds(shift, band), :]
                if masked:
                    s = jnp.where(row < band - shift, s, NEG_INF)
                m = jnp.max(s, axis=0, keepdims=True)
                if has_sink:
                    m = jnp.maximum(m, sink_ref[half])
                e = jnp.exp(s - m)
                l = jnp.sum(e, axis=0, keepdims=True)
                if has_sink:
                    l = l + jnp.exp(sink_ref[half] - m)
                p_scr[slot][u, half] = e.astype(p_scr[slot].dtype)
                inv.append(jnp.broadcast_to(1.0 / l, (LANES, LANES)))
            n_scr[slot][u] = jnp.where(row[:LANES] < HEAD_DIM, inv[0], inv[1])

    def outputs(gi, slot):
        for u in range(unroll):
            c = gi * unroll + u
            r0 = pl.multiple_of(c * CHUNK, CHUNK)
            k0 = key_start(c)
            o = (lax.dot_general(p_scr[slot][u, 0], vl_ref[pl.ds(k0, band), :], tn,
                                 preferred_element_type=jnp.float32)
                 + lax.dot_general(p_scr[slot][u, 1], vh_ref[pl.ds(k0, band), :], tn,
                                   preferred_element_type=jnp.float32))
            o = o * n_scr[slot][u].T
            for p in range(PAIRS):
                o_ref[pl.ds(r0, CHUNK), p * LANES:(p + 1) * LANES] = (
                    o[p * CHUNK:(p + 1) * CHUNK].astype(o_ref.dtype))

    def stage(g, parity, masked):
        static = isinstance(g, int)
        if not static or g < n_groups:
            scores(g, parity)
        if not static or 0 <= g - 1 < n_groups:
            softmax(g - 1, 1 - parity, masked)
        if not static or 0 <= g - 2 < n_groups:
            outputs(g - 2, parity)

    first_loop = max(n_masked + 1, 2)
    first_loop += first_loop % 2
    n_pairs = max(n_groups - first_loop, 0) // 2
    for g in range(min(first_loop, n_groups)):
        stage(g, g % 2, masked=True)
    if n_pairs:
        def body(k, carry):
            g = first_loop + 2 * k
            stage(g, 0, masked=False)
            stage(g + 1, 1, masked=False)
            return carry
        lax.fori_loop(0, n_pairs, body, 0)
    for g in range(min(first_loop, n_groups) + 2 * n_pairs, n_groups + 2):
        stage(g, g % 2, masked=(g - 1) < n_masked)


def _attend_prompt(q, kl, kh, vl, vh, bias, sink, *, nb, seq, n_prev):
    qw = q.shape[1]
    n_kv = kl.shape[1] // LANES
    band = (n_prev + 1) * CHUNK
    unroll = ATTN_UNROLL_WIDE if n_prev > PREV_CHUNKS_A else ATTN_UNROLL_NARROW
    assert seq % (CHUNK * unroll) == 0 and seq >= band and seq >= 2 * CHUNK * unroll
    kv_spec = pl.BlockSpec((seq, LANES), lambda b, j: (b, j))
    in_specs = [pl.BlockSpec((seq, KV_GROUP_WIDTH), lambda b, j: (b, j))] + [kv_spec] * 4
    args = [q, kl, kh, vl, vh]
    if bias is not None:
        in_specs.append(pl.BlockSpec((HEADS_PER_VREG,) + bias.shape[1:], lambda b, j: (j, 0, 0)))
        args.append(bias)
    if sink is not None:
        in_specs.append(pl.BlockSpec((HEADS_PER_VREG, 1, LANES), lambda b, j: (j, 0, 0)))
        args.append(sink)
    return pl.pallas_call(
        functools.partial(_attn_prompt_kernel, seq=seq, band=band, n_prev=n_prev, unroll=unroll,
                          has_bias=bias is not None, has_sink=sink is not None),
        grid=(nb, n_kv),
        in_specs=in_specs,
        out_specs=pl.BlockSpec((seq, KV_GROUP_WIDTH), lambda b, j: (b, j)),
        out_shape=jax.ShapeDtypeStruct((nb * seq, qw), jnp.bfloat16),
        scratch_shapes=[pltpu.VMEM((unroll, HEADS_PER_VREG, band, LANES), jnp.float32)] * 2
        + [pltpu.VMEM((unroll, HEADS_PER_VREG, band, LANES), jnp.bfloat16)] * 2
        + [pltpu.VMEM((unroll, LANES, LANES), jnp.float32)] * 2,
        compiler_params=_params(("parallel", "arbitrary")),
        name="attn_prompt",
    )(*args)


def _attn_sample_kernel(*refs, lq, has_bias, has_sink, n_kv):
    it = iter(refs)
    q_ref, k_ref, v_ref = next(it), next(it), next(it)
    bias_ref = next(it) if has_bias else None
    sink_ref = next(it) if has_sink else None
    o_ref = next(it)
    for j in range(n_kv):
        qj = q_ref[:, j * KV_GROUP_WIDTH:(j + 1) * KV_GROUP_WIDTH]
        qcat = jnp.concatenate(
            [qj[:, g * HEAD_DIM:(g + 1) * HEAD_DIM] for g in range(GROUP)], axis=0)
        kj = k_ref[0, :, j * HEAD_DIM:(j + 1) * HEAD_DIM]
        vj = v_ref[0, :, j * HEAD_DIM:(j + 1) * HEAD_DIM]
        s = lax.dot_general(qcat, kj, (((1,), (1,)), ((), ())),
                            preferred_element_type=jnp.float32)
        if has_bias:
            s = s + bias_ref[j]
        m = jnp.max(s, axis=-1, keepdims=True)
        if has_sink:
            sink = sink_ref[j]
            m = jnp.maximum(m, sink)
        e = jnp.exp(s - m)
        l = jnp.sum(e, axis=-1, keepdims=True)
        if has_sink:
            l = l + jnp.exp(sink - m)
        p = (e * (1.0 / l)).astype(vj.dtype)
        o = jnp.dot(p, vj, preferred_element_type=jnp.float32)
        ocat = jnp.concatenate([o[g * lq:(g + 1) * lq] for g in range(GROUP)], axis=1)
        o_ref[:, j * KV_GROUP_WIDTH:(j + 1) * KV_GROUP_WIDTH] = ocat.astype(o_ref.dtype)


def _attend_sample(q, k, v, bias, sink, *, row0, lq):
    ns, lk, kw = k.shape
    qw = q.shape[1]
    assert row0 % lq == 0
    blk0 = row0 // lq
    in_specs = [pl.BlockSpec((lq, qw), lambda i: (blk0 + i, 0)),
                pl.BlockSpec((1, lk, kw), lambda i: (i, 0, 0)),
                pl.BlockSpec((1, lk, kw), lambda i: (i, 0, 0))]
    args = [q, k, v]
    if bias is not None:
        in_specs.append(pl.BlockSpec(bias.shape, lambda i: (0, 0, 0)))
        args.append(bias)
    if sink is not None:
        in_specs.append(pl.BlockSpec(sink.shape, lambda i: (0, 0, 0)))
        args.append(sink)
    return pl.pallas_call(
        functools.partial(_attn_sample_kernel, lq=lq, has_bias=bias is not None,
                          has_sink=sink is not None, n_kv=kw // HEAD_DIM),
        grid=(ns,),
        in_specs=in_specs,
        out_specs=pl.BlockSpec((lq, qw), lambda i: (i, 0)),
        out_shape=jax.ShapeDtypeStruct((ns * lq, qw), jnp.bfloat16),
        compiler_params=_params(("arbitrary",)),
        name="attn_sample",
    )(*args)


def _out_proj_kernel(*refs, n_o, o_head, n_x, x_head):
    o_refs, w_ref = refs[:n_o], refs[n_o]
    x_refs, y_ref = refs[n_o + 1:n_o + 1 + n_x], refs[n_o + 1 + n_x]
    y_ref[...] = _read_rows(x_refs, x_head) + jnp.dot(
        _read_rows(o_refs, o_head), w_ref[...], preferred_element_type=jnp.float32)


def _out_proj(o, w, x):
    t = _n_rows(x)
    d = w.shape[1]
    tm = ROW_TILE
    o_specs, o_head = _row_specs(o, tm)
    x_specs, x_head = _row_specs(x, tm)
    return pl.pallas_call(
        functools.partial(_out_proj_kernel, n_o=len(o_specs), o_head=o_head,
                          n_x=len(x_specs), x_head=x_head),
        grid=(t // tm,),
        in_specs=o_specs + [pl.BlockSpec(w.shape, lambda i: (0, 0))] + x_specs,
        out_specs=pl.BlockSpec((tm, d), lambda i: (i, 0)),
        out_shape=jax.ShapeDtypeStruct((t, d), jnp.float32),
        compiler_params=_params(("parallel",)),
        name="out_proj",
    )(*_as_list(o), w, *_as_list(x))


def _swiglu_stage(h_scr, act_ref, wg_ref, wu_ref):
    h = h_scr[...]
    gate = jnp.dot(h, wg_ref[...], preferred_element_type=jnp.float32)
    up = jnp.dot(h, wu_ref[...], preferred_element_type=jnp.float32)
    act_ref[...] = ((gate * jax.nn.sigmoid(gate)) * up).astype(act_ref.dtype)


def _ffn_kernel(e_ref, n_ref, x_ref, g_ref, wg_ref, wu_ref, wd_ref, y_ref, h_scr, acc_scr,
                act0, act1, *, n_f):
    s = pl.program_id(1)
    act = (act0, act1)

    @pl.when(pl.program_id(0) < n_ref[0])
    def _():
        @pl.when(s == 0)
        def _():
            h_scr[...] = _rms_rows(x_ref[...], g_ref[...]).astype(h_scr.dtype)
            acc_scr[...] = jnp.zeros_like(acc_scr)
            _swiglu_stage(h_scr, act[0], wg_ref, wu_ref)

        for parity in range(2):
            @pl.when((s > 0) & (s < n_f) & (s % 2 == parity))
            def _():
                acc_scr[...] += jnp.dot(act[1 - parity][...], wd_ref[...],
                                        preferred_element_type=jnp.float32)
                _swiglu_stage(h_scr, act[parity], wg_ref, wu_ref)

        @pl.when(s == n_f)
        def _():
            part = jnp.dot(act[(n_f - 1) % 2][...], wd_ref[...],
                           preferred_element_type=jnp.float32)
            y_ref[...] = x_ref[...] + (acc_scr[...] + part)

    @pl.when((pl.program_id(0) >= n_ref[0]) & (s == n_f))
    def _():
        y_ref[...] = jnp.zeros_like(y_ref)


def _swiglu_blocks(x, g, wg, wu, wd, layer, block_expert, n_used, *, tm, tf):
    r, d = x.shape
    ff = wg.shape[3]
    assert r % tm == 0 and ff % tf == 0
    n_f = ff // tf

    def blk(i, n):
        return jnp.minimum(i, n[0] - 1)

    def up_col(i, s, n):
        return jnp.where(i < n[0], jnp.minimum(s, n_f - 1), n_f - 1)

    def down_row(i, s, n):
        return jnp.where(i < n[0], jnp.maximum(s - 1, 0), n_f - 1)

    grid_spec = pltpu.PrefetchScalarGridSpec(
        num_scalar_prefetch=2,
        grid=(r // tm, n_f + 1),
        in_specs=[pl.BlockSpec((tm, d), lambda i, s, e, n: (blk(i, n), 0)),
                  pl.BlockSpec((1, d), lambda i, s, e, n: (0, 0)),
                  pl.BlockSpec((None, None, d, tf),
                               lambda i, s, e, n: (layer, e[blk(i, n)], 0, up_col(i, s, n))),
                  pl.BlockSpec((None, None, d, tf),
                               lambda i, s, e, n: (layer, e[blk(i, n)], 0, up_col(i, s, n))),
                  pl.BlockSpec((None, None, tf, d),
                               lambda i, s, e, n: (layer, e[blk(i, n)], down_row(i, s, n), 0))],
        out_specs=pl.BlockSpec((tm, d), lambda i, s, e, n: (i, 0)),
        scratch_shapes=[pltpu.VMEM((tm, d), jnp.bfloat16),
                        pltpu.VMEM((tm, d), jnp.float32),
                        pltpu.VMEM((tm, tf), jnp.bfloat16),
                        pltpu.VMEM((tm, tf), jnp.bfloat16)],
    )
    return pl.pallas_call(
        functools.partial(_ffn_kernel, n_f=n_f),
        grid_spec=grid_spec,
        out_shape=jax.ShapeDtypeStruct((r, d), jnp.float32),
        compiler_params=_params(("arbitrary", "arbitrary")),
        name="swiglu",
    )(block_expert, n_used, x, g, wg, wu, wd)


def _router_kernel(x_ref, g_ref, w_ref, b_ref, idx_ref, gate_ref, h_ref):
    h = _rms_rows(x_ref[...], g_ref[...]).astype(jnp.bfloat16)
    h_ref[...] = h.reshape(h_ref.shape)
    logits = jnp.dot(h, w_ref[...], preferred_element_type=jnp.float32) + b_ref[...]
    lane = lax.broadcasted_iota(jnp.int32, logits.shape, 1)
    m1 = jnp.max(logits, axis=-1, keepdims=True)
    e1 = jnp.min(jnp.where(logits == m1, lane, LANES), axis=-1, keepdims=True)
    rest = jnp.where(lane == e1, -jnp.inf, logits)
    m2 = jnp.max(rest, axis=-1, keepdims=True)
    e2 = jnp.min(jnp.where(rest == m2, lane, LANES), axis=-1, keepdims=True)
    t = jnp.exp(m2 - m1)
    denom = 1.0 + t
    idx_ref[...] = jnp.where(lane == 0, e1, jnp.where(lane == 1, e2, 0))
    gate_ref[...] = jnp.where(lane == 0, 1.0 / denom, jnp.where(lane == 1, t / denom, 0.0))


def _route(x, g, w_router, b_router):
    t, d = x.shape
    tm = ROW_TILE
    n_e = w_router.shape[1]
    w = jnp.zeros((d, LANES), jnp.bfloat16).at[:, :n_e].set(w_router.astype(jnp.bfloat16))
    b = jnp.full((1, LANES), -jnp.inf, jnp.float32).at[0, :n_e].set(b_router.astype(jnp.float32))
    return pl.pallas_call(
        _router_kernel,
        grid=(t // tm,),
        in_specs=[pl.BlockSpec((tm, d), lambda i: (i, 0)),
                  pl.BlockSpec((1, d), lambda i: (0, 0)),
                  pl.BlockSpec((d, LANES), lambda i: (0, 0)),
                  pl.BlockSpec((1, LANES), lambda i: (0, 0))],
        out_specs=[pl.BlockSpec((tm, LANES), lambda i: (i, 0))] * 2
        + [pl.BlockSpec((tm, d // LANES, LANES), lambda i: (i, 0, 0))],
        out_shape=[jax.ShapeDtypeStruct((t, LANES), jnp.int32),
                   jax.ShapeDtypeStruct((t, LANES), jnp.float32),
                   jax.ShapeDtypeStruct((t, d // LANES, LANES), jnp.bfloat16)],
        compiler_params=_params(("parallel",)),
        name="router",
    )(x, g, w, b)


def _experts_kernel(e_ref, n_ref, first_ref, next_ref, prev_dst_ref, dst_ref, h_hbm,
                    wg_ref, wu_ref, wd_ref, y_hbm, xbuf, ybuf, h_scr, acc_scr, act0, act1,
                    gsem, ssem, *, tm, n_f):
    i, s = pl.program_id(0), pl.program_id(1)
    act = (act0, act1)
    n_used = n_ref[0]
    slot = i % 2
    other = 1 - slot
    share = tm // n_f
    unroll = 2 * DMA_THREADS

    def gather_copy(idx_ref, r, to_slot):
        return pltpu.make_async_copy(h_hbm.at[idx_ref[r]], xbuf.at[to_slot, r], gsem.at[to_slot])

    def scatter_copy(idx_ref, r, from_slot):
        return pltpu.make_async_copy(ybuf.at[from_slot, r], y_hbm.at[idx_ref[r]],
                                     ssem.at[from_slot])

    def issue_all(copy_of_row):
        def issue(r4, carry):
            for k in range(unroll):
                copy_of_row(r4 * unroll + k).start(priority=k % DMA_THREADS)
            return carry
        lax.fori_loop(0, tm // unroll, issue, 0)

    def wait_gather(s_):
        pltpu.make_async_copy(h_hbm.at[pl.ds(0, tm)], xbuf.at[s_], gsem.at[s_]).wait()

    def wait_scatter(s_):
        pltpu.make_async_copy(ybuf.at[s_], y_hbm.at[pl.ds(0, tm)], ssem.at[s_]).wait()

    def row_copies(step):
        for k in range(share):
            r = step * share + k
            gather_copy(next_ref, r, other).start(priority=1)
            scatter_copy(prev_dst_ref, r, other).start(priority=1)

    @pl.when(i < n_used)
    def _():
        @pl.when(s == 0)
        def _():
            @pl.when(i == 0)
            def _():
                issue_all(lambda r: gather_copy(first_ref, r, 0))
                ybuf[1] = jnp.zeros(ybuf.shape[1:], ybuf.dtype)
                for half in range(2):
                    spare = pltpu.make_async_copy(
                        ybuf.at[1], y_hbm.at[pl.ds(y_hbm.shape[0] - (3 - half) * tm, tm)],
                        ssem.at[1])
                    spare.start()
                    spare.wait()

            wait_gather(slot)
            h_scr[...] = xbuf[slot].reshape(h_scr.shape)
            acc_scr[...] = jnp.zeros_like(acc_scr)
            row_copies(0)
            _swiglu_stage(h_scr, act[0], wg_ref, wu_ref)

        for parity in range(2):
            @pl.when((s > 0) & (s < n_f) & (s % 2 == parity))
            def _():
                row_copies(s)
                acc_scr[...] += jnp.dot(act[1 - parity][...], wd_ref[...],
                                        preferred_element_type=jnp.float32)
                _swiglu_stage(h_scr, act[parity], wg_ref, wu_ref)

        @pl.when(s == n_f)
        def _():
            part = jnp.dot(act[(n_f - 1) % 2][...], wd_ref[...],
                           preferred_element_type=jnp.float32)

            @pl.when(i >= 1)
            def _():
                wait_scatter(slot)

            ybuf[slot] = (acc_scr[...] + part).reshape(ybuf.shape[1:])

            @pl.when(i == n_used - 1)
            def _():
                issue_all(lambda r: scatter_copy(dst_ref, r, slot))
                wait_scatter(slot)
                wait_scatter(other)
                wait_gather(other)


def _experts(h_rows, wg, wu, wd, layer, block_expert, n_used, row_tok, row_dst, n_out, *, tm, tf):
    tile = h_rows.shape[1:]
    d, ff = wg.shape[2], wg.shape[3]
    rows = row_tok.shape[0]
    assert rows % tm == 0 and row_dst.shape[0] == rows + tm and ff % tf == 0
    row_tok, row_dst = _index_blocks(row_tok, tm), _index_blocks(row_dst, tm)
    n_f = ff // tf
    assert tm % (n_f * 2 * DMA_THREADS) == 0

    def blk(i, n):
        return jnp.minimum(i, n[0] - 1)

    def up_col(i, s, n):
        return jnp.where(i < n[0], jnp.minimum(s, n_f - 1), n_f - 1)

    def down_row(i, s, n):
        return jnp.where(i < n[0], jnp.maximum(s - 1, 0), n_f - 1)

    smem = functools.partial(pl.BlockSpec, memory_space=pltpu.SMEM)
    grid_spec = pltpu.PrefetchScalarGridSpec(
        num_scalar_prefetch=2,
        grid=(rows // tm, n_f + 1),
        in_specs=[smem((None, None, tm), lambda i, f, e, n: (0, 0, 0)),
                  smem((None, None, tm), lambda i, f, e, n: (blk(i + 1, n), 0, 0)),
                  smem((None, None, tm), lambda i, f, e, n: (blk(i, n), 0, 0)),
                  smem((None, None, tm), lambda i, f, e, n: (blk(i, n) + 1, 0, 0)),
                  pl.BlockSpec(memory_space=pl.ANY),
                  pl.BlockSpec((None, None, d, tf),
                               lambda i, f, e, n: (layer, e[blk(i, n)], 0, up_col(i, f, n))),
                  pl.BlockSpec((None, None, d, tf),
                               lambda i, f, e, n: (layer, e[blk(i, n)], 0, up_col(i, f, n))),
                  pl.BlockSpec((None, None, tf, d),
                               lambda i, f, e, n: (layer, e[blk(i, n)], down_row(i, f, n), 0))],
        out_specs=pl.BlockSpec(memory_space=pl.ANY),
        scratch_shapes=[pltpu.VMEM((2, tm) + tile, jnp.bfloat16),
                        pltpu.VMEM((2, tm) + tile, jnp.float32),
                        pltpu.VMEM((tm, d), jnp.bfloat16),
                        pltpu.VMEM((tm, d), jnp.float32),
                        pltpu.VMEM((tm, tf), jnp.bfloat16),
                        pltpu.VMEM((tm, tf), jnp.bfloat16),
                        pltpu.SemaphoreType.DMA((2,)),
                        pltpu.SemaphoreType.DMA((2,))],
    )
    return pl.pallas_call(
        functools.partial(_experts_kernel, tm=tm, n_f=n_f),
        grid_spec=grid_spec,
        out_shape=jax.ShapeDtypeStruct((n_out,) + tile, jnp.float32),
        compiler_params=_params(("arbitrary", "arbitrary")),
        name="experts",
    )(block_expert, n_used, row_tok, row_tok, row_dst, row_dst, h_rows, wg, wu, wd)


def _index_blocks(idx, tm):
    return idx.reshape(-1, 1, tm)


def _mix_kernel(x_ref, gate_ref, y0_ref, y1_ref, *out_refs, n_head):
    gates = gate_ref[...]
    shape = x_ref.shape
    y = gates[:, 0:1] * y0_ref[...].reshape(shape) + gates[:, 1:2] * y1_ref[...].reshape(shape)
    out = x_ref[...] + y
    if len(out_refs) == 1:
        out_refs[0][...] = out
    else:
        @pl.when(pl.program_id(0) < n_head)
        def _():
            out_refs[0][...] = out

        @pl.when(pl.program_id(0) >= n_head)
        def _():
            out_refs[1][...] = out


def _mix(x, gates, y, split_at=None):
    t, d = x.shape
    rows = COMBINE_ROWS
    tile = y.shape[1:]
    assert t % rows == 0
    n_t = t // rows
    if split_at is None:
        n_head = 0
        out_specs = pl.BlockSpec((rows, d), lambda i: (i, 0))
        out_shape = jax.ShapeDtypeStruct((t, d), jnp.float32)
    else:
        assert split_at % rows == 0 and 0 < split_at < t
        n_head = split_at // rows
        out_specs = [pl.BlockSpec((rows, d), lambda i: (jnp.minimum(i, n_head - 1), 0)),
                     pl.BlockSpec((rows, d), lambda i: (jnp.maximum(i - n_head, 0), 0))]
        out_shape = [jax.ShapeDtypeStruct((split_at, d), jnp.float32),
                     jax.ShapeDtypeStruct((t - split_at, d), jnp.float32)]
    return pl.pallas_call(
        functools.partial(_mix_kernel, n_head=n_head),
        grid=(n_t,),
        in_specs=[pl.BlockSpec((rows, d), lambda i: (i, 0)),
                  pl.BlockSpec((rows, LANES), lambda i: (i, 0)),
                  pl.BlockSpec((rows,) + tile, lambda i: (i, 0, 0)),
                  pl.BlockSpec((rows,) + tile, lambda i: (n_t + i, 0, 0))],
        out_specs=out_specs,
        out_shape=out_shape,
        compiler_params=_params(("arbitrary",)),
        name="mix",
    )(x, gates, y, y)


def _moe(x, g, w_router, b_router, wg, wu, wd, layer, split_at=None):
    t, d = x.shape
    n_e = wg.shape[1]
    n_assign = t * TOP_K
    top_idx, gates, h_rows = _route(x, g, w_router, b_router)
    flat_e = top_idx[:, :TOP_K].reshape(-1)
    onehot = (flat_e[:, None] == jnp.arange(n_e, dtype=jnp.int32)[None, :]).astype(jnp.int32)
    rank = jnp.sum((jnp.cumsum(onehot, axis=0) - onehot) * onehot, axis=1)
    counts = jnp.sum(onehot, axis=0)
    padded = (counts + MOE_ROWS - 1) // MOE_ROWS * MOE_ROWS
    ends = jnp.cumsum(padded)
    dest = jnp.sum((ends - padded)[None, :] * onehot, axis=1) + rank
    n_blocks = (n_assign + n_e * (MOE_ROWS - 1) + MOE_ROWS - 1) // MOE_ROWS
    rows = n_blocks * MOE_ROWS
    row_assign = jnp.full((rows,), -1, jnp.int32).at[dest].set(jnp.arange(n_assign, dtype=jnp.int32))
    real = row_assign >= 0
    row_tok = jnp.where(real, row_assign // TOP_K, 0)
    spare = n_assign + (jnp.arange(rows, dtype=jnp.int32) // MOE_ROWS % 2) * MOE_ROWS \
        + jnp.arange(rows, dtype=jnp.int32) % MOE_ROWS
    row_dst = jnp.where(real, (row_assign % TOP_K) * t + row_assign // TOP_K, spare)
    row_dst = jnp.concatenate(
        [n_assign + 2 * MOE_ROWS + jnp.arange(MOE_ROWS, dtype=jnp.int32), row_dst])
    block_start = jnp.arange(n_blocks, dtype=jnp.int32) * MOE_ROWS
    block_expert = jnp.minimum(
        jnp.sum((ends[None, :] <= block_start[:, None]).astype(jnp.int32), axis=1), n_e - 1)
    n_used = (ends[-1:] // MOE_ROWS).astype(jnp.int32)
    y = _experts(h_rows, wg, wu, wd, layer, block_expert, n_used, row_tok, row_dst,
                 n_assign + 3 * MOE_ROWS, tm=MOE_ROWS, tf=MOE_COL_TILE)
    return _mix(x, gates, y, split_at)


def _rope_tables(pos):
    half = HEAD_DIM // 2
    inv_freq = 1.0 / (ROPE_THETA ** (jnp.arange(0, HEAD_DIM, 2, dtype=jnp.float32) / HEAD_DIM))
    ang = pos.astype(jnp.float32)[:, None] * inv_freq[None, :]
    cos, sin = jnp.cos(ang), jnp.sin(ang)
    reps = LANES // half
    signs = jnp.tile(jnp.concatenate([-jnp.ones((half,)), jnp.ones((half,))]), HEADS_PER_VREG)
    return jnp.tile(cos, (1, reps)), jnp.tile(sin, (1, reps)) * signs[None, :].astype(jnp.float32)


def _rel_bias(table, lq, n_before):
    n_keys = n_before + lq
    p = n_keys + lq
    rel = n_before + (lq - 1) - jnp.arange(p)
    diag = table.astype(jnp.float32)[:, jnp.clip(rel, REL_MIN, REL_MAX) - REL_MIN]
    skew = jnp.tile(diag, (1, lq))[:, :lq * (p - 1)].reshape(-1, lq, p - 1)
    return skew[:, :, lq - 1:lq - 1 + n_keys]


def _pair_major(a):
    h = a.shape[0]
    a = a.reshape((h // GROUP, PAIRS, HEADS_PER_VREG) + a.shape[1:])
    return jnp.swapaxes(a, 1, 2)


def _prompt_bias(table, n_prev):
    pad = n_prev * CHUNK
    b = _pair_major(_rel_bias(table, CHUNK, pad))
    kvh = b.shape[0]
    b = jnp.transpose(b, (0, 1, 4, 2, 3)).reshape(kvh * HEADS_PER_VREG, pad + CHUNK, LANES)
    return jnp.pad(b, ((0, 0), (0, pad), (0, 0)))


def _prompt_sink(sink):
    s = _pair_major(sink.astype(jnp.float32))
    s = jnp.repeat(s[..., None], CHUNK, axis=-1)
    return s.reshape(-1, 1, LANES)


def _sample_bias(table, lq, n_cached):
    b = _rel_bias(table, lq, n_cached)
    return b.reshape(b.shape[0] // GROUP, GROUP * lq, n_cached + lq)


def _sample_sink(sink, lq):
    h = sink.shape[0]
    s = jnp.repeat(sink.astype(jnp.float32).reshape(h // GROUP, GROUP, 1), lq, axis=2)
    return s.reshape(h // GROUP, GROUP * lq, 1)


def _head_gain(g):
    return jnp.tile(g.astype(jnp.float32), HEADS_PER_VREG)[None, :]


def kernel(x_prompt, x_sample, cache_k_a, cache_v_a, cache_k_b, cache_v_b, g_attn, g_ffn, w_qkv_a, g_q_a, g_k_a, sink_a, w_o_a, g_kv_b, w_kv_b, g_k_b, w_q_b, g_q_b, rel_bias_b, w_o_b, w_gate_ffn, w_up_ffn, w_down_ffn, w_router, b_router, w_gate_moe, w_up_moe, w_down_moe):
    bf16 = jnp.bfloat16
    nb, seq, d = x_prompt.shape
    ns, seq_s, _ = x_sample.shape
    tp, ts = nb * seq, ns * seq_s
    t = tp + ts
    depth = g_attn.shape[0]
    n_a = w_qkv_a.shape[0]
    n_kv = cache_k_b.shape[2]
    kvw = n_kv * HEAD_DIM
    keep_a, keep_b = min(WINDOW_A, seq), min(BAND_B, seq)
    assert t % ROW_TILE == 0 and t % FFN_ROWS == 0

    x = (x_prompt.reshape(tp, d), x_sample.reshape(ts, d))
    pos = jnp.concatenate([jnp.tile(jnp.arange(seq), nb),
                           jnp.tile(PAST_LEN + jnp.arange(seq_s), ns)])
    cos, sin = _rope_tables(pos)
    row = lambda v: v.astype(jnp.float32)[None, :]

    def attend(q, kv_pad, k_f32, v_f32, cache_k, cache_v, n_prev, table, sink):
        bias_p = None if table is None else _prompt_bias(table, n_prev)
        sink_p = None if sink is None else _prompt_sink(sink)
        o_p = _attend_prompt(q, *kv_pad, bias_p, sink_p, nb=nb, seq=seq, n_prev=n_prev)
        n_cached = cache_k.shape[1]
        ks = jnp.concatenate([cache_k.reshape(ns, n_cached, kvw),
                              k_f32[tp:].reshape(ns, seq_s, kvw)], axis=1).astype(bf16)
        vs = jnp.concatenate([cache_v.reshape(ns, n_cached, kvw),
                              v_f32[tp:].reshape(ns, seq_s, kvw)], axis=1).astype(bf16)
        bias_s = None if table is None else _sample_bias(table, seq_s, n_cached)
        sink_s = None if sink is None else _sample_sink(sink, seq_s)
        return o_p, _attend_sample(q, ks, vs, bias_s, sink_s, row0=tp, lq=seq_s)

    def tail(c, n_keep):
        rows = jnp.stack([c[(b + 1) * seq - n_keep:(b + 1) * seq] for b in range(nb)])
        return rows.reshape(nb, n_keep, n_kv, HEAD_DIM)

    def rolled(cache, c):
        new = c[tp:].reshape(ns, seq_s, n_kv, HEAD_DIM)
        return jnp.concatenate([cache, new], axis=1)[:, seq_s:]

    ffn_w = [w.astype(bf16)[:, None] for w in (w_gate_ffn, w_up_ffn, w_down_ffn)]
    moe_w = [w.astype(bf16) for w in (w_gate_moe, w_up_moe, w_down_moe)]

    ka_p, va_p, ka_s, va_s = [], [], [], []
    kvb_pad = kb_f = vb_f = None
    for layer in range(depth):
        g_in = row(g_attn[layer])
        if layer < n_a:
            i = layer
            q, kl, kh, vl, vh, k_f32, v_f32 = _project(
                x, g_in, w_qkv_a[i].astype(bf16), _head_gain(g_q_a[i]), _head_gain(g_k_a[i]),
                cos, sin, nq=(w_qkv_a.shape[2] - 2 * kvw) // COL_TILE, has_kv=True, rope=True)
            o = attend(q, (kl, kh, vl, vh), k_f32, v_f32, cache_k_a[i], cache_v_a[i],
                       PREV_CHUNKS_A, None, sink_a[i])
            x = _out_proj(o, w_o_a[i].astype(bf16), x)
            ka_p.append(tail(k_f32, keep_a))
            va_p.append(tail(v_f32, keep_a))
            ka_s.append(rolled(cache_k_a[i], k_f32))
            va_s.append(rolled(cache_v_a[i], v_f32))
        else:
            jb = layer - n_a
            (q,) = _project(x, g_in, w_q_b[jb].astype(bf16), _head_gain(g_q_b[jb]), None,
                            None, None, nq=w_q_b.shape[2] // COL_TILE, has_kv=False, rope=False)
            o = attend(q, kvb_pad, kb_f, vb_f, cache_k_b, cache_v_b, PREV_CHUNKS_B,
                       rel_bias_b[jb], None)
            x = _out_proj(o, w_o_b[jb].astype(bf16), x)

        g_mid = row(g_ffn[layer])
        m = layer // 2
        if layer % 2 == 0:
            n_tiles = t // FFN_ROWS
            x = _swiglu_blocks(x, g_mid, *ffn_w, m, jnp.zeros((n_tiles,), jnp.int32),
                               jnp.full((1,), n_tiles, jnp.int32), tm=FFN_ROWS, tf=COL_TILE)
        else:
            x = _moe(x, g_mid, w_router[m], b_router[m], *moe_w, m,
                     split_at=tp if layer == depth - 1 else None)

        if layer == n_a - 1:
            *kvb_pad, kb_f, vb_f = _project(x, row(g_kv_b), w_kv_b.astype(bf16), None,
                                            _head_gain(g_k_b), None, None,
                                            nq=0, has_kv=True, rope=False)

    if not isinstance(x, tuple):
        x = (x[:tp], x[tp:])
    y_prompt = x[0].reshape(nb, seq, d)
    y_sample = x[1].reshape(ns, seq_s, d)
    return (y_prompt, y_sample, jnp.stack(ka_p), jnp.stack(va_p), tail(kb_f, keep_b),
            tail(vb_f, keep_b), jnp.stack(ka_s), jnp.stack(va_s),
            rolled(cache_k_b, kb_f), rolled(cache_v_b, vb_f))
```

```python
import functools

import jax
import jax.numpy as jnp
from jax import lax
from jax.experimental import pallas as pl
from jax.experimental.pallas import tpu as pltpu

CHUNK = 64
HEAD_DIM = 64
GROUP = 4
WINDOW_A = 128
PREV_CHUNKS_A = WINDOW_A // CHUNK
PREV_CHUNKS_B = 8
BAND_B = PREV_CHUNKS_B * CHUNK
REL_MIN = -(CHUNK - 1)
REL_MAX = 256
ROPE_THETA = 10000.0
TOP_K = 2
RMS_EPS = 1e-6
NEG_INF = -1e30
PAST_LEN = 1024

LANES = 128
HEADS_PER_VREG = LANES // HEAD_DIM
KV_GROUP_WIDTH = GROUP * HEAD_DIM
PAIRS = GROUP // HEADS_PER_VREG

ROW_TILE = 512
COL_TILE = 512
FFN_ROWS = 832
MOE_ROWS = 640
MOE_COL_TILE = 1024
COMBINE_ROWS = 256
ATTN_UNROLL_WIDE = 4
ATTN_UNROLL_NARROW = 8
DMA_THREADS = 2
VMEM_LIMIT = 56 * 1024 * 1024


def _params(semantics):
    return pltpu.CompilerParams(dimension_semantics=semantics, vmem_limit_bytes=VMEM_LIMIT)


def _rms_rows(x, g):
    ms = jnp.sum(x * x, axis=-1, keepdims=True) * (1.0 / x.shape[-1])
    return (x * lax.rsqrt(ms + RMS_EPS)) * g


def _head_norm(y, gain, cos, sin):
    lane = lax.broadcasted_iota(jnp.int32, y.shape, 1)
    low = lane < HEAD_DIM
    ss = y * y
    s_lo = jnp.sum(jnp.where(low, ss, 0.0), axis=-1, keepdims=True)
    s_hi = jnp.sum(jnp.where(low, 0.0, ss), axis=-1, keepdims=True)
    ms = jnp.where(low, s_lo, s_hi) * (1.0 / HEAD_DIM)
    yn = (y * lax.rsqrt(ms + RMS_EPS)) * gain
    if cos is None:
        return yn
    half = HEAD_DIM // 2
    fwd = pltpu.roll(yn, half, 1)
    bwd = pltpu.roll(yn, LANES - half, 1)
    rot = jnp.where((lane % HEAD_DIM) < half, bwd, fwd)
    return yn * cos + rot * sin


def _store_half_padded(lo_ref, hi_ref, c, z):
    lane = lax.broadcasted_iota(jnp.int32, z.shape, 1)
    low = lane < HEAD_DIM
    zr = pltpu.roll(z, HEAD_DIM, 1)
    dt = lo_ref.dtype
    a, b = 2 * c * LANES, (2 * c + 1) * LANES
    lo_ref[:, a:a + LANES] = jnp.where(low, z, 0.0).astype(dt)
    lo_ref[:, b:b + LANES] = jnp.where(low, zr, 0.0).astype(dt)
    hi_ref[:, a:a + LANES] = jnp.where(low, 0.0, zr).astype(dt)
    hi_ref[:, b:b + LANES] = jnp.where(low, 0.0, z).astype(dt)


def _row_specs(x, tm):
    def spec(cols, fn):
        return pl.BlockSpec((tm, cols), lambda i: (fn(i), 0))

    if not isinstance(x, tuple):
        return [spec(x.shape[1], lambda i: i)], 0
    head, tail = x
    assert head.shape[0] % tm == 0 and tail.shape[0] % tm == 0 and head.shape[1] == tail.shape[1]
    n_head = head.shape[0] // tm
    return [spec(head.shape[1], lambda i: jnp.minimum(i, n_head - 1)),
            spec(head.shape[1], lambda i: jnp.maximum(i - n_head, 0))], n_head


def _read_rows(x_refs, n_head):
    if len(x_refs) == 1:
        return x_refs[0][...]
    return jnp.where(pl.program_id(0) < n_head, x_refs[0][...], x_refs[1][...])


def _n_rows(x):
    return x[0].shape[0] + x[1].shape[0] if isinstance(x, tuple) else x.shape[0]


def _as_list(x):
    return list(x) if isinstance(x, tuple) else [x]


def _proj_kernel(*refs, nq, has_kv, rope, n_x, n_head):
    it = iter(refs)
    x_refs = [next(it) for _ in range(n_x)]
    g_ref, w_ref = next(it), next(it)
    gq_ref = next(it) if nq else None
    gk_ref = next(it) if has_kv else None
    cos_ref, sin_ref = (next(it), next(it)) if rope else (None, None)
    q_ref = next(it) if nq else None
    if has_kv:
        kl_ref, kh_ref, vl_ref, vh_ref, kf_ref, vf_ref = (next(it) for _ in range(6))
    h_scr = next(it)
    h_scr[...] = _rms_rows(_read_rows(x_refs, n_head), g_ref[...]).astype(h_scr.dtype)
    cos = cos_ref[...] if rope else None
    sin = sin_ref[...] if rope else None
    groups = COL_TILE // LANES
    for j in range(nq + (2 if has_kv else 0)):
        y = jnp.dot(h_scr[...], w_ref[j], preferred_element_type=jnp.float32)
        for c in range(groups):
            yc = y[:, c * LANES:(c + 1) * LANES]
            if j < nq:
                z = _head_norm(yc, gq_ref[...], cos, sin) * (HEAD_DIM ** -0.5)
                q_ref[:, (j * groups + c) * LANES:(j * groups + c + 1) * LANES] = z.astype(q_ref.dtype)
            elif j == nq:
                z = _head_norm(yc, gk_ref[...], cos, sin)
                kf_ref[:, c * LANES:(c + 1) * LANES] = z
                _store_half_padded(kl_ref, kh_ref, c, z)
            else:
                vf_ref[:, c * LANES:(c + 1) * LANES] = yc
                _store_half_padded(vl_ref, vh_ref, c, yc)


def _project(x, g, w, gq, gk, cos, sin, *, nq, has_kv, rope):
    t = _n_rows(x)
    d = w.shape[0]
    tm = ROW_TILE
    n_col = nq + (2 if has_kv else 0)
    assert w.shape == (d, n_col * COL_TILE) and t % tm == 0
    w = jnp.swapaxes(w.reshape(d, n_col, COL_TILE), 0, 1)
    row = lambda i: (i, 0)
    x_specs, n_head = _row_specs(x, tm)
    in_specs = x_specs + [pl.BlockSpec((1, d), lambda i: (0, 0)),
                          pl.BlockSpec((n_col, d, COL_TILE), lambda i: (0, 0, 0))]
    args = _as_list(x) + [g, w]
    vec = pl.BlockSpec((1, LANES), lambda i: (0, 0))
    if nq:
        in_specs.append(vec)
        args.append(gq)
    if has_kv:
        in_specs.append(vec)
        args.append(gk)
    if rope:
        in_specs += [pl.BlockSpec((tm, LANES), row)] * 2
        args += [cos, sin]
    out_shape, out_specs = [], []
    if nq:
        out_shape.append(jax.ShapeDtypeStruct((t, nq * COL_TILE), jnp.bfloat16))
        out_specs.append(pl.BlockSpec((tm, nq * COL_TILE), row))
    if has_kv:
        for _ in range(4):
            out_shape.append(jax.ShapeDtypeStruct((t, HEADS_PER_VREG * COL_TILE), jnp.bfloat16))
            out_specs.append(pl.BlockSpec((tm, HEADS_PER_VREG * COL_TILE), row))
        for _ in range(2):
            out_shape.append(jax.ShapeDtypeStruct((t, COL_TILE), jnp.float32))
            out_specs.append(pl.BlockSpec((tm, COL_TILE), row))
    return pl.pallas_call(
        functools.partial(_proj_kernel, nq=nq, has_kv=has_kv, rope=rope,
                          n_x=len(x_specs), n_head=n_head),
        grid=(t // tm,),
        in_specs=in_specs,
        out_specs=out_specs,
        out_shape=out_shape,
        scratch_shapes=[pltpu.VMEM((tm, d), jnp.bfloat16)],
        compiler_params=_params(("parallel",)),
        name="proj",
    )(*args)


def _attn_prompt_kernel(*refs, seq, band, n_prev, unroll, has_bias, has_sink):
    it = iter(refs)
    q_ref, kl_ref, kh_ref, vl_ref, vh_ref = (next(it) for _ in range(5))
    bias_ref = next(it) if has_bias else None
    sink_ref = next(it) if has_sink else None
    o_ref = next(it)
    s_scr = (next(it), next(it))
    p_scr = (next(it), next(it))
    n_scr = (next(it), next(it))
    row = lax.broadcasted_iota(jnp.int32, (band, LANES), 0)
    nt = (((1,), (1,)), ((), ()))
    tn = (((0,), (0,)), ((), ()))
    n_groups = seq // (CHUNK * unroll)
    n_masked = min(-(-n_prev // unroll), n_groups)

    def key_start(c):
        return pl.multiple_of(jnp.maximum(c - n_prev, 0) * CHUNK, CHUNK)

    def scores(gi, slot):
        for u in range(unroll):
            c = gi * unroll + u
            r0 = pl.multiple_of(c * CHUNK, CHUNK)
            k0 = key_start(c)
            qs = jnp.concatenate([q_ref[pl.ds(r0, CHUNK), p * LANES:(p + 1) * LANES]
                                  for p in range(PAIRS)], axis=0)
            for half, k_ref in enumerate((kl_ref, kh_ref)):
                s_scr[slot][u, half] = lax.dot_general(k_ref[pl.ds(k0, band), :], qs, nt,
                                                        preferred_element_type=jnp.float32)

    def softmax(gi, slot, masked):
        for u in range(unroll):
            c = gi * unroll + u
            shift = pl.multiple_of(jnp.maximum(n_prev - c, 0) * CHUNK, CHUNK) if masked else 0
            inv = []
            for half in range(HEADS_PER_VREG):
                s = s_scr[slot][u, half]
                if has_bias:
                    s = s + bias_ref[half, pl.ds(shift, band), :]
                if masked:
                    s = jnp.where(row < band - shift, s, NEG_INF)
                m = jnp.max(s, axis=0, keepdims=True)
                if has_sink:
                    m = jnp.maximum(m, sink_ref[half])
                e = jnp.exp(s - m)
                l = jnp.sum(e, axis=0, keepdims=True)
                if has_sink:
                    l = l + jnp.exp(sink_ref[half] - m)
                p_scr[slot][u, half] = e.astype(p_scr[slot].dtype)
                inv.append(jnp.broadcast_to(1.0 / l, (LANES, LANES)))
            n_scr[slot][u] = jnp.where(row[:LANES] < HEAD_DIM, inv[0], inv[1])

    def outputs(gi, slot):
        for u in range(unroll):
            c = gi * unroll + u
            r0 = pl.multiple_of(c * CHUNK, CHUNK)
            k0 = key_start(c)
            o = (lax.dot_general(p_scr[slot][u, 0], vl_ref[pl.ds(k0, band), :], tn,
                                 preferred_element_type=jnp.float32)
                 + lax.dot_general(p_scr[slot][u, 1], vh_ref[pl.ds(k0, band), :], tn,
                                   preferred_element_type=jnp.float32))
            o = o * n_scr[slot][u].T
            for p in range(PAIRS):
                o_ref[pl.ds(r0, CHUNK), p * LANES:(p + 1) * LANES] = (
                    o[p * CHUNK:(p + 1) * CHUNK].astype(o_ref.dtype))

    def stage(g, parity, masked):
        static = isinstance(g, int)
        if not static or g < n_groups:
            scores(g, parity)
        if not static or 0 <= g - 1 < n_groups:
            softmax(g - 1, 1 - parity, masked)
        if not static or 0 <= g - 2 < n_groups:
            outputs(g - 2, parity)

    first_loop = max(n_masked + 1, 2)
    first_loop += first_loop % 2
    n_pairs = max(n_groups - first_loop, 0) // 2
    for g in range(min(first_loop, n_groups)):
        stage(g, g % 2, masked=True)
    if n_pairs:
        def body(k, carry):
            g = first_loop + 2 * k
            stage(g, 0, masked=False)
            stage(g + 1, 1, masked=False)
            return carry
        lax.fori_loop(0, n_pairs, body, 0)
    for g in range(min(first_loop, n_groups) + 2 * n_pairs, n_groups + 2):
        stage(g, g % 2, masked=(g - 1) < n_masked)


def _attend_prompt(q, kl, kh, vl, vh, bias, sink, *, nb, seq, n_prev):
    qw = q.shape[1]
    n_kv = kl.shape[1] // LANES
    band = (n_prev + 1) * CHUNK
    unroll = ATTN_UNROLL_WIDE if n_prev > PREV_CHUNKS_A else ATTN_UNROLL_NARROW
    assert seq % (CHUNK * unroll) == 0 and seq >= band and seq >= 2 * CHUNK * unroll
    kv_spec = pl.BlockSpec((seq, LANES), lambda b, j: (b, j))
    in_specs = [pl.BlockSpec((seq, KV_GROUP_WIDTH), lambda b, j: (b, j))] + [kv_spec] * 4
    args = [q, kl, kh, vl, vh]
    if bias is not None:
        in_specs.append(pl.BlockSpec((HEADS_PER_VREG,) + bias.shape[1:], lambda b, j: (j, 0, 0)))
        args.append(bias)
    if sink is not None:
        in_specs.append(pl.BlockSpec((HEADS_PER_VREG, 1, LANES), lambda b, j: (j, 0, 0)))
        args.append(sink)
    return pl.pallas_call(
        functools.partial(_attn_prompt_kernel, seq=seq, band=band, n_prev=n_prev, unroll=unroll,
                          has_bias=bias is not None, has_sink=sink is not None),
        grid=(nb, n_kv),
        in_specs=in_specs,
        out_specs=pl.BlockSpec((seq, KV_GROUP_WIDTH), lambda b, j: (b, j)),
        out_shape=jax.ShapeDtypeStruct((nb * seq, qw), jnp.bfloat16),
        scratch_shapes=[pltpu.VMEM((unroll, HEADS_PER_VREG, band, LANES), jnp.float32)] * 2
        + [pltpu.VMEM((unroll, HEADS_PER_VREG, band, LANES), jnp.bfloat16)] * 2
        + [pltpu.VMEM((unroll, LANES, LANES), jnp.float32)] * 2,
        compiler_params=_params(("parallel", "arbitrary")),
        name="attn_prompt",
    )(*args)


def _attn_sample_kernel(*refs, lq, has_bias, has_sink, n_kv):
    it = iter(refs)
    q_ref, k_ref, v_ref = next(it), next(it), next(it)
    bias_ref = next(it) if has_bias else None
    sink_ref = next(it) if has_sink else None
    o_ref = next(it)
    for j in range(n_kv):
        qj = q_ref[:, j * KV_GROUP_WIDTH:(j + 1) * KV_GROUP_WIDTH]
        qcat = jnp.concatenate(
            [qj[:, g * HEAD_DIM:(g + 1) * HEAD_DIM] for g in range(GROUP)], axis=0)
        kj = k_ref[0, :, j * HEAD_DIM:(j + 1) * HEAD_DIM]
        vj = v_ref[0, :, j * HEAD_DIM:(j + 1) * HEAD_DIM]
        s = lax.dot_general(qcat, kj, (((1,), (1,)), ((), ())),
                            preferred_element_type=jnp.float32)
        if has_bias:
            s = s + bias_ref[j]
        m = jnp.max(s, axis=-1, keepdims=True)
        if has_sink:
            sink = sink_ref[j]
            m = jnp.maximum(m, sink)
        e = jnp.exp(s - m)
        l = jnp.sum(e, axis=-1, keepdims=True)
        if has_sink:
            l = l + jnp.exp(sink - m)
        p = (e * (1.0 / l)).astype(vj.dtype)
        o = jnp.dot(p, vj, preferred_element_type=jnp.float32)
        ocat = jnp.concatenate([o[g * lq:(g + 1) * lq] for g in range(GROUP)], axis=1)
        o_ref[:, j * KV_GROUP_WIDTH:(j + 1) * KV_GROUP_WIDTH] = ocat.astype(o_ref.dtype)


def _attend_sample(q, k, v, bias, sink, *, row0, lq):
    ns, lk, kw = k.shape
    qw = q.shape[1]
    assert row0 % lq == 0
    blk0 = row0 // lq
    in_specs = [pl.BlockSpec((lq, qw), lambda i: (blk0 + i, 0)),
                pl.BlockSpec((1, lk, kw), lambda i: (i, 0, 0)),
                pl.BlockSpec((1, lk, kw), lambda i: (i, 0, 0))]
    args = [q, k, v]
    if bias is not None:
        in_specs.append(pl.BlockSpec(bias.shape, lambda i: (0, 0, 0)))
        args.append(bias)
    if sink is not None:
        in_specs.append(pl.BlockSpec(sink.shape, lambda i: (0, 0, 0)))
        args.append(sink)
    return pl.pallas_call(
        functools.partial(_attn_sample_kernel, lq=lq, has_bias=bias is not None,
                          has_sink=sink is not None, n_kv=kw // HEAD_DIM),
        grid=(ns,),
        in_specs=in_specs,
        out_specs=pl.BlockSpec((lq, qw), lambda i: (i, 0)),
        out_shape=jax.ShapeDtypeStruct((ns * lq, qw), jnp.bfloat16),
        compiler_params=_params(("arbitrary",)),
        name="attn_sample",
    )(*args)


def _out_proj_kernel(*refs, n_o, o_head, n_x, x_head):
    o_refs, w_ref = refs[:n_o], refs[n_o]
    x_refs, y_ref = refs[n_o + 1:n_o + 1 + n_x], refs[n_o + 1 + n_x]
    y_ref[...] = _read_rows(x_refs, x_head) + jnp.dot(
        _read_rows(o_refs, o_head), w_ref[...], preferred_element_type=jnp.float32)


def _out_proj(o, w, x):
    t = _n_rows(x)
    d = w.shape[1]
    tm = ROW_TILE
    o_specs, o_head = _row_specs(o, tm)
    x_specs, x_head = _row_specs(x, tm)
    return pl.pallas_call(
        functools.partial(_out_proj_kernel, n_o=len(o_specs), o_head=o_head,
                          n_x=len(x_specs), x_head=x_head),
        grid=(t // tm,),
        in_specs=o_specs + [pl.BlockSpec(w.shape, lambda i: (0, 0))] + x_specs,
        out_specs=pl.BlockSpec((tm, d), lambda i: (i, 0)),
        out_shape=jax.ShapeDtypeStruct((t, d), jnp.float32),
        compiler_params=_params(("parallel",)),
        name="out_proj",
    )(*_as_list(o), w, *_as_list(x))


def _swiglu_stage(h_scr, act_ref, wg_ref, wu_ref):
    h = h_scr[...]
    gate = jnp.dot(h, wg_ref[...], preferred_element_type=jnp.float32)
    up = jnp.dot(h, wu_ref[...], preferred_element_type=jnp.float32)
    act_ref[...] = ((gate * jax.nn.sigmoid(gate)) * up).astype(act_ref.dtype)


def _ffn_kernel(e_ref, n_ref, x_ref, g_ref, wg_ref, wu_ref, wd_ref, y_ref, h_scr, acc_scr,
                act0, act1, *, n_f):
    s = pl.program_id(1)
    act = (act0, act1)

    @pl.when(pl.program_id(0) < n_ref[0])
    def _():
        @pl.when(s == 0)
        def _():
            h_scr[...] = _rms_rows(x_ref[...], g_ref[...]).astype(h_scr.dtype)
            acc_scr[...] = jnp.zeros_like(acc_scr)
            _swiglu_stage(h_scr, act[0], wg_ref, wu_ref)

        for parity in range(2):
            @pl.when((s > 0) & (s < n_f) & (s % 2 == parity))
            def _():
                acc_scr[...] += jnp.dot(act[1 - parity][...], wd_ref[...],
                                        preferred_element_type=jnp.float32)
                _swiglu_stage(h_scr, act[parity], wg_ref, wu_ref)

        @pl.when(s == n_f)
        def _():
            part = jnp.dot(act[(n_f - 1) % 2][...], wd_ref[...],
                           preferred_element_type=jnp.float32)
            y_ref[...] = x_ref[...] + (acc_scr[...] + part)

    @pl.when((pl.program_id(0) >= n_ref[0]) & (s == n_f))
    def _():
        y_ref[...] = jnp.zeros_like(y_ref)


def _swiglu_blocks(x, g, wg, wu, wd, layer, block_expert, n_used, *, tm, tf):
    r, d = x.shape
    ff = wg.shape[3]
    assert r % tm == 0 and ff % tf == 0
    n_f = ff // tf

    def blk(i, n):
        return jnp.minimum(i, n[0] - 1)

    def up_col(i, s, n):
        return jnp.where(i < n[0], jnp.minimum(s, n_f - 1), n_f - 1)

    def down_row(i, s, n):
        return jnp.where(i < n[0], jnp.maximum(s - 1, 0), n_f - 1)

    grid_spec = pltpu.PrefetchScalarGridSpec(
        num_scalar_prefetch=2,
        grid=(r // tm, n_f + 1),
        in_specs=[pl.BlockSpec((tm, d), lambda i, s, e, n: (blk(i, n), 0)),
                  pl.BlockSpec((1, d), lambda i, s, e, n: (0, 0)),
                  pl.BlockSpec((None, None, d, tf),
                               lambda i, s, e, n: (layer, e[blk(i, n)], 0, up_col(i, s, n))),
                  pl.BlockSpec((None, None, d, tf),
                               lambda i, s, e, n: (layer, e[blk(i, n)], 0, up_col(i, s, n))),
                  pl.BlockSpec((None, None, tf, d),
                               lambda i, s, e, n: (layer, e[blk(i, n)], down_row(i, s, n), 0))],
        out_specs=pl.BlockSpec((tm, d), lambda i, s, e, n: (i, 0)),
        scratch_shapes=[pltpu.VMEM((tm, d), jnp.bfloat16),
                        pltpu.VMEM((tm, d), jnp.float32),
                        pltpu.VMEM((tm, tf), jnp.bfloat16),
                        pltpu.VMEM((tm, tf), jnp.bfloat16)],
    )
    return pl.pallas_call(
        functools.partial(_ffn_kernel, n_f=n_f),
        grid_spec=grid_spec,
        out_shape=jax.ShapeDtypeStruct((r, d), jnp.float32),
        compiler_params=_params(("arbitrary", "arbitrary")),
        name="swiglu",
    )(block_expert, n_used, x, g, wg, wu, wd)


def _router_kernel(x_ref, g_ref, w_ref, b_ref, idx_ref, gate_ref, h_ref):
    h = _rms_rows(x_ref[...], g_ref[...]).astype(jnp.bfloat16)
    h_ref[...] = h.reshape(h_ref.shape)
    logits = jnp.dot(h, w_ref[...], preferred_element_type=jnp.float32) + b_ref[...]
    lane = lax.broadcasted_iota(jnp.int32, logits.shape, 1)
    m1 = jnp.max(logits, axis=-1, keepdims=True)
    e1 = jnp.min(jnp.where(logits == m1, lane, LANES), axis=-1, keepdims=True)
    rest = jnp.where(lane == e1, -jnp.inf, logits)
    m2 = jnp.max(rest, axis=-1, keepdims=True)
    e2 = jnp.min(jnp.where(rest == m2, lane, LANES), axis=-1, keepdims=True)
    t = jnp.exp(m2 - m1)
    denom = 1.0 + t
    idx_ref[...] = jnp.where(lane == 0, e1, jnp.where(lane == 1, e2, 0))
    gate_ref[...] = jnp.where(lane == 0, 1.0 / denom, jnp.where(lane == 1, t / denom, 0.0))


def _route(x, g, w_router, b_router):
    t, d = x.shape
    tm = ROW_TILE
    n_e = w_router.shape[1]
    w = jnp.zeros((d, LANES), jnp.bfloat16).at[:, :n_e].set(w_router.astype(jnp.bfloat16))
    b = jnp.full((1, LANES), -jnp.inf, jnp.float32).at[0, :n_e].set(b_router.astype(jnp.float32))
    return pl.pallas_call(
        _router_kernel,
        grid=(t // tm,),
        in_specs=[pl.BlockSpec((tm, d), lambda i: (i, 0)),
                  pl.BlockSpec((1, d), lambda i: (0, 0)),
                  pl.BlockSpec((d, LANES), lambda i: (0, 0)),
                  pl.BlockSpec((1, LANES), lambda i: (0, 0))],
        out_specs=[pl.BlockSpec((tm, LANES), lambda i: (i, 0))] * 2
        + [pl.BlockSpec((tm, d // LANES, LANES), lambda i: (i, 0, 0))],
        out_shape=[jax.ShapeDtypeStruct((t, LANES), jnp.int32),
                   jax.ShapeDtypeStruct((t, LANES), jnp.float32),
                   jax.ShapeDtypeStruct((t, d // LANES, LANES), jnp.bfloat16)],
        compiler_params=_params(("parallel",)),
        name="router",
    )(x, g, w, b)


def _experts_kernel(e_ref, n_ref, first_ref, next_ref, prev_dst_ref, dst_ref, h_hbm,
                    wg_ref, wu_ref, wd_ref, y_hbm, xbuf, ybuf, h_scr, acc_scr, act0, act1,
                    gsem, ssem, *, tm, n_f):
    i, s = pl.program_id(0), pl.program_id(1)
    act = (act0, act1)
    n_used = n_ref[0]
    slot = i % 2
    other = 1 - slot
    share = tm // n_f
    unroll = 2 * DMA_THREADS

    def gather_copy(idx_ref, r, to_slot):
        return pltpu.make_async_copy(h_hbm.at[idx_ref[r]], xbuf.at[to_slot, r], gsem.at[to_slot])

    def scatter_copy(idx_ref, r, from_slot):
        return pltpu.make_async_copy(ybuf.at[from_slot, r], y_hbm.at[idx_ref[r]],
                                     ssem.at[from_slot])

    def issue_all(copy_of_row):
        def issue(r4, carry):
            for k in range(unroll):
                copy_of_row(r4 * unroll + k).start(priority=k % DMA_THREADS)
            return carry
        lax.fori_loop(0, tm // unroll, issue, 0)

    def wait_gather(s_):
        pltpu.make_async_copy(h_hbm.at[pl.ds(0, tm)], xbuf.at[s_], gsem.at[s_]).wait()

    def wait_scatter(s_):
        pltpu.make_async_copy(ybuf.at[s_], y_hbm.at[pl.ds(0, tm)], ssem.at[s_]).wait()

    def row_copies(step):
        for k in range(share):
            r = step * share + k
            gather_copy(next_ref, r, other).start(priority=1)
            scatter_copy(prev_dst_ref, r, other).start(priority=1)

    @pl.when(i < n_used)
    def _():
        @pl.when(s == 0)
        def _():
            @pl.when(i == 0)
            def _():
                issue_all(lambda r: gather_copy(first_ref, r, 0))
                ybuf[1] = jnp.zeros(ybuf.shape[1:], ybuf.dtype)
                for half in range(2):
                    spare = pltpu.make_async_copy(
                        ybuf.at[1], y_hbm.at[pl.ds(y_hbm.shape[0] - (3 - half) * tm, tm)],
                        ssem.at[1])
                    spare.start()
                    spare.wait()

            wait_gather(slot)
            h_scr[...] = xbuf[slot].reshape(h_scr.shape)
            acc_scr[...] = jnp.zeros_like(acc_scr)
            row_copies(0)
            _swiglu_stage(h_scr, act[0], wg_ref, wu_ref)

        for parity in range(2):
            @pl.when((s > 0) & (s < n_f) & (s % 2 == parity))
            def _():
                row_copies(s)
                acc_scr[...] += jnp.dot(act[1 - parity][...], wd_ref[...],
                                        preferred_element_type=jnp.float32)
                _swiglu_stage(h_scr, act[parity], wg_ref, wu_ref)

        @pl.when(s == n_f)
        def _():
            part = jnp.dot(act[(n_f - 1) % 2][...], wd_ref[...],
                           preferred_element_type=jnp.float32)

            @pl.when(i >= 1)
            def _():
                wait_scatter(slot)

            ybuf[slot] = (acc_scr[...] + part).reshape(ybuf.shape[1:])

            @pl.when(i == n_used - 1)
            def _():
                issue_all(lambda r: scatter_copy(dst_ref, r, slot))
                wait_scatter(slot)
                wait_scatter(other)
                wait_gather(other)


def _experts(h_rows, wg, wu, wd, layer, block_expert, n_used, row_tok, row_dst, n_out, *, tm, tf):
    tile = h_rows.shape[1:]
    d, ff = wg.shape[2], wg.shape[3]
    rows = row_tok.shape[0]
    assert rows % tm == 0 and row_dst.shape[0] == rows + tm and ff % tf == 0
    row_tok, row_dst = _index_blocks(row_tok, tm), _index_blocks(row_dst, tm)
    n_f = ff // tf
    assert tm % (n_f * 2 * DMA_THREADS) == 0

    def blk(i, n):
        return jnp.minimum(i, n[0] - 1)

    def up_col(i, s, n):
        return jnp.where(i < n[0], jnp.minimum(s, n_f - 1), n_f - 1)

    def down_row(i, s, n):
        return jnp.where(i < n[0], jnp.maximum(s - 1, 0), n_f - 1)

    smem = functools.partial(pl.BlockSpec, memory_space=pltpu.SMEM)
    grid_spec = pltpu.PrefetchScalarGridSpec(
        num_scalar_prefetch=2,
        grid=(rows // tm, n_f + 1),
        in_specs=[smem((None, None, tm), lambda i, f, e, n: (0, 0, 0)),
                  smem((None, None, tm), lambda i, f, e, n: (blk(i + 1, n), 0, 0)),
                  smem((None, None, tm), lambda i, f, e, n: (blk(i, n), 0, 0)),
                  smem((None, None, tm), lambda i, f, e, n: (blk(i, n) + 1, 0, 0)),
                  pl.BlockSpec(memory_space=pl.ANY),
                  pl.BlockSpec((None, None, d, tf),
                               lambda i, f, e, n: (layer, e[blk(i, n)], 0, up_col(i, f, n))),
                  pl.BlockSpec((None, None, d, tf),
                               lambda i, f, e, n: (layer, e[blk(i, n)], 0, up_col(i, f, n))),
                  pl.BlockSpec((None, None, tf, d),
                               lambda i, f, e, n: (layer, e[blk(i, n)], down_row(i, f, n), 0))],
        out_specs=pl.BlockSpec(memory_space=pl.ANY),
        scratch_shapes=[pltpu.VMEM((2, tm) + tile, jnp.bfloat16),
                        pltpu.VMEM((2, tm) + tile, jnp.float32),
                        pltpu.VMEM((tm, d), jnp.bfloat16),
                        pltpu.VMEM((tm, d), jnp.float32),
                        pltpu.VMEM((tm, tf), jnp.bfloat16),
                        pltpu.VMEM((tm, tf), jnp.bfloat16),
                        pltpu.SemaphoreType.DMA((2,)),
                        pltpu.SemaphoreType.DMA((2,))],
    )
    return pl.pallas_call(
        functools.partial(_experts_kernel, tm=tm, n_f=n_f),
        grid_spec=grid_spec,
        out_shape=jax.ShapeDtypeStruct((n_out,) + tile, jnp.float32),
        compiler_params=_params(("arbitrary", "arbitrary")),
        name="experts",
    )(block_expert, n_used, row_tok, row_tok, row_dst, row_dst, h_rows, wg, wu, wd)


def _index_blocks(idx, tm):
    return idx.reshape(-1, 1, tm)


def _mix_kernel(x_ref, gate_ref, y0_ref, y1_ref, *out_refs, n_head):
    gates = gate_ref[...]
    shape = x_ref.shape
    y = gates[:, 0:1] * y0_ref[...].reshape(shape) + gates[:, 1:2] * y1_ref[...].reshape(shape)
    out = x_ref[...] + y
    if len(out_refs) == 1:
        out_refs[0][...] = out
    else:
        @pl.when(pl.program_id(0) < n_head)
        def _():
            out_refs[0][...] = out

        @pl.when(pl.program_id(0) >= n_head)
        def _():
            out_refs[1][...] = out


def _mix(x, gates, y, split_at=None):
    t, d = x.shape
    rows = COMBINE_ROWS
    tile = y.shape[1:]
    assert t % rows == 0
    n_t = t // rows
    if split_at is None:
        n_head = 0
        out_specs = pl.BlockSpec((rows, d), lambda i: (i, 0))
        out_shape = jax.ShapeDtypeStruct((t, d), jnp.float32)
    else:
        assert split_at % rows == 0 and 0 < split_at < t
        n_head = split_at // rows
        out_specs = [pl.BlockSpec((rows, d), lambda i: (jnp.minimum(i, n_head - 1), 0)),
                     pl.BlockSpec((rows, d), lambda i: (jnp.maximum(i - n_head, 0), 0))]
        out_shape = [jax.ShapeDtypeStruct((split_at, d), jnp.float32),
                     jax.ShapeDtypeStruct((t - split_at, d), jnp.float32)]
    return pl.pallas_call(
        functools.partial(_mix_kernel, n_head=n_head),
        grid=(n_t,),
        in_specs=[pl.BlockSpec((rows, d), lambda i: (i, 0)),
                  pl.BlockSpec((rows, LANES), lambda i: (i, 0)),
                  pl.BlockSpec((rows,) + tile, lambda i: (i, 0, 0)),
                  pl.BlockSpec((rows,) + tile, lambda i: (n_t + i, 0, 0))],
        out_specs=out_specs,
        out_shape=out_shape,
        compiler_params=_params(("arbitrary",)),
        name="mix",
    )(x, gates, y, y)


def _moe(x, g, w_router, b_router, wg, wu, wd, layer, split_at=None):
    t, d = x.shape
    n_e = wg.shape[1]
    n_assign = t * TOP_K
    top_idx, gates, h_rows = _route(x, g, w_router, b_router)
    flat_e = top_idx[:, :TOP_K].reshape(-1)
    onehot = (flat_e[:, None] == jnp.arange(n_e, dtype=jnp.int32)[None, :]).astype(jnp.int32)
    rank = jnp.sum((jnp.cumsum(onehot, axis=0) - onehot) * onehot, axis=1)
    counts = jnp.sum(onehot, axis=0)
    padded = (counts + MOE_ROWS - 1) // MOE_ROWS * MOE_ROWS
    ends = jnp.cumsum(padded)
    dest = jnp.sum((ends - padded)[None, :] * onehot, axis=1) + rank
    n_blocks = (n_assign + n_e * (MOE_ROWS - 1) + MOE_ROWS - 1) // MOE_ROWS
    rows = n_blocks * MOE_ROWS
    row_assign = jnp.full((rows,), -1, jnp.int32).at[dest].set(jnp.arange(n_assign, dtype=jnp.int32))
    real = row_assign >= 0
    row_tok = jnp.where(real, row_assign // TOP_K, 0)
    spare = n_assign + (jnp.arange(rows, dtype=jnp.int32) // MOE_ROWS % 2) * MOE_ROWS \
        + jnp.arange(rows, dtype=jnp.int32) % MOE_ROWS
    row_dst = jnp.where(real, (row_assign % TOP_K) * t + row_assign // TOP_K, spare)
    row_dst = jnp.concatenate(
        [n_assign + 2 * MOE_ROWS + jnp.arange(MOE_ROWS, dtype=jnp.int32), row_dst])
    block_start = jnp.arange(n_blocks, dtype=jnp.int32) * MOE_ROWS
    block_expert = jnp.minimum(
        jnp.sum((ends[None, :] <= block_start[:, None]).astype(jnp.int32), axis=1), n_e - 1)
    n_used = (ends[-1:] // MOE_ROWS).astype(jnp.int32)
    y = _experts(h_rows, wg, wu, wd, layer, block_expert, n_used, row_tok, row_dst,
                 n_assign + 3 * MOE_ROWS, tm=MOE_ROWS, tf=MOE_COL_TILE)
    return _mix(x, gates, y, split_at)


def _rope_tables(pos):
    half = HEAD_DIM // 2
    inv_freq = 1.0 / (ROPE_THETA ** (jnp.arange(0, HEAD_DIM, 2, dtype=jnp.float32) / HEAD_DIM))
    ang = pos.astype(jnp.float32)[:, None] * inv_freq[None, :]
    cos, sin = jnp.cos(ang), jnp.sin(ang)
    reps = LANES // half
    signs = jnp.tile(jnp.concatenate([-jnp.ones((half,)), jnp.ones((half,))]), HEADS_PER_VREG)
    return jnp.tile(cos, (1, reps)), jnp.tile(sin, (1, reps)) * signs[None, :].astype(jnp.float32)


def _rel_bias(table, lq, n_before):
    n_keys = n_before + lq
    p = n_keys + lq
    rel = n_before + (lq - 1) - jnp.arange(p)
    diag = table.astype(jnp.float32)[:, jnp.clip(rel, REL_MIN, REL_MAX) - REL_MIN]
    skew = jnp.tile(diag, (1, lq))[:, :lq * (p - 1)].reshape(-1, lq, p - 1)
    return skew[:, :, lq - 1:lq - 1 + n_keys]


def _pair_major(a):
    h = a.shape[0]
    a = a.reshape((h // GROUP, PAIRS, HEADS_PER_VREG) + a.shape[1:])
    return jnp.swapaxes(a, 1, 2)


def _prompt_bias(table, n_prev):
    pad = n_prev * CHUNK
    b = _pair_major(_rel_bias(table, CHUNK, pad))
    kvh = b.shape[0]
    b = jnp.transpose(b, (0, 1, 4, 2, 3)).reshape(kvh * HEADS_PER_VREG, pad + CHUNK, LANES)
    return jnp.pad(b, ((0, 0), (0, pad), (0, 0)))


def _prompt_sink(sink):
    s = _pair_major(sink.astype(jnp.float32))
    s = jnp.repeat(s[..., None], CHUNK, axis=-1)
    return s.reshape(-1, 1, LANES)


def _sample_bias(table, lq, n_cached):
    b = _rel_bias(table, lq, n_cached)
    return b.reshape(b.shape[0] // GROUP, GROUP * lq, n_cached + lq)


def _sample_sink(sink, lq):
    h = sink.shape[0]
    s = jnp.repeat(sink.astype(jnp.float32).reshape(h // GROUP, GROUP, 1), lq, axis=2)
    return s.reshape(h // GROUP, GROUP * lq, 1)


def _head_gain(g):
    return jnp.tile(g.astype(jnp.float32), HEADS_PER_VREG)[None, :]


def kernel(x_prompt, x_sample, cache_k_a, cache_v_a, cache_k_b, cache_v_b, g_attn, g_ffn, w_qkv_a, g_q_a, g_k_a, sink_a, w_o_a, g_kv_b, w_kv_b, g_k_b, w_q_b, g_q_b, rel_bias_b, w_o_b, w_gate_ffn, w_up_ffn, w_down_ffn, w_router, b_router, w_gate_moe, w_up_moe, w_down_moe):
    bf16 = jnp.bfloat16
    nb, seq, d = x_prompt.shape
    ns, seq_s, _ = x_sample.shape
    tp, ts = nb * seq, ns * seq_s
    t = tp + ts
    depth = g_attn.shape[0]
    n_a = w_qkv_a.shape[0]
    n_kv = cache_k_b.shape[2]
    kvw = n_kv * HEAD_DIM
    keep_a, keep_b = min(WINDOW_A, seq), min(BAND_B, seq)
    assert t % ROW_TILE == 0 and t % FFN_ROWS == 0

    x = (x_prompt.reshape(tp, d), x_sample.reshape(ts, d))
    pos = jnp.concatenate([jnp.tile(jnp.arange(seq), nb),
                           jnp.tile(PAST_LEN + jnp.arange(seq_s), ns)])
    cos, sin = _rope_tables(pos)
    row = lambda v: v.astype(jnp.float32)[None, :]

    def attend(q, kv_pad, k_f32, v_f32, cache_k, cache_v, n_prev, table, sink):
        bias_p = None if table is None else _prompt_bias(table, n_prev)
        sink_p = None if sink is None else _prompt_sink(sink)
        o_p = _attend_prompt(q, *kv_pad, bias_p, sink_p, nb=nb, seq=seq, n_prev=n_prev)
        n_cached = cache_k.shape[1]
        ks = jnp.concatenate([cache_k.reshape(ns, n_cached, kvw),
                              k_f32[tp:].reshape(ns, seq_s, kvw)], axis=1).astype(bf16)
        vs = jnp.concatenate([cache_v.reshape(ns, n_cached, kvw),
                              v_f32[tp:].reshape(ns, seq_s, kvw)], axis=1).astype(bf16)
        bias_s = None if table is None else _sample_bias(table, seq_s, n_cached)
        sink_s = None if sink is None else _sample_sink(sink, seq_s)
        return o_p, _attend_sample(q, ks, vs, bias_s, sink_s, row0=tp, lq=seq_s)

    def tail(c, n_keep):
        rows = jnp.stack([c[(b + 1) * seq - n_keep:(b + 1) * seq] for b in range(nb)])
        return rows.reshape(nb, n_keep, n_kv, HEAD_DIM)

    def rolled(cache, c):
        new = c[tp:].reshape(ns, seq_s, n_kv, HEAD_DIM)
        return jnp.concatenate([cache, new], axis=1)[:, seq_s:]

    ffn_w = [w.astype(bf16)[:, None] for w in (w_gate_ffn, w_up_ffn, w_down_ffn)]
    moe_w = [w.astype(bf16) for w in (w_gate_moe, w_up_moe, w_down_moe)]

    ka_p, va_p, ka_s, va_s = [], [], [], []
    kvb_pad = kb_f = vb_f = None
    for layer in range(depth):
        g_in = row(g_attn[layer])
        if layer < n_a:
            i = layer
            q, kl, kh, vl, vh, k_f32, v_f32 = _project(
                x, g_in, w_qkv_a[i].astype(bf16), _head_gain(g_q_a[i]), _head_gain(g_k_a[i]),
                cos, sin, nq=(w_qkv_a.shape[2] - 2 * kvw) // COL_TILE, has_kv=True, rope=True)
            o = attend(q, (kl, kh, vl, vh), k_f32, v_f32, cache_k_a[i], cache_v_a[i],
                       PREV_CHUNKS_A, None, sink_a[i])
            x = _out_proj(o, w_o_a[i].astype(bf16), x)
            ka_p.append(tail(k_f32, keep_a))
            va_p.append(tail(v_f32, keep_a))
            ka_s.append(rolled(cache_k_a[i], k_f32))
            va_s.append(rolled(cache_v_a[i], v_f32))
        else:
            jb = layer - n_a
            (q,) = _project(x, g_in, w_q_b[jb].astype(bf16), _head_gain(g_q_b[jb]), None,
                            None, None, nq=w_q_b.shape[2] // COL_TILE, has_kv=False, rope=False)
            o = attend(q, kvb_pad, kb_f, vb_f, cache_k_b, cache_v_b, PREV_CHUNKS_B,
                       rel_bias_b[jb], None)
            x = _out_proj(o, w_o_b[jb].astype(bf16), x)

        g_mid = row(g_ffn[layer])
        m = layer // 2
        if layer % 2 == 0:
            n_tiles = t // FFN_ROWS
            x = _swiglu_blocks(x, g_mid, *ffn_w, m, jnp.zeros((n_tiles,), jnp.int32),
                               jnp.full((1,), n_tiles, jnp.int32), tm=FFN_ROWS, tf=COL_TILE)
        else:
            x = _moe(x, g_mid, w_router[m], b_router[m], *moe_w, m,
                     split_at=tp if layer == depth - 1 else None)

        if layer == n_a - 1:
            *kvb_pad, kb_f, vb_f = _project(x, row(g_kv_b), w_kv_b.astype(bf16), None,
                                            _head_gain(g_k_b), None, None,
                                            nq=0, has_kv=True, rope=False)

    if not isinstance(x, tuple):
        x = (x[:tp], x[tp:])
    y_prompt = x[0].reshape(nb, seq, d)
    y_sample = x[1].reshape(ns, seq_s, d)
    return (y_prompt, y_sample, jnp.stack(ka_p), jnp.stack(va_p), tail(kb_f, keep_b),
            tail(vb_f, keep_b), jnp.stack(ka_s), jnp.stack(va_s),
            rolled(cache_k_b, kb_f), rolled(cache_v_b, vb_f))
```

```python
import functools

import jax
import jax.numpy as jnp
from jax import lax
from jax.experimental import pallas as pl
from jax.experimental.pallas import tpu as pltpu

CHUNK = 64
HEAD_DIM = 64
GROUP = 4
WINDOW_A = 128
PREV_CHUNKS_A = WINDOW_A // CHUNK
PREV_CHUNKS_B = 8
BAND_B = PREV_CHUNKS_B * CHUNK
REL_MIN = -(CHUNK - 1)
REL_MAX = 256
ROPE_THETA = 10000.0
TOP_K = 2
RMS_EPS = 1e-6
NEG_INF = -1e30
PAST_LEN = 1024

LANES = 128
HEADS_PER_VREG = LANES // HEAD_DIM
KV_GROUP_WIDTH = GROUP * HEAD_DIM
PAIRS = GROUP // HEADS_PER_VREG

ROW_TILE = 512
COL_TILE = 512
FFN_ROWS = 832
MOE_ROWS = 640
MOE_COL_TILE = 1024
COMBINE_ROWS = 256
ATTN_UNROLL_WIDE = 4
ATTN_UNROLL_NARROW = 8
DMA_THREADS = 2
VMEM_LIMIT = 56 * 1024 * 1024


def _params(semantics):
    return pltpu.CompilerParams(dimension_semantics=semantics, vmem_limit_bytes=VMEM_LIMIT)


def _rms_rows(x, g):
    ms = jnp.sum(x * x, axis=-1, keepdims=True) * (1.0 / x.shape[-1])
    return (x * lax.rsqrt(ms + RMS_EPS)) * g


def _head_norm(y, gain, cos, sin):
    lane = lax.broadcasted_iota(jnp.int32, y.shape, 1)
    low = lane < HEAD_DIM
    ss = y * y
    s_lo = jnp.sum(jnp.where(low, ss, 0.0), axis=-1, keepdims=True)
    s_hi = jnp.sum(jnp.where(low, 0.0, ss), axis=-1, keepdims=True)
    ms = jnp.where(low, s_lo, s_hi) * (1.0 / HEAD_DIM)
    yn = (y * lax.rsqrt(ms + RMS_EPS)) * gain
    if cos is None:
        return yn
    half = HEAD_DIM // 2
    fwd = pltpu.roll(yn, half, 1)
    bwd = pltpu.roll(yn, LANES - half, 1)
    rot = jnp.where((lane % HEAD_DIM) < half, bwd, fwd)
    return yn * cos + rot * sin


def _store_half_padded(lo_ref, hi_ref, c, z):
    lane = lax.broadcasted_iota(jnp.int32, z.shape, 1)
    low = lane < HEAD_DIM
    zr = pltpu.roll(z, HEAD_DIM, 1)
    dt = lo_ref.dtype
    a, b = 2 * c * LANES, (2 * c + 1) * LANES
    lo_ref[:, a:a + LANES] = jnp.where(low, z, 0.0).astype(dt)
    lo_ref[:, b:b + LANES] = jnp.where(low, zr, 0.0).astype(dt)
    hi_ref[:, a:a + LANES] = jnp.where(low, 0.0, zr).astype(dt)
    hi_ref[:, b:b + LANES] = jnp.where(low, 0.0, z).astype(dt)


def _row_specs(x, tm):
    def spec(cols, fn):
        return pl.BlockSpec((tm, cols), lambda i: (fn(i), 0))

    if not isinstance(x, tuple):
        return [spec(x.shape[1], lambda i: i)], 0
    head, tail = x
    assert head.shape[0] % tm == 0 and tail.shape[0] % tm == 0 and head.shape[1] == tail.shape[1]
    n_head = head.shape[0] // tm
    return [spec(head.shape[1], lambda i: jnp.minimum(i, n_head - 1)),
            spec(head.shape[1], lambda i: jnp.maximum(i - n_head, 0))], n_head


def _read_rows(x_refs, n_head):
    if len(x_refs) == 1:
        return x_refs[0][...]
    return jnp.where(pl.program_id(0) < n_head, x_refs[0][...], x_refs[1][...])


def _n_rows(x):
    return x[0].shape[0] + x[1].shape[0] if isinstance(x, tuple) else x.shape[0]


def _as_list(x):
    return list(x) if isinstance(x, tuple) else [x]


def _proj_kernel(*refs, nq, has_kv, rope, n_x, n_head):
    it = iter(refs)
    x_refs = [next(it) for _ in range(n_x)]
    g_ref, w_ref = next(it), next(it)
    gq_ref = next(it) if nq else None
    gk_ref = next(it) if has_kv else None
    cos_ref, sin_ref = (next(it), next(it)) if rope else (None, None)
    q_ref = next(it) if nq else None
    if has_kv:
        kl_ref, kh_ref, vl_ref, vh_ref, kf_ref, vf_ref = (next(it) for _ in range(6))
    h_scr = next(it)
    h_scr[...] = _rms_rows(_read_rows(x_refs, n_head), g_ref[...]).astype(h_scr.dtype)
    cos = cos_ref[...] if rope else None
    sin = sin_ref[...] if rope else None
    groups = COL_TILE // LANES
    for j in range(nq + (2 if has_kv else 0)):
        y = jnp.dot(h_scr[...], w_ref[j], preferred_element_type=jnp.float32)
        for c in range(groups):
            yc = y[:, c * LANES:(c + 1) * LANES]
            if j < nq:
                z = _head_norm(yc, gq_ref[...], cos, sin) * (HEAD_DIM ** -0.5)
                q_ref[:, (j * groups + c) * LANES:(j * groups + c + 1) * LANES] = z.astype(q_ref.dtype)
            elif j == nq:
                z = _head_norm(yc, gk_ref[...], cos, sin)
                kf_ref[:, c * LANES:(c + 1) * LANES] = z
                _store_half_padded(kl_ref, kh_ref, c, z)
            else:
                vf_ref[:, c * LANES:(c + 1) * LANES] = yc
                _store_half_padded(vl_ref, vh_ref, c, yc)


def _project(x, g, w, gq, gk, cos, sin, *, nq, has_kv, rope):
    t = _n_rows(x)
    d = w.shape[0]
    tm = ROW_TILE
    n_col = nq + (2 if has_kv else 0)
    assert w.shape == (d, n_col * COL_TILE) and t % tm == 0
    w = jnp.swapaxes(w.reshape(d, n_col, COL_TILE), 0, 1)
    row = lambda i: (i, 0)
    x_specs, n_head = _row_specs(x, tm)
    in_specs = x_specs + [pl.BlockSpec((1, d), lambda i: (0, 0)),
                          pl.BlockSpec((n_col, d, COL_TILE), lambda i: (0, 0, 0))]
    args = _as_list(x) + [g, w]
    vec = pl.BlockSpec((1, LANES), lambda i: (0, 0))
    if nq:
        in_specs.append(vec)
        args.append(gq)
    if has_kv:
        in_specs.append(vec)
        args.append(gk)
    if rope:
        in_specs += [pl.BlockSpec((tm, LANES), row)] * 2
        args += [cos, sin]
    out_shape, out_specs = [], []
    if nq:
        out_shape.append(jax.ShapeDtypeStruct((t, nq * COL_TILE), jnp.bfloat16))
        out_specs.append(pl.BlockSpec((tm, nq * COL_TILE), row))
    if has_kv:
        for _ in range(4):
            out_shape.append(jax.ShapeDtypeStruct((t, HEADS_PER_VREG * COL_TILE), jnp.bfloat16))
            out_specs.append(pl.BlockSpec((tm, HEADS_PER_VREG * COL_TILE), row))
        for _ in range(2):
            out_shape.append(jax.ShapeDtypeStruct((t, COL_TILE), jnp.float32))
            out_specs.append(pl.BlockSpec((tm, COL_TILE), row))
    return pl.pallas_call(
        functools.partial(_proj_kernel, nq=nq, has_kv=has_kv, rope=rope,
                          n_x=len(x_specs), n_head=n_head),
        grid=(t // tm,),
        in_specs=in_specs,
        out_specs=out_specs,
        out_shape=out_shape,
        scratch_shapes=[pltpu.VMEM((tm, d), jnp.bfloat16)],
        compiler_params=_params(("parallel",)),
        name="proj",
    )(*args)


def _attn_prompt_kernel(*refs, seq, band, n_prev, unroll, has_bias, has_sink):
    it = iter(refs)
    q_ref, kl_ref, kh_ref, vl_ref, vh_ref = (next(it) for _ in range(5))
    bias_ref = next(it) if has_bias else None
    sink_ref = next(it) if has_sink else None
    o_ref = next(it)
    s_scr = (next(it), next(it))
    p_scr = (next(it), next(it))
    n_scr = (next(it), next(it))
    row = lax.broadcasted_iota(jnp.int32, (band, LANES), 0)
    nt = (((1,), (1,)), ((), ()))
    tn = (((0,), (0,)), ((), ()))
    n_groups = seq // (CHUNK * unroll)
    n_masked = min(-(-n_prev // unroll), n_groups)

    def key_start(c):
        return pl.multiple_of(jnp.maximum(c - n_prev, 0) * CHUNK, CHUNK)

    def scores(gi, slot):
        for u in range(unroll):
            c = gi * unroll + u
            r0 = pl.multiple_of(c * CHUNK, CHUNK)
            k0 = key_start(c)
            qs = jnp.concatenate([q_ref[pl.ds(r0, CHUNK), p * LANES:(p + 1) * LANES]
                                  for p in range(PAIRS)], axis=0)
            for half, k_ref in enumerate((kl_ref, kh_ref)):
                s_scr[slot][u, half] = lax.dot_general(k_ref[pl.ds(k0, band), :], qs, nt,
                                                        preferred_element_type=jnp.float32)

    def softmax(gi, slot, masked):
        for u in range(unroll):
            c = gi * unroll + u
            shift = pl.multiple_of(jnp.maximum(n_prev - c, 0) * CHUNK, CHUNK) if masked else 0
            inv = []
            for half in range(HEADS_PER_VREG):
                s = s_scr[slot][u, half]
                if has_bias:
                    s = s + bias_ref[half, pl.ds(shift, band), :]
                if masked:
                    s = jnp.where(row < band - shift, s, NEG_INF)
                m = jnp.max(s, axis=0, keepdims=True)
                if has_sink:
                    m = jnp.maximum(m, sink_ref[half])
                e = jnp.exp(s - m)
                l = jnp.sum(e, axis=0, keepdims=True)
                if has_sink:
                    l = l + jnp.exp(sink_ref[half] - m)
                p_scr[slot][u, half] = e.astype(p_scr[slot].dtype)
                inv.append(jnp.broadcast_to(1.0 / l, (LANES, LANES)))
            n_scr[slot][u] = jnp.where(row[:LANES] < HEAD_DIM, inv[0], inv[1])

    def outputs(gi, slot):
        for u in range(unroll):
            c = gi * unroll + u
            r0 = pl.multiple_of(c * CHUNK, CHUNK)
            k0 = key_start(c)
            o = (lax.dot_general(p_scr[slot][u, 0], vl_ref[pl.ds(k0, band), :], tn,
                                 preferred_element_type=jnp.float32)
                 + lax.dot_general(p_scr[slot][u, 1], vh_ref[pl.ds(k0, band), :], tn,
                                   preferred_element_type=jnp.float32))
            o = o * n_scr[slot][u].T
            for p in range(PAIRS):
                o_ref[pl.ds(r0, CHUNK), p * LANES:(p + 1) * LANES] = (
                    o[p * CHUNK:(p + 1) * CHUNK].astype(o_ref.dtype))

    def stage(g, parity, masked):
        static = isinstance(g, int)
        if not static or g < n_groups:
            scores(g, parity)
        if not static or 0 <= g - 1 < n_groups:
            softmax(g - 1, 1 - parity, masked)
        if not static or 0 <= g - 2 < n_groups:
            outputs(g - 2, parity)

    first_loop = max(n_masked + 1, 2)
    first_loop += first_loop % 2
    n_pairs = max(n_groups - first_loop, 0) // 2
    for g in range(min(first_loop, n_groups)):
        stage(g, g % 2, masked=True)
    if n_pairs:
        def body(k, carry):
            g = first_loop + 2 * k
            stage(g, 0, masked=False)
            stage(g + 1, 1, masked=False)
            return carry
        lax.fori_loop(0, n_pairs, body, 0)
    for g in range(min(first_loop, n_groups) + 2 * n_pairs, n_groups + 2):
        stage(g, g % 2, masked=(g - 1) < n_masked)


def _attend_prompt(q, kl, kh, vl, vh, bias, sink, *, nb, seq, n_prev):
    qw = q.shape[1]
    n_kv = kl.shape[1] // LANES
    band = (n_prev + 1) * CHUNK
    unroll = ATTN_UNROLL_WIDE if n_prev > PREV_CHUNKS_A else ATTN_UNROLL_NARROW
    assert seq % (CHUNK * unroll) == 0 and seq >= band and seq >= 2 * CHUNK * unroll
    kv_spec = pl.BlockSpec((seq, LANES), lambda b, j: (b, j))
    in_specs = [pl.BlockSpec((seq, KV_GROUP_WIDTH), lambda b, j: (b, j))] + [kv_spec] * 4
    args = [q, kl, kh, vl, vh]
    if bias is not None:
        in_specs.append(pl.BlockSpec((HEADS_PER_VREG,) + bias.shape[1:], lambda b, j: (j, 0, 0)))
        args.append(bias)
    if sink is not None:
        in_specs.append(pl.BlockSpec((HEADS_PER_VREG, 1, LANES), lambda b, j: (j, 0, 0)))
        args.append(sink)
    return pl.pallas_call(
        functools.partial(_attn_prompt_kernel, seq=seq, band=band, n_prev=n_prev, unroll=unroll,
                          has_bias=bias is not None, has_sink=sink is not None),
        grid=(nb, n_kv),
        in_specs=in_specs,
        out_specs=pl.BlockSpec((seq, KV_GROUP_WIDTH), lambda b, j: (b, j)),
        out_shape=jax.ShapeDtypeStruct((nb * seq, qw), jnp.bfloat16),
        scratch_shapes=[pltpu.VMEM((unroll, HEADS_PER_VREG, band, LANES), jnp.float32)] * 2
        + [pltpu.VMEM((unroll, HEADS_PER_VREG, band, LANES), jnp.bfloat16)] * 2
        + [pltpu.VMEM((unroll, LANES, LANES), jnp.float32)] * 2,
        compiler_params=_params(("parallel", "arbitrary")),
        name="attn_prompt",
    )(*args)


def _attn_sample_kernel(*refs, lq, has_bias, has_sink, n_kv):
    it = iter(refs)
    q_ref, k_ref, v_ref = next(it), next(it), next(it)
    bias_ref = next(it) if has_bias else None
    sink_ref = next(it) if has_sink else None
    o_ref = next(it)
    for j in range(n_kv):
        qj = q_ref[:, j * KV_GROUP_WIDTH:(j + 1) * KV_GROUP_WIDTH]
        qcat = jnp.concatenate(
            [qj[:, g * HEAD_DIM:(g + 1) * HEAD_DIM] for g in range(GROUP)], axis=0)
        kj = k_ref[0, :, j * HEAD_DIM:(j + 1) * HEAD_DIM]
        vj = v_ref[0, :, j * HEAD_DIM:(j + 1) * HEAD_DIM]
        s = lax.dot_general(qcat, kj, (((1,), (1,)), ((), ())),
                            preferred_element_type=jnp.float32)
        if has_bias:
            s = s + bias_ref[j]
        m = jnp.max(s, axis=-1, keepdims=True)
        if has_sink:
            sink = sink_ref[j]
            m = jnp.maximum(m, sink)
        e = jnp.exp(s - m)
        l = jnp.sum(e, axis=-1, keepdims=True)
        if has_sink:
            l = l + jnp.exp(sink - m)
        p = (e * (1.0 / l)).astype(vj.dtype)
        o = jnp.dot(p, vj, preferred_element_type=jnp.float32)
        ocat = jnp.concatenate([o[g * lq:(g + 1) * lq] for g in range(GROUP)], axis=1)
        o_ref[:, j * KV_GROUP_WIDTH:(j + 1) * KV_GROUP_WIDTH] = ocat.astype(o_ref.dtype)


def _attend_sample(q, k, v, bias, sink, *, row0, lq):
    ns, lk, kw = k.shape
    qw = q.shape[1]
    assert row0 % lq == 0
    blk0 = row0 // lq
    in_specs = [pl.BlockSpec((lq, qw), lambda i: (blk0 + i, 0)),
                pl.BlockSpec((1, lk, kw), lambda i: (i, 0, 0)),
                pl.BlockSpec((1, lk, kw), lambda i: (i, 0, 0))]
    args = [q, k, v]
    if bias is not None:
        in_specs.append(pl.BlockSpec(bias.shape, lambda i: (0, 0, 0)))
        args.append(bias)
    if sink is not None:
        in_specs.append(pl.BlockSpec(sink.shape, lambda i: (0, 0, 0)))
        args.append(sink)
    return pl.pallas_call(
        functools.partial(_attn_sample_kernel, lq=lq, has_bias=bias is not None,
                          has_sink=sink is not None, n_kv=kw // HEAD_DIM),
        grid=(ns,),
        in_specs=in_specs,
        out_specs=pl.BlockSpec((lq, qw), lambda i: (i, 0)),
        out_shape=jax.ShapeDtypeStruct((ns * lq, qw), jnp.bfloat16),
        compiler_params=_params(("arbitrary",)),
        name="attn_sample",
    )(*args)


def _out_proj_kernel(*refs, n_o, o_head, n_x, x_head):
    o_refs, w_ref = refs[:n_o], refs[n_o]
    x_refs, y_ref = refs[n_o + 1:n_o + 1 + n_x], refs[n_o + 1 + n_x]
    y_ref[...] = _read_rows(x_refs, x_head) + jnp.dot(
        _read_rows(o_refs, o_head), w_ref[...], preferred_element_type=jnp.float32)


def _out_proj(o, w, x):
    t = _n_rows(x)
    d = w.shape[1]
    tm = ROW_TILE
    o_specs, o_head = _row_specs(o, tm)
    x_specs, x_head = _row_specs(x, tm)
    return pl.pallas_call(
        functools.partial(_out_proj_kernel, n_o=len(o_specs), o_head=o_head,
                          n_x=len(x_specs), x_head=x_head),
        grid=(t // tm,),
        in_specs=o_specs + [pl.BlockSpec(w.shape, lambda i: (0, 0))] + x_specs,
        out_specs=pl.BlockSpec((tm, d), lambda i: (i, 0)),
        out_shape=jax.ShapeDtypeStruct((t, d), jnp.float32),
        compiler_params=_params(("parallel",)),
        name="out_proj",
    )(*_as_list(o), w, *_as_list(x))


def _swiglu_stage(h_scr, act_ref, wg_ref, wu_ref):
    h = h_scr[...]
    gate = jnp.dot(h, wg_ref[...], preferred_element_type=jnp.float32)
    up = jnp.dot(h, wu_ref[...], preferred_element_type=jnp.float32)
    act_ref[...] = ((gate * jax.nn.sigmoid(gate)) * up).astype(act_ref.dtype)


def _ffn_kernel(e_ref, n_ref, x_ref, g_ref, wg_ref, wu_ref, wd_ref, y_ref, h_scr, acc_scr,
                act0, act1, *, n_f):
    s = pl.program_id(1)
    act = (act0, act1)

    @pl.when(pl.program_id(0) < n_ref[0])
    def _():
        @pl.when(s == 0)
        def _():
            h_scr[...] = _rms_rows(x_ref[...], g_ref[...]).astype(h_scr.dtype)
            acc_scr[...] = jnp.zeros_like(acc_scr)
            _swiglu_stage(h_scr, act[0], wg_ref, wu_ref)

        for parity in range(2):
            @pl.when((s > 0) & (s < n_f) & (s % 2 == parity))
            def _():
                acc_scr[...] += jnp.dot(act[1 - parity][...], wd_ref[...],
                                        preferred_element_type=jnp.float32)
                _swiglu_stage(h_scr, act[parity], wg_ref, wu_ref)

        @pl.when(s == n_f)
        def _():
            part = jnp.dot(act[(n_f - 1) % 2][...], wd_ref[...],
                           preferred_element_type=jnp.float32)
            y_ref[...] = x_ref[...] + (acc_scr[...] + part)

    @pl.when((pl.program_id(0) >= n_ref[0]) & (s == n_f))
    def _():
        y_ref[...] = jnp.zeros_like(y_ref)


def _swiglu_blocks(x, g, wg, wu, wd, layer, block_expert, n_used, *, tm, tf):
    r, d = x.shape
    ff = wg.shape[3]
    assert r % tm == 0 and ff % tf == 0
    n_f = ff // tf

    def blk(i, n):
        return jnp.minimum(i, n[0] - 1)

    def up_col(i, s, n):
        return jnp.where(i < n[0], jnp.minimum(s, n_f - 1), n_f - 1)

    def down_row(i, s, n):
        return jnp.where(i < n[0], jnp.maximum(s - 1, 0), n_f - 1)

    grid_spec = pltpu.PrefetchScalarGridSpec(
        num_scalar_prefetch=2,
        grid=(r // tm, n_f + 1),
        in_specs=[pl.BlockSpec((tm, d), lambda i, s, e, n: (blk(i, n), 0)),
                  pl.BlockSpec((1, d), lambda i, s, e, n: (0, 0)),
                  pl.BlockSpec((None, None, d, tf),
                               lambda i, s, e, n: (layer, e[blk(i, n)], 0, up_col(i, s, n))),
                  pl.BlockSpec((None, None, d, tf),
                               lambda i, s, e, n: (layer, e[blk(i, n)], 0, up_col(i, s, n))),
                  pl.BlockSpec((None, None, tf, d),
                               lambda i, s, e, n: (layer, e[blk(i, n)], down_row(i, s, n), 0))],
        out_specs=pl.BlockSpec((tm, d), lambda i, s, e, n: (i, 0)),
        scratch_shapes=[pltpu.VMEM((tm, d), jnp.bfloat16),
                        pltpu.VMEM((tm, d), jnp.float32),
                        pltpu.VMEM((tm, tf), jnp.bfloat16),
                        pltpu.VMEM((tm, tf), jnp.bfloat16)],
    )
    return pl.pallas_call(
        functools.partial(_ffn_kernel, n_f=n_f),
        grid_spec=grid_spec,
        out_shape=jax.ShapeDtypeStruct((r, d), jnp.float32),
        compiler_params=_params(("arbitrary", "arbitrary")),
        name="swiglu",
    )(block_expert, n_used, x, g, wg, wu, wd)


def _router_kernel(x_ref, g_ref, w_ref, b_ref, idx_ref, gate_ref, h_ref):
    h = _rms_rows(x_ref[...], g_ref[...]).astype(jnp.bfloat16)
    h_ref[...] = h.reshape(h_ref.shape)
    logits = jnp.dot(h, w_ref[...], preferred_element_type=jnp.float32) + b_ref[...]
    lane = lax.broadcasted_iota(jnp.int32, logits.shape, 1)
    m1 = jnp.max(logits, axis=-1, keepdims=True)
    e1 = jnp.min(jnp.where(logits == m1, lane, LANES), axis=-1, keepdims=True)
    rest = jnp.where(lane == e1, -jnp.inf, logits)
    m2 = jnp.max(rest, axis=-1, keepdims=True)
    e2 = jnp.min(jnp.where(rest == m2, lane, LANES), axis=-1, keepdims=True)
    t = jnp.exp(m2 - m1)
    denom = 1.0 + t
    idx_ref[...] = jnp.where(lane == 0, e1, jnp.where(lane == 1, e2, 0))
    gate_ref[...] = jnp.where(lane == 0, 1.0 / denom, jnp.where(lane == 1, t / denom, 0.0))


def _route(x, g, w_router, b_router):
    t, d = x.shape
    tm = ROW_TILE
    n_e = w_router.shape[1]
    w = jnp.zeros((d, LANES), jnp.bfloat16).at[:, :n_e].set(w_router.astype(jnp.bfloat16))
    b = jnp.full((1, LANES), -jnp.inf, jnp.float32).at[0, :n_e].set(b_router.astype(jnp.float32))
    return pl.pallas_call(
        _router_kernel,
        grid=(t // tm,),
        in_specs=[pl.BlockSpec((tm, d), lambda i: (i, 0)),
                  pl.BlockSpec((1, d), lambda i: (0, 0)),
                  pl.BlockSpec((d, LANES), lambda i: (0, 0)),
                  pl.BlockSpec((1, LANES), lambda i: (0, 0))],
        out_specs=[pl.BlockSpec((tm, LANES), lambda i: (i, 0))] * 2
        + [pl.BlockSpec((tm, d // LANES, LANES), lambda i: (i, 0, 0))],
        out_shape=[jax.ShapeDtypeStruct((t, LANES), jnp.int32),
                   jax.ShapeDtypeStruct((t, LANES), jnp.float32),
                   jax.ShapeDtypeStruct((t, d // LANES, LANES), jnp.bfloat16)],
        compiler_params=_params(("parallel",)),
        name="router",
    )(x, g, w, b)


def _experts_kernel(e_ref, n_ref, first_ref, next_ref, prev_dst_ref, dst_ref, h_hbm,
                    wg_ref, wu_ref, wd_ref, y_hbm, xbuf, ybuf, h_scr, acc_scr, act0, act1,
                    gsem, ssem, *, tm, n_f):
    i, s = pl.program_id(0), pl.program_id(1)
    act = (act0, act1)
    n_used = n_ref[0]
    slot = i % 2
    other = 1 - slot
    share = tm // n_f
    unroll = 2 * DMA_THREADS

    def gather_copy(idx_ref, r, to_slot):
        return pltpu.make_async_copy(h_hbm.at[idx_ref[r]], xbuf.at[to_slot, r], gsem.at[to_slot])

    def scatter_copy(idx_ref, r, from_slot):
        return pltpu.make_async_copy(ybuf.at[from_slot, r], y_hbm.at[idx_ref[r]],
                                     ssem.at[from_slot])

    def issue_all(copy_of_row):
        def issue(r4, carry):
            for k in range(unroll):
                copy_of_row(r4 * unroll + k).start(priority=k % DMA_THREADS)
            return carry
        lax.fori_loop(0, tm // unroll, issue, 0)

    def wait_gather(s_):
        pltpu.make_async_copy(h_hbm.at[pl.ds(0, tm)], xbuf.at[s_], gsem.at[s_]).wait()

    def wait_scatter(s_):
        pltpu.make_async_copy(ybuf.at[s_], y_hbm.at[pl.ds(0, tm)], ssem.at[s_]).wait()

    def row_copies(step):
        for k in range(share):
            r = step * share + k
            gather_copy(next_ref, r, other).start(priority=1)
            scatter_copy(prev_dst_ref, r, other).start(priority=1)

    @pl.when(i < n_used)
    def _():
        @pl.when(s == 0)
        def _():
            @pl.when(i == 0)
            def _():
                issue_all(lambda r: gather_copy(first_ref, r, 0))
                ybuf[1] = jnp.zeros(ybuf.shape[1:], ybuf.dtype)
                for half in range(2):
                    spare = pltpu.make_async_copy(
                        ybuf.at[1], y_hbm.at[pl.ds(y_hbm.shape[0] - (3 - half) * tm, tm)],
                        ssem.at[1])
                    spare.start()
                    spare.wait()

            wait_gather(slot)
            h_scr[...] = xbuf[slot].reshape(h_scr.shape)
            acc_scr[...] = jnp.zeros_like(acc_scr)
            row_copies(0)
            _swiglu_stage(h_scr, act[0], wg_ref, wu_ref)

        for parity in range(2):
            @pl.when((s > 0) & (s < n_f) & (s % 2 == parity))
            def _():
                row_copies(s)
                acc_scr[...] += jnp.dot(act[1 - parity][...], wd_ref[...],
                                        preferred_element_type=jnp.float32)
                _swiglu_stage(h_scr, act[parity], wg_ref, wu_ref)

        @pl.when(s == n_f)
        def _():
            part = jnp.dot(act[(n_f - 1) % 2][...], wd_ref[...],
                           preferred_element_type=jnp.float32)

            @pl.when(i >= 1)
            def _():
                wait_scatter(slot)

            ybuf[slot] = (acc_scr[...] + part).reshape(ybuf.shape[1:])

            @pl.when(i == n_used - 1)
            def _():
                issue_all(lambda r: scatter_copy(dst_ref, r, slot))
                wait_scatter(slot)
                wait_scatter(other)
                wait_gather(other)


def _experts(h_rows, wg, wu, wd, layer, block_expert, n_used, row_tok, row_dst, n_out, *, tm, tf):
    tile = h_rows.shape[1:]
    d, ff = wg.shape[2], wg.shape[3]
    rows = row_tok.shape[0]
    assert rows % tm == 0 and row_dst.shape[0] == rows + tm and ff % tf == 0
    row_tok, row_dst = _index_blocks(row_tok, tm), _index_blocks(row_dst, tm)
    n_f = ff // tf
    assert tm % (n_f * 2 * DMA_THREADS) == 0

    def blk(i, n):
        return jnp.minimum(i, n[0] - 1)

    def up_col(i, s, n):
        return jnp.where(i < n[0], jnp.minimum(s, n_f - 1), n_f - 1)

    def down_row(i, s, n):
        return jnp.where(i < n[0], jnp.maximum(s - 1, 0), n_f - 1)

    smem = functools.partial(pl.BlockSpec, memory_space=pltpu.SMEM)
    grid_spec = pltpu.PrefetchScalarGridSpec(
        num_scalar_prefetch=2,
        grid=(rows // tm, n_f + 1),
        in_specs=[smem((None, None, tm), lambda i, f, e, n: (0, 0, 0)),
                  smem((None, None, tm), lambda i, f, e, n: (blk(i + 1, n), 0, 0)),
                  smem((None, None, tm), lambda i, f, e, n: (blk(i, n), 0, 0)),
                  smem((None, None, tm), lambda i, f, e, n: (blk(i, n) + 1, 0, 0)),
                  pl.BlockSpec(memory_space=pl.ANY),
                  pl.BlockSpec((None, None, d, tf),
                               lambda i, f, e, n: (layer, e[blk(i, n)], 0, up_col(i, f, n))),
                  pl.BlockSpec((None, None, d, tf),
                               lambda i, f, e, n: (layer, e[blk(i, n)], 0, up_col(i, f, n))),
                  pl.BlockSpec((None, None, tf, d),
                               lambda i, f, e, n: (layer, e[blk(i, n)], down_row(i, f, n), 0))],
        out_specs=pl.BlockSpec(memory_space=pl.ANY),
        scratch_shapes=[pltpu.VMEM((2, tm) + tile, jnp.bfloat16),
                        pltpu.VMEM((2, tm) + tile, jnp.float32),
                        pltpu.VMEM((tm, d), jnp.bfloat16),
                        pltpu.VMEM((tm, d), jnp.float32),
                        pltpu.VMEM((tm, tf), jnp.bfloat16),
                        pltpu.VMEM((tm, tf), jnp.bfloat16),
                        pltpu.SemaphoreType.DMA((2,)),
                        pltpu.SemaphoreType.DMA((2,))],
    )
    return pl.pallas_call(
        functools.partial(_experts_kernel, tm=tm, n_f=n_f),
        grid_spec=grid_spec,
        out_shape=jax.ShapeDtypeStruct((n_out,) + tile, jnp.float32),
        compiler_params=_params(("arbitrary", "arbitrary")),
        name="experts",
    )(block_expert, n_used, row_tok, row_tok, row_dst, row_dst, h_rows, wg, wu, wd)


def _index_blocks(idx, tm):
    return idx.reshape(-1, 1, tm)


def _mix_kernel(x_ref, gate_ref, y0_ref, y1_ref, *out_refs, n_head):
    gates = gate_ref[...]
    shape = x_ref.shape
    y = gates[:, 0:1] * y0_ref[...].reshape(shape) + gates[:, 1:2] * y1_ref[...].reshape(shape)
    out = x_ref[...] + y
    if len(out_refs) == 1:
        out_refs[0][...] = out
    else:
        @pl.when(pl.program_id(0) < n_head)
        def _():
            out_refs[0][...] = out

        @pl.when(pl.program_id(0) >= n_head)
        def _():
            out_refs[1][...] = out


def _mix(x, gates, y, split_at=None):
    t, d = x.shape
    rows = COMBINE_ROWS
    tile = y.shape[1:]
    assert t % rows == 0
    n_t = t // rows
    if split_at is None:
        n_head = 0
        out_specs = pl.BlockSpec((rows, d), lambda i: (i, 0))
        out_shape = jax.ShapeDtypeStruct((t, d), jnp.float32)
    else:
        assert split_at % rows == 0 and 0 < split_at < t
        n_head = split_at // rows
        out_specs = [pl.BlockSpec((rows, d), lambda i: (jnp.minimum(i, n_head - 1), 0)),
                     pl.BlockSpec((rows, d), lambda i: (jnp.maximum(i - n_head, 0), 0))]
        out_shape = [jax.ShapeDtypeStruct((split_at, d), jnp.float32),
                     jax.ShapeDtypeStruct((t - split_at, d), jnp.float32)]
    return pl.pallas_call(
        functools.partial(_mix_kernel, n_head=n_head),
        grid=(n_t,),
        in_specs=[pl.BlockSpec((rows, d), lambda i: (i, 0)),
                  pl.BlockSpec((rows, LANES), lambda i: (i, 0)),
                  pl.BlockSpec((rows,) + tile, lambda i: (i, 0, 0)),
                  pl.BlockSpec((rows,) + tile, lambda i: (n_t + i, 0, 0))],
        out_specs=out_specs,
        out_shape=out_shape,
        compiler_params=_params(("arbitrary",)),
        name="mix",
    )(x, gates, y, y)


def _moe(x, g, w_router, b_router, wg, wu, wd, layer, split_at=None):
    t, d = x.shape
    n_e = wg.shape[1]
    n_assign = t * TOP_K
    top_idx, gates, h_rows = _route(x, g, w_router, b_router)
    flat_e = top_idx[:, :TOP_K].reshape(-1)
    onehot = (flat_e[:, None] == jnp.arange(n_e, dtype=jnp.int32)[None, :]).astype(jnp.int32)
    rank = jnp.sum((jnp.cumsum(onehot, axis=0) - onehot) * onehot, axis=1)
    counts = jnp.sum(onehot, axis=0)
    padded = (counts + MOE_ROWS - 1) // MOE_ROWS * MOE_ROWS
    ends = jnp.cumsum(padded)
    dest = jnp.sum((ends - padded)[None, :] * onehot, axis=1) + rank
    n_blocks = (n_assign + n_e * (MOE_ROWS - 1) + MOE_ROWS - 1) // MOE_ROWS
    rows = n_blocks * MOE_ROWS
    row_assign = jnp.full((rows,), -1, jnp.int32).at[dest].set(
        jnp.arange(n_assign, dtype=jnp.int32), unique_indices=True, mode='promise_in_bounds')
    real = row_assign >= 0
    row_tok = jnp.where(real, row_assign // TOP_K, 0)
    spare = n_assign + (jnp.arange(rows, dtype=jnp.int32) // MOE_ROWS % 2) * MOE_ROWS \
        + jnp.arange(rows, dtype=jnp.int32) % MOE_ROWS
    row_dst = jnp.where(real, (row_assign % TOP_K) * t + row_assign // TOP_K, spare)
    row_dst = jnp.concatenate(
        [n_assign + 2 * MOE_ROWS + jnp.arange(MOE_ROWS, dtype=jnp.int32), row_dst])
    block_start = jnp.arange(n_blocks, dtype=jnp.int32) * MOE_ROWS
    block_expert = jnp.minimum(
        jnp.sum((ends[None, :] <= block_start[:, None]).astype(jnp.int32), axis=1), n_e - 1)
    n_used = (ends[-1:] // MOE_ROWS).astype(jnp.int32)
    y = _experts(h_rows, wg, wu, wd, layer, block_expert, n_used, row_tok, row_dst,
                 n_assign + 3 * MOE_ROWS, tm=MOE_ROWS, tf=MOE_COL_TILE)
    return _mix(x, gates, y, split_at)


def _rope_tables(pos):
    half = HEAD_DIM // 2
    inv_freq = 1.0 / (ROPE_THETA ** (jnp.arange(0, HEAD_DIM, 2, dtype=jnp.float32) / HEAD_DIM))
    ang = pos.astype(jnp.float32)[:, None] * inv_freq[None, :]
    cos, sin = jnp.cos(ang), jnp.sin(ang)
    reps = LANES // half
    signs = jnp.tile(jnp.concatenate([-jnp.ones((half,)), jnp.ones((half,))]), HEADS_PER_VREG)
    return jnp.tile(cos, (1, reps)), jnp.tile(sin, (1, reps)) * signs[None, :].astype(jnp.float32)


def _rel_bias(table, lq, n_before):
    n_keys = n_before + lq
    p = n_keys + lq
    rel = n_before + (lq - 1) - jnp.arange(p)
    diag = table.astype(jnp.float32)[:, jnp.clip(rel, REL_MIN, REL_MAX) - REL_MIN]
    skew = jnp.tile(diag, (1, lq))[:, :lq * (p - 1)].reshape(-1, lq, p - 1)
    return skew[:, :, lq - 1:lq - 1 + n_keys]


def _pair_major(a):
    h = a.shape[0]
    a = a.reshape((h // GROUP, PAIRS, HEADS_PER_VREG) + a.shape[1:])
    return jnp.swapaxes(a, 1, 2)


def _prompt_bias(table, n_prev):
    pad = n_prev * CHUNK
    b = _pair_major(_rel_bias(table, CHUNK, pad))
    kvh = b.shape[0]
    b = jnp.transpose(b, (0, 1, 4, 2, 3)).reshape(kvh * HEADS_PER_VREG, pad + CHUNK, LANES)
    return jnp.pad(b, ((0, 0), (0, pad), (0, 0)))


def _prompt_sink(sink):
    s = _pair_major(sink.astype(jnp.float32))
    s = jnp.repeat(s[..., None], CHUNK, axis=-1)
    return s.reshape(-1, 1, LANES)


def _sample_bias(table, lq, n_cached):
    b = _rel_bias(table, lq, n_cached)
    return b.reshape(b.shape[0] // GROUP, GROUP * lq, n_cached + lq)


def _sample_sink(sink, lq):
    h = sink.shape[0]
    s = jnp.repeat(sink.astype(jnp.float32).reshape(h // GROUP, GROUP, 1), lq, axis=2)
    return s.reshape(h // GROUP, GROUP * lq, 1)


def _head_gain(g):
    return jnp.tile(g.astype(jnp.float32), HEADS_PER_VREG)[None, :]


def kernel(x_prompt, x_sample, cache_k_a, cache_v_a, cache_k_b, cache_v_b, g_attn, g_ffn, w_qkv_a, g_q_a, g_k_a, sink_a, w_o_a, g_kv_b, w_kv_b, g_k_b, w_q_b, g_q_b, rel_bias_b, w_o_b, w_gate_ffn, w_up_ffn, w_down_ffn, w_router, b_router, w_gate_moe, w_up_moe, w_down_moe):
    bf16 = jnp.bfloat16
    nb, seq, d = x_prompt.shape
    ns, seq_s, _ = x_sample.shape
    tp, ts = nb * seq, ns * seq_s
    t = tp + ts
    depth = g_attn.shape[0]
    n_a = w_qkv_a.shape[0]
    n_kv = cache_k_b.shape[2]
    kvw = n_kv * HEAD_DIM
    keep_a, keep_b = min(WINDOW_A, seq), min(BAND_B, seq)
    assert t % ROW_TILE == 0 and t % FFN_ROWS == 0

    x = (x_prompt.reshape(tp, d), x_sample.reshape(ts, d))
    pos = jnp.concatenate([jnp.tile(jnp.arange(seq), nb),
                           jnp.tile(PAST_LEN + jnp.arange(seq_s), ns)])
    cos, sin = _rope_tables(pos)
    row = lambda v: v.astype(jnp.float32)[None, :]

    def attend(q, kv_pad, k_f32, v_f32, cache_k, cache_v, n_prev, table, sink):
        bias_p = None if table is None else _prompt_bias(table, n_prev)
        sink_p = None if sink is None else _prompt_sink(sink)
        o_p = _attend_prompt(q, *kv_pad, bias_p, sink_p, nb=nb, seq=seq, n_prev=n_prev)
        n_cached = cache_k.shape[1]
        ks = jnp.concatenate([cache_k.reshape(ns, n_cached, kvw),
                              k_f32[tp:].reshape(ns, seq_s, kvw)], axis=1).astype(bf16)
        vs = jnp.concatenate([cache_v.reshape(ns, n_cached, kvw),
                              v_f32[tp:].reshape(ns, seq_s, kvw)], axis=1).astype(bf16)
        bias_s = None if table is None else _sample_bias(table, seq_s, n_cached)
        sink_s = None if sink is None else _sample_sink(sink, seq_s)
        return o_p, _attend_sample(q, ks, vs, bias_s, sink_s, row0=tp, lq=seq_s)

    def tail(c, n_keep):
        rows = jnp.stack([c[(b + 1) * seq - n_keep:(b + 1) * seq] for b in range(nb)])
        return rows.reshape(nb, n_keep, n_kv, HEAD_DIM)

    def rolled(cache, c):
        new = c[tp:].reshape(ns, seq_s, n_kv, HEAD_DIM)
        return jnp.concatenate([cache, new], axis=1)[:, seq_s:]

    ffn_w = [w.astype(bf16)[:, None] for w in (w_gate_ffn, w_up_ffn, w_down_ffn)]
    moe_w = [w.astype(bf16) for w in (w_gate_moe, w_up_moe, w_down_moe)]

    ka_p, va_p, ka_s, va_s = [], [], [], []
    kvb_pad = kb_f = vb_f = None
    for layer in range(depth):
        g_in = row(g_attn[layer])
        if layer < n_a:
            i = layer
            q, kl, kh, vl, vh, k_f32, v_f32 = _project(
                x, g_in, w_qkv_a[i].astype(bf16), _head_gain(g_q_a[i]), _head_gain(g_k_a[i]),
                cos, sin, nq=(w_qkv_a.shape[2] - 2 * kvw) // COL_TILE, has_kv=True, rope=True)
            o = attend(q, (kl, kh, vl, vh), k_f32, v_f32, cache_k_a[i], cache_v_a[i],
                       PREV_CHUNKS_A, None, sink_a[i])
            x = _out_proj(o, w_o_a[i].astype(bf16), x)
            ka_p.append(tail(k_f32, keep_a))
            va_p.append(tail(v_f32, keep_a))
            ka_s.append(rolled(cache_k_a[i], k_f32))
            va_s.append(rolled(cache_v_a[i], v_f32))
        else:
            jb = layer - n_a
            (q,) = _project(x, g_in, w_q_b[jb].astype(bf16), _head_gain(g_q_b[jb]), None,
                            None, None, nq=w_q_b.shape[2] // COL_TILE, has_kv=False, rope=False)
            o = attend(q, kvb_pad, kb_f, vb_f, cache_k_b, cache_v_b, PREV_CHUNKS_B,
                       rel_bias_b[jb], None)
            x = _out_proj(o, w_o_b[jb].astype(bf16), x)

        g_mid = row(g_ffn[layer])
        m = layer // 2
        if layer % 2 == 0:
            n_tiles = t // FFN_ROWS
            x = _swiglu_blocks(x, g_mid, *ffn_w, m, jnp.zeros((n_tiles,), jnp.int32),
                               jnp.full((1,), n_tiles, jnp.int32), tm=FFN_ROWS, tf=COL_TILE)
        else:
            x = _moe(x, g_mid, w_router[m], b_router[m], *moe_w, m,
                     split_at=tp if layer == depth - 1 else None)

        if layer == n_a - 1:
            *kvb_pad, kb_f, vb_f = _project(x, row(g_kv_b), w_kv_b.astype(bf16), None,
                                            _head_gain(g_k_b), None, None,
                                            nq=0, has_kv=True, rope=False)

    if not isinstance(x, tuple):
        x = (x[:tp], x[tp:])
    y_prompt = x[0].reshape(nb, seq, d)
    y_sample = x[1].reshape(ns, seq_s, d)
    return (y_prompt, y_sample, jnp.stack(ka_p), jnp.stack(va_p), tail(kb_f, keep_b),
            tail(vb_f, keep_b), jnp.stack(ka_s), jnp.stack(va_s),
            rolled(cache_k_b, kb_f), rolled(cache_v_b, vb_f))
```

```python
import functools

import jax
import jax.numpy as jnp
from jax import lax
from jax.experimental import pallas as pl
from jax.experimental.pallas import tpu as pltpu

CHUNK = 64
HEAD_DIM = 64
GROUP = 4
WINDOW_A = 128
PREV_CHUNKS_A = WINDOW_A // CHUNK
PREV_CHUNKS_B = 8
BAND_B = PREV_CHUNKS_B * CHUNK
REL_MIN = -(CHUNK - 1)
REL_MAX = 256
ROPE_THETA = 10000.0
TOP_K = 2
RMS_EPS = 1e-6
NEG_INF = -1e30
LOG2_E = 1.4426950408889634
PAST_LEN = 1024

LANES = 128
HEADS_PER_VREG = LANES // HEAD_DIM
KV_GROUP_WIDTH = GROUP * HEAD_DIM
PAIRS = GROUP // HEADS_PER_VREG

ROW_TILE = 512
COL_TILE = 512
FFN_ROWS = 832
MOE_ROWS = 640
MOE_COL_TILE = 1024
COMBINE_ROWS = 256
ATTN_UNROLL_WIDE = 4
ATTN_UNROLL_NARROW = 8
DMA_THREADS = 2
VMEM_LIMIT = 56 * 1024 * 1024


def _params(semantics):
    return pltpu.CompilerParams(dimension_semantics=semantics, vmem_limit_bytes=VMEM_LIMIT)


def _rms_rows(x, g):
    ms = jnp.sum(x * x, axis=-1, keepdims=True) * (1.0 / x.shape[-1])
    return (x * lax.rsqrt(ms + RMS_EPS)) * g


def _head_norm(y, gain, cos, sin):
    lane = lax.broadcasted_iota(jnp.int32, y.shape, 1)
    low = lane < HEAD_DIM
    ss = y * y
    s_lo = jnp.sum(jnp.where(low, ss, 0.0), axis=-1, keepdims=True)
    s_hi = jnp.sum(jnp.where(low, 0.0, ss), axis=-1, keepdims=True)
    ms = jnp.where(low, s_lo, s_hi) * (1.0 / HEAD_DIM)
    yn = (y * lax.rsqrt(ms + RMS_EPS)) * gain
    if cos is None:
        return yn
    half = HEAD_DIM // 2
    fwd = pltpu.roll(yn, half, 1)
    bwd = pltpu.roll(yn, LANES - half, 1)
    rot = jnp.where((lane % HEAD_DIM) < half, bwd, fwd)
    return yn * cos + rot * sin


def _store_half_padded(lo_ref, hi_ref, c, z):
    lane = lax.broadcasted_iota(jnp.int32, z.shape, 1)
    low = lane < HEAD_DIM
    zr = pltpu.roll(z, HEAD_DIM, 1)
    dt = lo_ref.dtype
    a, b = 2 * c * LANES, (2 * c + 1) * LANES
    lo_ref[:, a:a + LANES] = jnp.where(low, z, 0.0).astype(dt)
    lo_ref[:, b:b + LANES] = jnp.where(low, zr, 0.0).astype(dt)
    hi_ref[:, a:a + LANES] = jnp.where(low, 0.0, zr).astype(dt)
    hi_ref[:, b:b + LANES] = jnp.where(low, 0.0, z).astype(dt)


def _row_specs(x, tm):
    def spec(cols, fn):
        return pl.BlockSpec((tm, cols), lambda i: (fn(i), 0))

    if not isinstance(x, tuple):
        return [spec(x.shape[1], lambda i: i)], 0
    head, tail = x
    assert head.shape[0] % tm == 0 and tail.shape[0] % tm == 0 and head.shape[1] == tail.shape[1]
    n_head = head.shape[0] // tm
    return [spec(head.shape[1], lambda i: jnp.minimum(i, n_head - 1)),
            spec(head.shape[1], lambda i: jnp.maximum(i - n_head, 0))], n_head


def _read_rows(x_refs, n_head):
    if len(x_refs) == 1:
        return x_refs[0][...]
    return jnp.where(pl.program_id(0) < n_head, x_refs[0][...], x_refs[1][...])


def _n_rows(x):
    return x[0].shape[0] + x[1].shape[0] if isinstance(x, tuple) else x.shape[0]


def _as_list(x):
    return list(x) if isinstance(x, tuple) else [x]


def _proj_kernel(*refs, nq, has_kv, rope, n_x, n_head):
    it = iter(refs)
    x_refs = [next(it) for _ in range(n_x)]
    g_ref, w_ref = next(it), next(it)
    gq_ref = next(it) if nq else None
    gk_ref = next(it) if has_kv else None
    cos_ref, sin_ref = (next(it), next(it)) if rope else (None, None)
    q_ref = next(it) if nq else None
    if has_kv:
        kl_ref, kh_ref, vl_ref, vh_ref, kf_ref, vf_ref = (next(it) for _ in range(6))
    h_scr = next(it)
    h_scr[...] = _rms_rows(_read_rows(x_refs, n_head), g_ref[...]).astype(h_scr.dtype)
    cos = cos_ref[...] if rope else None
    sin = sin_ref[...] if rope else None
    groups = COL_TILE // LANES
    for j in range(nq + (2 if has_kv else 0)):
        y = jnp.dot(h_scr[...], w_ref[j], preferred_element_type=jnp.float32)
        for c in range(groups):
            yc = y[:, c * LANES:(c + 1) * LANES]
            if j < nq:
                z = _head_norm(yc, gq_ref[...], cos, sin) * (HEAD_DIM ** -0.5 * LOG2_E)
                q_ref[:, (j * groups + c) * LANES:(j * groups + c + 1) * LANES] = z.astype(q_ref.dtype)
            elif j == nq:
                z = _head_norm(yc, gk_ref[...], cos, sin)
                kf_ref[:, c * LANES:(c + 1) * LANES] = z
                _store_half_padded(kl_ref, kh_ref, c, z)
            else:
                vf_ref[:, c * LANES:(c + 1) * LANES] = yc
                _store_half_padded(vl_ref, vh_ref, c, yc)


def _project(x, g, w, gq, gk, cos, sin, *, nq, has_kv, rope):
    t = _n_rows(x)
    d = w.shape[0]
    tm = ROW_TILE
    n_col = nq + (2 if has_kv else 0)
    assert w.shape == (d, n_col * COL_TILE) and t % tm == 0
    w = jnp.swapaxes(w.reshape(d, n_col, COL_TILE), 0, 1)
    row = lambda i: (i, 0)
    x_specs, n_head = _row_specs(x, tm)
    in_specs = x_specs + [pl.BlockSpec((1, d), lambda i: (0, 0)),
                          pl.BlockSpec((n_col, d, COL_TILE), lambda i: (0, 0, 0))]
    args = _as_list(x) + [g, w]
    vec = pl.BlockSpec((1, LANES), lambda i: (0, 0))
    if nq:
        in_specs.append(vec)
        args.append(gq)
    if has_kv:
        in_specs.append(vec)
        args.append(gk)
    if rope:
        in_specs += [pl.BlockSpec((tm, LANES), row)] * 2
        args += [cos, sin]
    out_shape, out_specs = [], []
    if nq:
        out_shape.append(jax.ShapeDtypeStruct((t, nq * COL_TILE), jnp.bfloat16))
        out_specs.append(pl.BlockSpec((tm, nq * COL_TILE), row))
    if has_kv:
        for _ in range(4):
            out_shape.append(jax.ShapeDtypeStruct((t, HEADS_PER_VREG * COL_TILE), jnp.bfloat16))
            out_specs.append(pl.BlockSpec((tm, HEADS_PER_VREG * COL_TILE), row))
        for _ in range(2):
            out_shape.append(jax.ShapeDtypeStruct((t, COL_TILE), jnp.float32))
            out_specs.append(pl.BlockSpec((tm, COL_TILE), row))
    return pl.pallas_call(
        functools.partial(_proj_kernel, nq=nq, has_kv=has_kv, rope=rope,
                          n_x=len(x_specs), n_head=n_head),
        grid=(t // tm,),
        in_specs=in_specs,
        out_specs=out_specs,
        out_shape=out_shape,
        scratch_shapes=[pltpu.VMEM((tm, d), jnp.bfloat16)],
        compiler_params=_params(("parallel",)),
        name="proj",
    )(*args)


def _attn_prompt_kernel(*refs, seq, band, n_prev, unroll, has_bias, has_sink):
    it = iter(refs)
    q_ref, kl_ref, kh_ref, vl_ref, vh_ref = (next(it) for _ in range(5))
    bias_ref = next(it) if has_bias else None
    sink_ref = next(it) if has_sink else None
    o_ref = next(it)
    s_scr = (next(it), next(it))
    p_scr = (next(it), next(it))
    n_scr = (next(it), next(it))
    row = lax.broadcasted_iota(jnp.int32, (band, LANES), 0)
    nt = (((1,), (1,)), ((), ()))
    tn = (((0,), (0,)), ((), ()))
    n_groups = seq // (CHUNK * unroll)
    n_masked = min(-(-n_prev // unroll), n_groups)

    def key_start(c):
        return pl.multiple_of(jnp.maximum(c - n_prev, 0) * CHUNK, CHUNK)

    def scores(gi, slot):
        for u in range(unroll):
            c = gi * unroll + u
            r0 = pl.multiple_of(c * CHUNK, CHUNK)
            k0 = key_start(c)
            qs = jnp.concatenate([q_ref[pl.ds(r0, CHUNK), p * LANES:(p + 1) * LANES]
                                  for p in range(PAIRS)], axis=0)
            for half, k_ref in enumerate((kl_ref, kh_ref)):
                s_scr[slot][u, half] = lax.dot_general(k_ref[pl.ds(k0, band), :], qs, nt,
                                                        preferred_element_type=jnp.float32)

    def softmax(gi, slot, masked):
        for u in range(unroll):
            c = gi * unroll + u
            shift = pl.multiple_of(jnp.maximum(n_prev - c, 0) * CHUNK, CHUNK) if masked else 0
            inv = []
            for half in range(HEADS_PER_VREG):
                s = s_scr[slot][u, half]
                if has_bias:
                    s = s + bias_ref[half, pl.ds(shift, band), :]
                if masked:
                    s = jnp.where(row < band - shift, s, NEG_INF)
                m = jnp.max(s, axis=0, keepdims=True)
                if has_sink:
                    m = jnp.maximum(m, sink_ref[half])
                e = jnp.exp2(s - m)
                l = jnp.sum(e, axis=0, keepdims=True)
                if has_sink:
                    l = l + jnp.exp2(sink_ref[half] - m)
                p_scr[slot][u, half] = e.astype(p_scr[slot].dtype)
                inv.append(jnp.broadcast_to(1.0 / l, (LANES, LANES)))
            n_scr[slot][u] = jnp.where(row[:LANES] < HEAD_DIM, inv[0], inv[1])

    def outputs(gi, slot):
        for u in range(unroll):
            c = gi * unroll + u
            r0 = pl.multiple_of(c * CHUNK, CHUNK)
            k0 = key_start(c)
            o = (lax.dot_general(p_scr[slot][u, 0], vl_ref[pl.ds(k0, band), :], tn,
                                 preferred_element_type=jnp.float32)
                 + lax.dot_general(p_scr[slot][u, 1], vh_ref[pl.ds(k0, band), :], tn,
                                   preferred_element_type=jnp.float32))
            o = o * n_scr[slot][u].T
            for p in range(PAIRS):
                o_ref[pl.ds(r0, CHUNK), p * LANES:(p + 1) * LANES] = (
                    o[p * CHUNK:(p + 1) * CHUNK].astype(o_ref.dtype))

    def stage(g, parity, masked):
        static = isinstance(g, int)
        if not static or g < n_groups:
            scores(g, parity)
        if not static or 0 <= g - 1 < n_groups:
            softmax(g - 1, 1 - parity, masked)
        if not static or 0 <= g - 2 < n_groups:
            outputs(g - 2, parity)

    first_loop = max(n_masked + 1, 2)
    first_loop += first_loop % 2
    n_pairs = max(n_groups - first_loop, 0) // 2
    for g in range(min(first_loop, n_groups)):
        stage(g, g % 2, masked=True)
    if n_pairs:
        def body(k, carry):
            g = first_loop + 2 * k
            stage(g, 0, masked=False)
            stage(g + 1, 1, masked=False)
            return carry
        lax.fori_loop(0, n_pairs, body, 0)
    for g in range(min(first_loop, n_groups) + 2 * n_pairs, n_groups + 2):
        stage(g, g % 2, masked=(g - 1) < n_masked)


def _attend_prompt(q, kl, kh, vl, vh, bias, sink, *, nb, seq, n_prev):
    qw = q.shape[1]
    n_kv = kl.shape[1] // LANES
    band = (n_prev + 1) * CHUNK
    unroll = ATTN_UNROLL_WIDE if n_prev > PREV_CHUNKS_A else ATTN_UNROLL_NARROW
    assert seq % (CHUNK * unroll) == 0 and seq >= band and seq >= 2 * CHUNK * unroll
    kv_spec = pl.BlockSpec((seq, LANES), lambda b, j: (b, j))
    in_specs = [pl.BlockSpec((seq, KV_GROUP_WIDTH), lambda b, j: (b, j))] + [kv_spec] * 4
    args = [q, kl, kh, vl, vh]
    if bias is not None:
        in_specs.append(pl.BlockSpec((HEADS_PER_VREG,) + bias.shape[1:], lambda b, j: (j, 0, 0)))
        args.append(bias)
    if sink is not None:
        in_specs.append(pl.BlockSpec((HEADS_PER_VREG, 1, LANES), lambda b, j: (j, 0, 0)))
        args.append(sink)
    return pl.pallas_call(
        functools.partial(_attn_prompt_kernel, seq=seq, band=band, n_prev=n_prev, unroll=unroll,
                          has_bias=bias is not None, has_sink=sink is not None),
        grid=(nb, n_kv),
        in_specs=in_specs,
        out_specs=pl.BlockSpec((seq, KV_GROUP_WIDTH), lambda b, j: (b, j)),
        out_shape=jax.ShapeDtypeStruct((nb * seq, qw), jnp.bfloat16),
        scratch_shapes=[pltpu.VMEM((unroll, HEADS_PER_VREG, band, LANES), jnp.float32)] * 2
        + [pltpu.VMEM((unroll, HEADS_PER_VREG, band, LANES), jnp.bfloat16)] * 2
        + [pltpu.VMEM((unroll, LANES, LANES), jnp.float32)] * 2,
        compiler_params=_params(("parallel", "arbitrary")),
        name="attn_prompt",
    )(*args)


def _attn_sample_kernel(*refs, lq, has_bias, has_sink, n_kv):
    it = iter(refs)
    q_ref, k_ref, v_ref = next(it), next(it), next(it)
    bias_ref = next(it) if has_bias else None
    sink_ref = next(it) if has_sink else None
    o_ref = next(it)
    for j in range(n_kv):
        qj = q_ref[:, j * KV_GROUP_WIDTH:(j + 1) * KV_GROUP_WIDTH]
        qcat = jnp.concatenate(
            [qj[:, g * HEAD_DIM:(g + 1) * HEAD_DIM] for g in range(GROUP)], axis=0)
        kj = k_ref[0, :, j * HEAD_DIM:(j + 1) * HEAD_DIM]
        vj = v_ref[0, :, j * HEAD_DIM:(j + 1) * HEAD_DIM]
        s = lax.dot_general(qcat, kj, (((1,), (1,)), ((), ())),
                            preferred_element_type=jnp.float32)
        if has_bias:
            s = s + bias_ref[j]
        m = jnp.max(s, axis=-1, keepdims=True)
        if has_sink:
            sink = sink_ref[j]
            m = jnp.maximum(m, sink)
        e = jnp.exp2(s - m)
        l = jnp.sum(e, axis=-1, keepdims=True)
        if has_sink:
            l = l + jnp.exp2(sink - m)
        p = (e * (1.0 / l)).astype(vj.dtype)
        o = jnp.dot(p, vj, preferred_element_type=jnp.float32)
        ocat = jnp.concatenate([o[g * lq:(g + 1) * lq] for g in range(GROUP)], axis=1)
        o_ref[:, j * KV_GROUP_WIDTH:(j + 1) * KV_GROUP_WIDTH] = ocat.astype(o_ref.dtype)


def _attend_sample(q, k, v, bias, sink, *, row0, lq):
    ns, lk, kw = k.shape
    qw = q.shape[1]
    assert row0 % lq == 0
    blk0 = row0 // lq
    in_specs = [pl.BlockSpec((lq, qw), lambda i: (blk0 + i, 0)),
                pl.BlockSpec((1, lk, kw), lambda i: (i, 0, 0)),
                pl.BlockSpec((1, lk, kw), lambda i: (i, 0, 0))]
    args = [q, k, v]
    if bias is not None:
        in_specs.append(pl.BlockSpec(bias.shape, lambda i: (0, 0, 0)))
        args.append(bias)
    if sink is not None:
        in_specs.append(pl.BlockSpec(sink.shape, lambda i: (0, 0, 0)))
        args.append(sink)
    return pl.pallas_call(
        functools.partial(_attn_sample_kernel, lq=lq, has_bias=bias is not None,
                          has_sink=sink is not None, n_kv=kw // HEAD_DIM),
        grid=(ns,),
        in_specs=in_specs,
        out_specs=pl.BlockSpec((lq, qw), lambda i: (i, 0)),
        out_shape=jax.ShapeDtypeStruct((ns * lq, qw), jnp.bfloat16),
        compiler_params=_params(("arbitrary",)),
        name="attn_sample",
    )(*args)


def _out_proj_kernel(*refs, n_o, o_head, n_x, x_head):
    o_refs, w_ref = refs[:n_o], refs[n_o]
    x_refs, y_ref = refs[n_o + 1:n_o + 1 + n_x], refs[n_o + 1 + n_x]
    y_ref[...] = _read_rows(x_refs, x_head) + jnp.dot(
        _read_rows(o_refs, o_head), w_ref[...], preferred_element_type=jnp.float32)


def _out_proj(o, w, x):
    t = _n_rows(x)
    d = w.shape[1]
    tm = ROW_TILE
    o_specs, o_head = _row_specs(o, tm)
    x_specs, x_head = _row_specs(x, tm)
    return pl.pallas_call(
        functools.partial(_out_proj_kernel, n_o=len(o_specs), o_head=o_head,
                          n_x=len(x_specs), x_head=x_head),
        grid=(t // tm,),
        in_specs=o_specs + [pl.BlockSpec(w.shape, lambda i: (0, 0))] + x_specs,
        out_specs=pl.BlockSpec((tm, d), lambda i: (i, 0)),
        out_shape=jax.ShapeDtypeStruct((t, d), jnp.float32),
        compiler_params=_params(("parallel",)),
        name="out_proj",
    )(*_as_list(o), w, *_as_list(x))


def _swiglu_stage(h_scr, act_ref, wg_ref, wu_ref):
    h = h_scr[...]
    gate = jnp.dot(h, wg_ref[...], preferred_element_type=jnp.float32)
    up = jnp.dot(h, wu_ref[...], preferred_element_type=jnp.float32)
    act_ref[...] = ((gate * jax.nn.sigmoid(gate)) * up).astype(act_ref.dtype)


def _ffn_kernel(e_ref, n_ref, x_ref, g_ref, wg_ref, wu_ref, wd_ref, y_ref, h_scr, acc_scr,
                act0, act1, *, n_f):
    s = pl.program_id(1)
    act = (act0, act1)

    @pl.when(pl.program_id(0) < n_ref[0])
    def _():
        @pl.when(s == 0)
        def _():
            h_scr[...] = _rms_rows(x_ref[...], g_ref[...]).astype(h_scr.dtype)
            acc_scr[...] = jnp.zeros_like(acc_scr)
            _swiglu_stage(h_scr, act[0], wg_ref, wu_ref)

        for parity in range(2):
            @pl.when((s > 0) & (s < n_f) & (s % 2 == parity))
            def _():
                acc_scr[...] += jnp.dot(act[1 - parity][...], wd_ref[...],
                                        preferred_element_type=jnp.float32)
                _swiglu_stage(h_scr, act[parity], wg_ref, wu_ref)

        @pl.when(s == n_f)
        def _():
            part = jnp.dot(act[(n_f - 1) % 2][...], wd_ref[...],
                           preferred_element_type=jnp.float32)
            y_ref[...] = x_ref[...] + (acc_scr[...] + part)

    @pl.when((pl.program_id(0) >= n_ref[0]) & (s == n_f))
    def _():
        y_ref[...] = jnp.zeros_like(y_ref)


def _swiglu_blocks(x, g, wg, wu, wd, layer, block_expert, n_used, *, tm, tf):
    r, d = x.shape
    ff = wg.shape[3]
    assert r % tm == 0 and ff % tf == 0
    n_f = ff // tf

    def blk(i, n):
        return jnp.minimum(i, n[0] - 1)

    def up_col(i, s, n):
        return jnp.where(i < n[0], jnp.minimum(s, n_f - 1), n_f - 1)

    def down_row(i, s, n):
        return jnp.where(i < n[0], jnp.maximum(s - 1, 0), n_f - 1)

    grid_spec = pltpu.PrefetchScalarGridSpec(
        num_scalar_prefetch=2,
        grid=(r // tm, n_f + 1),
        in_specs=[pl.BlockSpec((tm, d), lambda i, s, e, n: (blk(i, n), 0)),
                  pl.BlockSpec((1, d), lambda i, s, e, n: (0, 0)),
                  pl.BlockSpec((None, None, d, tf),
                               lambda i, s, e, n: (layer, e[blk(i, n)], 0, up_col(i, s, n))),
                  pl.BlockSpec((None, None, d, tf),
                               lambda i, s, e, n: (layer, e[blk(i, n)], 0, up_col(i, s, n))),
                  pl.BlockSpec((None, None, tf, d),
                               lambda i, s, e, n: (layer, e[blk(i, n)], down_row(i, s, n), 0))],
        out_specs=pl.BlockSpec((tm, d), lambda i, s, e, n: (i, 0)),
        scratch_shapes=[pltpu.VMEM((tm, d), jnp.bfloat16),
                        pltpu.VMEM((tm, d), jnp.float32),
                        pltpu.VMEM((tm, tf), jnp.bfloat16),
                        pltpu.VMEM((tm, tf), jnp.bfloat16)],
    )
    return pl.pallas_call(
        functools.partial(_ffn_kernel, n_f=n_f),
        grid_spec=grid_spec,
        out_shape=jax.ShapeDtypeStruct((r, d), jnp.float32),
        compiler_params=_params(("arbitrary", "arbitrary")),
        name="swiglu",
    )(block_expert, n_used, x, g, wg, wu, wd)


def _router_kernel(x_ref, g_ref, w_ref, b_ref, idx_ref, gate_ref, h_ref):
    h = _rms_rows(x_ref[...], g_ref[...]).astype(jnp.bfloat16)
    h_ref[...] = h.reshape(h_ref.shape)
    logits = jnp.dot(h, w_ref[...], preferred_element_type=jnp.float32) + b_ref[...]
    lane = lax.broadcasted_iota(jnp.int32, logits.shape, 1)
    m1 = jnp.max(logits, axis=-1, keepdims=True)
    e1 = jnp.min(jnp.where(logits == m1, lane, LANES), axis=-1, keepdims=True)
    rest = jnp.where(lane == e1, -jnp.inf, logits)
    m2 = jnp.max(rest, axis=-1, keepdims=True)
    e2 = jnp.min(jnp.where(rest == m2, lane, LANES), axis=-1, keepdims=True)
    t = jnp.exp(m2 - m1)
    denom = 1.0 + t
    idx_ref[...] = jnp.where(lane == 0, e1, jnp.where(lane == 1, e2, 0))
    gate_ref[...] = jnp.where(lane == 0, 1.0 / denom, jnp.where(lane == 1, t / denom, 0.0))


def _route(x, g, w_router, b_router):
    t, d = x.shape
    tm = ROW_TILE
    n_e = w_router.shape[1]
    w = jnp.zeros((d, LANES), jnp.bfloat16).at[:, :n_e].set(w_router.astype(jnp.bfloat16))
    b = jnp.full((1, LANES), -jnp.inf, jnp.float32).at[0, :n_e].set(b_router.astype(jnp.float32))
    return pl.pallas_call(
        _router_kernel,
        grid=(t // tm,),
        in_specs=[pl.BlockSpec((tm, d), lambda i: (i, 0)),
                  pl.BlockSpec((1, d), lambda i: (0, 0)),
                  pl.BlockSpec((d, LANES), lambda i: (0, 0)),
                  pl.BlockSpec((1, LANES), lambda i: (0, 0))],
        out_specs=[pl.BlockSpec((tm, LANES), lambda i: (i, 0))] * 2
        + [pl.BlockSpec((tm, d // LANES, LANES), lambda i: (i, 0, 0))],
        out_shape=[jax.ShapeDtypeStruct((t, LANES), jnp.int32),
                   jax.ShapeDtypeStruct((t, LANES), jnp.float32),
                   jax.ShapeDtypeStruct((t, d // LANES, LANES), jnp.bfloat16)],
        compiler_params=_params(("parallel",)),
        name="router",
    )(x, g, w, b)


def _experts_kernel(e_ref, n_ref, first_ref, next_ref, prev_dst_ref, dst_ref, h_hbm,
                    wg_ref, wu_ref, wd_ref, y_hbm, xbuf, ybuf, h_scr, acc_scr, act0, act1,
                    gsem, ssem, *, tm, n_f):
    i, s = pl.program_id(0), pl.program_id(1)
    act = (act0, act1)
    n_used = n_ref[0]
    slot = i % 2
    other = 1 - slot
    share = tm // n_f
    unroll = 2 * DMA_THREADS

    def gather_copy(idx_ref, r, to_slot):
        return pltpu.make_async_copy(h_hbm.at[idx_ref[r]], xbuf.at[to_slot, r], gsem.at[to_slot])

    def scatter_copy(idx_ref, r, from_slot):
        return pltpu.make_async_copy(ybuf.at[from_slot, r], y_hbm.at[idx_ref[r]],
                                     ssem.at[from_slot])

    def issue_all(copy_of_row):
        def issue(r4, carry):
            for k in range(unroll):
                copy_of_row(r4 * unroll + k).start(priority=k % DMA_THREADS)
            return carry
        lax.fori_loop(0, tm // unroll, issue, 0)

    def wait_gather(s_):
        pltpu.make_async_copy(h_hbm.at[pl.ds(0, tm)], xbuf.at[s_], gsem.at[s_]).wait()

    def wait_scatter(s_):
        pltpu.make_async_copy(ybuf.at[s_], y_hbm.at[pl.ds(0, tm)], ssem.at[s_]).wait()

    def row_copies(step):
        for k in range(share):
            r = step * share + k
            gather_copy(next_ref, r, other).start(priority=1)
            scatter_copy(prev_dst_ref, r, other).start(priority=1)

    @pl.when(i < n_used)
    def _():
        @pl.when(s == 0)
        def _():
            @pl.when(i == 0)
            def _():
                issue_all(lambda r: gather_copy(first_ref, r, 0))
                ybuf[1] = jnp.zeros(ybuf.shape[1:], ybuf.dtype)
                for half in range(2):
                    spare = pltpu.make_async_copy(
                        ybuf.at[1], y_hbm.at[pl.ds(y_hbm.shape[0] - (3 - half) * tm, tm)],
                        ssem.at[1])
                    spare.start()
                    spare.wait()

            wait_gather(slot)
            h_scr[...] = xbuf[slot].reshape(h_scr.shape)
            acc_scr[...] = jnp.zeros_like(acc_scr)
            row_copies(0)
            _swiglu_stage(h_scr, act[0], wg_ref, wu_ref)

        for parity in range(2):
            @pl.when((s > 0) & (s < n_f) & (s % 2 == parity))
            def _():
                row_copies(s)
                acc_scr[...] += jnp.dot(act[1 - parity][...], wd_ref[...],
                                        preferred_element_type=jnp.float32)
                _swiglu_stage(h_scr, act[parity], wg_ref, wu_ref)

        @pl.when(s == n_f)
        def _():
            part = jnp.dot(act[(n_f - 1) % 2][...], wd_ref[...],
                           preferred_element_type=jnp.float32)

            @pl.when(i >= 1)
            def _():
                wait_scatter(slot)

            ybuf[slot] = (acc_scr[...] + part).reshape(ybuf.shape[1:])

            @pl.when(i == n_used - 1)
            def _():
                issue_all(lambda r: scatter_copy(dst_ref, r, slot))
                wait_scatter(slot)
                wait_scatter(other)
                wait_gather(other)


def _experts(h_rows, wg, wu, wd, layer, block_expert, n_used, row_tok, row_dst, n_out, *, tm, tf):
    tile = h_rows.shape[1:]
    d, ff = wg.shape[2], wg.shape[3]
    rows = row_tok.shape[0]
    assert rows % tm == 0 and row_dst.shape[0] == rows + tm and ff % tf == 0
    row_tok, row_dst = _index_blocks(row_tok, tm), _index_blocks(row_dst, tm)
    n_f = ff // tf
    assert tm % (n_f * 2 * DMA_THREADS) == 0

    def blk(i, n):
        return jnp.minimum(i, n[0] - 1)

    def up_col(i, s, n):
        return jnp.where(i < n[0], jnp.minimum(s, n_f - 1), n_f - 1)

    def down_row(i, s, n):
        return jnp.where(i < n[0], jnp.maximum(s - 1, 0), n_f - 1)

    smem = functools.partial(pl.BlockSpec, memory_space=pltpu.SMEM)
    grid_spec = pltpu.PrefetchScalarGridSpec(
        num_scalar_prefetch=2,
        grid=(rows // tm, n_f + 1),
        in_specs=[smem((None, None, tm), lambda i, f, e, n: (0, 0, 0)),
                  smem((None, None, tm), lambda i, f, e, n: (blk(i + 1, n), 0, 0)),
                  smem((None, None, tm), lambda i, f, e, n: (blk(i, n), 0, 0)),
                  smem((None, None, tm), lambda i, f, e, n: (blk(i, n) + 1, 0, 0)),
                  pl.BlockSpec(memory_space=pl.ANY),
                  pl.BlockSpec((None, None, d, tf),
                               lambda i, f, e, n: (layer, e[blk(i, n)], 0, up_col(i, f, n))),
                  pl.BlockSpec((None, None, d, tf),
                               lambda i, f, e, n: (layer, e[blk(i, n)], 0, up_col(i, f, n))),
                  pl.BlockSpec((None, None, tf, d),
                               lambda i, f, e, n: (layer, e[blk(i, n)], down_row(i, f, n), 0))],
        out_specs=pl.BlockSpec(memory_space=pl.ANY),
        scratch_shapes=[pltpu.VMEM((2, tm) + tile, jnp.bfloat16),
                        pltpu.VMEM((2, tm) + tile, jnp.float32),
                        pltpu.VMEM((tm, d), jnp.bfloat16),
                        pltpu.VMEM((tm, d), jnp.float32),
                        pltpu.VMEM((tm, tf), jnp.bfloat16),
                        pltpu.VMEM((tm, tf), jnp.bfloat16),
                        pltpu.SemaphoreType.DMA((2,)),
                        pltpu.SemaphoreType.DMA((2,))],
    )
    return pl.pallas_call(
        functools.partial(_experts_kernel, tm=tm, n_f=n_f),
        grid_spec=grid_spec,
        out_shape=jax.ShapeDtypeStruct((n_out,) + tile, jnp.float32),
        compiler_params=_params(("arbitrary", "arbitrary")),
        name="experts",
    )(block_expert, n_used, row_tok, row_tok, row_dst, row_dst, h_rows, wg, wu, wd)


def _index_blocks(idx, tm):
    return idx.reshape(-1, 1, tm)


def _mix_kernel(x_ref, gate_ref, y0_ref, y1_ref, *out_refs, n_head):
    gates = gate_ref[...]
    shape = x_ref.shape
    y = gates[:, 0:1] * y0_ref[...].reshape(shape) + gates[:, 1:2] * y1_ref[...].reshape(shape)
    out = x_ref[...] + y
    if len(out_refs) == 1:
        out_refs[0][...] = out
    else:
        @pl.when(pl.program_id(0) < n_head)
        def _():
            out_refs[0][...] = out

        @pl.when(pl.program_id(0) >= n_head)
        def _():
            out_refs[1][...] = out


def _mix(x, gates, y, split_at=None):
    t, d = x.shape
    rows = COMBINE_ROWS
    tile = y.shape[1:]
    assert t % rows == 0
    n_t = t // rows
    if split_at is None:
        n_head = 0
        out_specs = pl.BlockSpec((rows, d), lambda i: (i, 0))
        out_shape = jax.ShapeDtypeStruct((t, d), jnp.float32)
    else:
        assert split_at % rows == 0 and 0 < split_at < t
        n_head = split_at // rows
        out_specs = [pl.BlockSpec((rows, d), lambda i: (jnp.minimum(i, n_head - 1), 0)),
                     pl.BlockSpec((rows, d), lambda i: (jnp.maximum(i - n_head, 0), 0))]
        out_shape = [jax.ShapeDtypeStruct((split_at, d), jnp.float32),
                     jax.ShapeDtypeStruct((t - split_at, d), jnp.float32)]
    return pl.pallas_call(
        functools.partial(_mix_kernel, n_head=n_head),
        grid=(n_t,),
        in_specs=[pl.BlockSpec((rows, d), lambda i: (i, 0)),
                  pl.BlockSpec((rows, LANES), lambda i: (i, 0)),
                  pl.BlockSpec((rows,) + tile, lambda i: (i, 0, 0)),
                  pl.BlockSpec((rows,) + tile, lambda i: (n_t + i, 0, 0))],
        out_specs=out_specs,
        out_shape=out_shape,
        compiler_params=_params(("arbitrary",)),
        name="mix",
    )(x, gates, y, y)


def _moe(x, g, w_router, b_router, wg, wu, wd, layer, split_at=None):
    t, d = x.shape
    n_e = wg.shape[1]
    n_assign = t * TOP_K
    top_idx, gates, h_rows = _route(x, g, w_router, b_router)
    flat_e = top_idx[:, :TOP_K].reshape(-1)
    onehot = (flat_e[:, None] == jnp.arange(n_e, dtype=jnp.int32)[None, :]).astype(jnp.int32)
    rank = jnp.sum((jnp.cumsum(onehot, axis=0) - onehot) * onehot, axis=1)
    counts = jnp.sum(onehot, axis=0)
    padded = (counts + MOE_ROWS - 1) // MOE_ROWS * MOE_ROWS
    ends = jnp.cumsum(padded)
    dest = jnp.sum((ends - padded)[None, :] * onehot, axis=1) + rank
    n_blocks = (n_assign + n_e * (MOE_ROWS - 1) + MOE_ROWS - 1) // MOE_ROWS
    rows = n_blocks * MOE_ROWS
    row_assign = jnp.full((rows,), -1, jnp.int32).at[dest].set(jnp.arange(n_assign, dtype=jnp.int32))
    real = row_assign >= 0
    row_tok = jnp.where(real, row_assign // TOP_K, 0)
    spare = n_assign + (jnp.arange(rows, dtype=jnp.int32) // MOE_ROWS % 2) * MOE_ROWS \
        + jnp.arange(rows, dtype=jnp.int32) % MOE_ROWS
    row_dst = jnp.where(real, (row_assign % TOP_K) * t + row_assign // TOP_K, spare)
    row_dst = jnp.concatenate(
        [n_assign + 2 * MOE_ROWS + jnp.arange(MOE_ROWS, dtype=jnp.int32), row_dst])
    block_start = jnp.arange(n_blocks, dtype=jnp.int32) * MOE_ROWS
    block_expert = jnp.minimum(
        jnp.sum((ends[None, :] <= block_start[:, None]).astype(jnp.int32), axis=1), n_e - 1)
    n_used = (ends[-1:] // MOE_ROWS).astype(jnp.int32)
    y = _experts(h_rows, wg, wu, wd, layer, block_expert, n_used, row_tok, row_dst,
                 n_assign + 3 * MOE_ROWS, tm=MOE_ROWS, tf=MOE_COL_TILE)
    return _mix(x, gates, y, split_at)


def _rope_tables(pos):
    half = HEAD_DIM // 2
    inv_freq = 1.0 / (ROPE_THETA ** (jnp.arange(0, HEAD_DIM, 2, dtype=jnp.float32) / HEAD_DIM))
    ang = pos.astype(jnp.float32)[:, None] * inv_freq[None, :]
    cos, sin = jnp.cos(ang), jnp.sin(ang)
    reps = LANES // half
    signs = jnp.tile(jnp.concatenate([-jnp.ones((half,)), jnp.ones((half,))]), HEADS_PER_VREG)
    return jnp.tile(cos, (1, reps)), jnp.tile(sin, (1, reps)) * signs[None, :].astype(jnp.float32)


def _rel_bias(table, lq, n_before):
    n_keys = n_before + lq
    p = n_keys + lq
    rel = n_before + (lq - 1) - jnp.arange(p)
    diag = table.astype(jnp.float32)[:, jnp.clip(rel, REL_MIN, REL_MAX) - REL_MIN]
    skew = jnp.tile(diag, (1, lq))[:, :lq * (p - 1)].reshape(-1, lq, p - 1)
    return skew[:, :, lq - 1:lq - 1 + n_keys]


def _pair_major(a):
    h = a.shape[0]
    a = a.reshape((h // GROUP, PAIRS, HEADS_PER_VREG) + a.shape[1:])
    return jnp.swapaxes(a, 1, 2)


def _prompt_bias(table, n_prev):
    pad = n_prev * CHUNK
    b = _pair_major(_rel_bias(table, CHUNK, pad))
    kvh = b.shape[0]
    b = jnp.transpose(b, (0, 1, 4, 2, 3)).reshape(kvh * HEADS_PER_VREG, pad + CHUNK, LANES)
    return jnp.pad(b, ((0, 0), (0, pad), (0, 0)))


def _prompt_sink(sink):
    s = _pair_major(sink.astype(jnp.float32))
    s = jnp.repeat(s[..., None], CHUNK, axis=-1)
    return s.reshape(-1, 1, LANES)


def _sample_bias(table, lq, n_cached):
    b = _rel_bias(table, lq, n_cached)
    return b.reshape(b.shape[0] // GROUP, GROUP * lq, n_cached + lq)


def _sample_sink(sink, lq):
    h = sink.shape[0]
    s = jnp.repeat(sink.astype(jnp.float32).reshape(h // GROUP, GROUP, 1), lq, axis=2)
    return s.reshape(h // GROUP, GROUP * lq, 1)


def _head_gain(g):
    return jnp.tile(g.astype(jnp.float32), HEADS_PER_VREG)[None, :]


def kernel(x_prompt, x_sample, cache_k_a, cache_v_a, cache_k_b, cache_v_b, g_attn, g_ffn, w_qkv_a, g_q_a, g_k_a, sink_a, w_o_a, g_kv_b, w_kv_b, g_k_b, w_q_b, g_q_b, rel_bias_b, w_o_b, w_gate_ffn, w_up_ffn, w_down_ffn, w_router, b_router, w_gate_moe, w_up_moe, w_down_moe):
    bf16 = jnp.bfloat16
    nb, seq, d = x_prompt.shape
    ns, seq_s, _ = x_sample.shape
    tp, ts = nb * seq, ns * seq_s
    t = tp + ts
    depth = g_attn.shape[0]
    n_a = w_qkv_a.shape[0]
    n_kv = cache_k_b.shape[2]
    kvw = n_kv * HEAD_DIM
    keep_a, keep_b = min(WINDOW_A, seq), min(BAND_B, seq)
    assert t % ROW_TILE == 0 and t % FFN_ROWS == 0

    x = (x_prompt.reshape(tp, d), x_sample.reshape(ts, d))
    pos = jnp.concatenate([jnp.tile(jnp.arange(seq), nb),
                           jnp.tile(PAST_LEN + jnp.arange(seq_s), ns)])
    cos, sin = _rope_tables(pos)
    row = lambda v: v.astype(jnp.float32)[None, :]

    def attend(q, kv_pad, k_f32, v_f32, cache_k, cache_v, n_prev, table, sink):
        table = None if table is None else table.astype(jnp.float32) * LOG2_E
        sink = None if sink is None else sink.astype(jnp.float32) * LOG2_E
        bias_p = None if table is None else _prompt_bias(table, n_prev)
        sink_p = None if sink is None else _prompt_sink(sink)
        o_p = _attend_prompt(q, *kv_pad, bias_p, sink_p, nb=nb, seq=seq, n_prev=n_prev)
        n_cached = cache_k.shape[1]
        ks = jnp.concatenate([cache_k.reshape(ns, n_cached, kvw),
                              k_f32[tp:].reshape(ns, seq_s, kvw)], axis=1).astype(bf16)
        vs = jnp.concatenate([cache_v.reshape(ns, n_cached, kvw),
                              v_f32[tp:].reshape(ns, seq_s, kvw)], axis=1).astype(bf16)
        bias_s = None if table is None else _sample_bias(table, seq_s, n_cached)
        sink_s = None if sink is None else _sample_sink(sink, seq_s)
        return o_p, _attend_sample(q, ks, vs, bias_s, sink_s, row0=tp, lq=seq_s)

    def tail(c, n_keep):
        rows = jnp.stack([c[(b + 1) * seq - n_keep:(b + 1) * seq] for b in range(nb)])
        return rows.reshape(nb, n_keep, n_kv, HEAD_DIM)

    def rolled(cache, c):
        new = c[tp:].reshape(ns, seq_s, n_kv, HEAD_DIM)
        return jnp.concatenate([cache, new], axis=1)[:, seq_s:]

    ffn_w = [w.astype(bf16)[:, None] for w in (w_gate_ffn, w_up_ffn, w_down_ffn)]
    moe_w = [w.astype(bf16) for w in (w_gate_moe, w_up_moe, w_down_moe)]

    ka_p, va_p, ka_s, va_s = [], [], [], []
    kvb_pad = kb_f = vb_f = None
    for layer in range(depth):
        g_in = row(g_attn[layer])
        if layer < n_a:
            i = layer
            q, kl, kh, vl, vh, k_f32, v_f32 = _project(
                x, g_in, w_qkv_a[i].astype(bf16), _head_gain(g_q_a[i]), _head_gain(g_k_a[i]),
                cos, sin, nq=(w_qkv_a.shape[2] - 2 * kvw) // COL_TILE, has_kv=True, rope=True)
            o = attend(q, (kl, kh, vl, vh), k_f32, v_f32, cache_k_a[i], cache_v_a[i],
                       PREV_CHUNKS_A, None, sink_a[i])
            x = _out_proj(o, w_o_a[i].astype(bf16), x)
            ka_p.append(tail(k_f32, keep_a))
            va_p.append(tail(v_f32, keep_a))
            ka_s.append(rolled(cache_k_a[i], k_f32))
            va_s.append(rolled(cache_v_a[i], v_f32))
        else:
            jb = layer - n_a
            (q,) = _project(x, g_in, w_q_b[jb].astype(bf16), _head_gain(g_q_b[jb]), None,
                            None, None, nq=w_q_b.shape[2] // COL_TILE, has_kv=False, rope=False)
            o = attend(q, kvb_pad, kb_f, vb_f, cache_k_b, cache_v_b, PREV_CHUNKS_B,
                       rel_bias_b[jb], None)
            x = _out_proj(o, w_o_b[jb].astype(bf16), x)

        g_mid = row(g_ffn[layer])
        m = layer // 2
        if layer % 2 == 0:
            n_tiles = t // FFN_ROWS
            x = _swiglu_blocks(x, g_mid, *ffn_w, m, jnp.zeros((n_tiles,), jnp.int32),
                               jnp.full((1,), n_tiles, jnp.int32), tm=FFN_ROWS, tf=COL_TILE)
        else:
            x = _moe(x, g_mid, w_router[m], b_router[m], *moe_w, m,
                     split_at=tp if layer == depth - 1 else None)

        if layer == n_a - 1:
            *kvb_pad, kb_f, vb_f = _project(x, row(g_kv_b), w_kv_b.astype(bf16), None,
                                            _head_gain(g_k_b), None, None,
                                            nq=0, has_kv=True, rope=False)

    if not isinstance(x, tuple):
        x = (x[:tp], x[tp:])
    y_prompt = x[0].reshape(nb, seq, d)
    y_sample = x[1].reshape(ns, seq_s, d)
    return (y_prompt, y_sample, jnp.stack(ka_p), jnp.stack(va_p), tail(kb_f, keep_b),
            tail(vb_f, keep_b), jnp.stack(ka_s), jnp.stack(va_s),
            rolled(cache_k_b, kb_f), rolled(cache_v_b, vb_f))
```
